```python
import math
import jax, jax.numpy as jnp
from jax import lax
import numpy as np

D_MODEL = 1024
BATCH = 16
SEQ = 256
DEPTH = 2
DEC_BATCH = 8
DEC_SEQ = 1024
PAST_LEN = 256

GRID_W = 64
HEAD_DIM = 64
ROPE_THETA = 10000.0
ROPE_FREQS = HEAD_DIM // 4
Q_BLOCK = 128
RMS_EPS = 1e-6
N_EVEN = (DEPTH + 1) // 2
N_ODD = DEPTH // 2
A_Q_HEADS = 8
A_KV_HEADS = 2
A_GROUP = A_Q_HEADS // A_KV_HEADS
B_HEADS = 4
B_V_DIM = 2 * HEAD_DIM
A_Q_W = A_Q_HEADS * HEAD_DIM
A_KV_W = A_KV_HEADS * HEAD_DIM
B_QK_W = B_HEADS * 2 * HEAD_DIM
B_V_W = B_HEADS * B_V_DIM
AB_IN = A_Q_W + 2 * A_KV_W + 2 * B_QK_W + B_V_W
AB_OUT = A_Q_W + B_V_W
AB_SPLITS = (A_Q_W, A_Q_W + A_KV_W, A_Q_W + 2 * A_KV_W, A_Q_W + 2 * A_KV_W + B_QK_W,
             A_Q_W + 2 * A_KV_W + 2 * B_QK_W)
C_HEADS = 16
C_WIDTH = C_HEADS * HEAD_DIM
NA_ROWS = 8
NA_COLS = 16
N_EXPERTS = 16
N_GROUPS = 4
EXPERTS_PER_GROUP = N_EXPERTS // N_GROUPS
TOP_K = 2
D_EXPERT = 1024

kernel_name = 'hybrid_diffusion_prefix_step'

F32 = jnp.float32


def _rmsnorm(x, g):
    xf = x.astype(F32)
    y = xf * lax.rsqrt(jnp.mean(xf * xf, axis=-1, keepdims=True) + RMS_EPS)
    return (y * g.astype(F32)).astype(x.dtype)


def _modulation(cond, w, b):
    m = jax.nn.silu(cond) @ w + b
    return tuple(t[:, None, :] for t in jnp.split(m, 6, axis=-1))


def _adaln(x, g, shift, scale):
    return _rmsnorm(x, g) * (1.0 + scale) + shift


def _axial_rope(x):
    seq = x.shape[1]
    pos = jnp.arange(seq, dtype=jnp.int32)
    rows = (pos // GRID_W).astype(F32)
    cols = (pos % GRID_W).astype(F32)
    inv = ROPE_THETA ** (-jnp.arange(ROPE_FREQS, dtype=F32) / ROPE_FREQS)

    def rot(xa, p):
        ang = p[:, None] * inv[None, :]
        ang = jnp.concatenate([ang, ang], axis=-1)[None, :, None, :]
        x1, x2 = jnp.split(xa.astype(F32), 2, axis=-1)
        return xa.astype(F32) * jnp.cos(ang) + jnp.concatenate([-x2, x1], axis=-1) * jnp.sin(ang)

    half = HEAD_DIM // 2
    out = jnp.concatenate([rot(x[..., :half], rows), rot(x[..., half:], cols)], axis=-1)
    return out.astype(x.dtype)


def _blocks(x):
    b, s = x.shape[:2]
    return jnp.moveaxis(x.reshape(b, s // Q_BLOCK, Q_BLOCK, *x.shape[2:]), 1, 0)


def _unblocks(y):
    nb, b, qb = y.shape[:3]
    return jnp.moveaxis(y, 0, 1).reshape(b, nb * qb, *y.shape[3:])


def _gqa_attend(q, k, v):
    scale = HEAD_DIM ** -0.5

    def one(qb):
        s = jnp.einsum('bqhgd,bkhd->bhgqk', qb, k).astype(F32) * scale
        p = jax.nn.softmax(s, axis=-1).astype(v.dtype)
        return jnp.einsum('bhgqk,bkhd->bqhgd', p, v)

    return _unblocks(lax.map(one, _blocks(q)))


def _diff_attend(q1, q2, k1, k2, v, lam):
    scale = HEAD_DIM ** -0.5

    def one(qs):
        qb1, qb2 = qs
        p1 = jax.nn.softmax(jnp.einsum('bqhd,bkhd->bhqk', qb1, k1).astype(F32) * scale, axis=-1)
        p2 = jax.nn.softmax(jnp.einsum('bqhd,bkhd->bhqk', qb2, k2).astype(F32) * scale, axis=-1)
        a = (p1 - lam * p2).astype(v.dtype)
        return jnp.einsum('bhqk,bkhe->bqhe', a, v)

    return _unblocks(lax.map(one, (_blocks(q1), _blocks(q2))))


def _diff_lambda(lq1, lk1, lq2, lk2, lam_init):
    return (jnp.exp(jnp.sum(lq1.astype(F32) * lk1.astype(F32)))
            - jnp.exp(jnp.sum(lq2.astype(F32) * lk2.astype(F32))) + lam_init)


def _ab_project(h, w_in, aqn, akn, bqn, bkn, rope):
    b, s, _ = h.shape
    aq, ak, av, bq, bk, bv = jnp.split(h @ w_in, list(AB_SPLITS), axis=-1)
    aq = _rmsnorm(aq.reshape(b, s, A_Q_HEADS, HEAD_DIM), aqn)
    ak = _rmsnorm(ak.reshape(b, s, A_KV_HEADS, HEAD_DIM), akn)
    av = av.reshape(b, s, A_KV_HEADS, HEAD_DIM)
    bq = _rmsnorm(bq.reshape(b, s, 2 * B_HEADS, HEAD_DIM), bqn)
    bk = _rmsnorm(bk.reshape(b, s, 2 * B_HEADS, HEAD_DIM), bkn)
    bv = bv.reshape(b, s, B_HEADS, B_V_DIM)
    if rope:
        aq, ak, bq, bk = _axial_rope(aq), _axial_rope(ak), _axial_rope(bq), _axial_rope(bk)
    bq = bq.reshape(b, s, B_HEADS, 2, HEAD_DIM)
    bk = bk.reshape(b, s, B_HEADS, 2, HEAD_DIM)
    return aq, ak, av, bq, bk, bv


def _ab_mix(aq, ak, av, bq, bk, bv, lam, lam_init, subln, w_out):
    b, s = aq.shape[:2]
    oa = _gqa_attend(aq.reshape(b, s, A_KV_HEADS, A_GROUP, HEAD_DIM), ak, av).reshape(b, s, A_Q_W)
    ob = _diff_attend(bq[..., 0, :], bq[..., 1, :], bk[..., 0, :], bk[..., 1, :], bv, lam)
    ob = (_rmsnorm(ob, subln) * (1.0 - lam_init)).reshape(b, s, B_V_W)
    return jnp.concatenate([oa, ob], axis=-1) @ w_out


def _c_project(h, w_in, qn, kn):
    b, s, _ = h.shape
    q, k, v = jnp.split(h @ w_in, 3, axis=-1)
    q = _rmsnorm(q.reshape(b, s, C_HEADS, HEAD_DIM), qn)
    k = _rmsnorm(k.reshape(b, s, C_HEADS, HEAD_DIM), kn)
    return q, k, v.reshape(b, s, C_HEADS, HEAD_DIM)


def _na_attend(q, k, v, ctx_k, ctx_v, rpb):
    b, seq, h, d = q.shape
    rows = seq // GRID_W
    kh = min(NA_ROWS, rows)
    t = ctx_k.shape[1]
    qg = q.reshape(b, rows, GRID_W, h, d)
    r = jnp.arange(rows)
    row_start = jnp.clip(r - kh // 2, 0, rows - kh)
    row_idx = row_start[:, None] + jnp.arange(kh)[None, :]
    kw = k.reshape(b, rows, GRID_W, h, d)[:, row_idx].reshape(b, rows, kh * GRID_W, h, d)
    vw = v.reshape(b, rows, GRID_W, h, d)[:, row_idx].reshape(b, rows, kh * GRID_W, h, d)
    cols = jnp.arange(GRID_W)
    col_start = jnp.clip(cols - NA_COLS // 2, 0, GRID_W - NA_COLS)
    col_in = (cols[None, :] >= col_start[:, None]) & (cols[None, :] < col_start[:, None] + NA_COLS)
    mask = jnp.broadcast_to(col_in[:, None, :], (GRID_W, kh, GRID_W)).reshape(GRID_W, kh * GRID_W)
    row_off = row_idx - r[:, None]
    col_off = jnp.clip(cols[None, :] - cols[:, None], -(NA_COLS - 1), NA_COLS - 1)
    bias = rpb[:, row_off[:, :, None, None] + (NA_ROWS - 1), col_off[None, None, :, :] + (NA_COLS - 1)]
    bias = bias.transpose(1, 0, 3, 2, 4).reshape(rows, h, GRID_W, kh * GRID_W).astype(F32)
    scale = HEAD_DIM ** -0.5
    s_lat = jnp.einsum('brqhd,brkhd->brhqk', qg, kw).astype(F32) * scale + bias[None]
    s_lat = jnp.where(mask, s_lat, -jnp.inf)
    s_ctx = jnp.einsum('brqhd,bkhd->brhqk', qg, ctx_k).astype(F32) * scale
    p = jax.nn.softmax(jnp.concatenate([s_ctx, s_lat], axis=-1), axis=-1).astype(v.dtype)
    o = (jnp.einsum('brhqk,bkhd->brqhd', p[..., :t], ctx_v)
         + jnp.einsum('brhqk,brkhd->brqhd', p[..., t:], vw))
    return o.reshape(b, seq, h * d)


def _moe(h, w_router, b_router, w_gate, w_up, w_down):
    b, s, d = h.shape
    n = h.reshape(b * s, d)
    scores = jax.nn.softmax(n.astype(F32) @ w_router.astype(F32), axis=-1)
    sel = scores + b_router.astype(F32)
    grp = lax.top_k(sel.reshape(-1, N_GROUPS, EXPERTS_PER_GROUP), TOP_K)[0].sum(-1)
    best = jnp.argmax(grp, axis=-1)
    in_group = (jnp.arange(N_EXPERTS) // EXPERTS_PER_GROUP)[None, :] == best[:, None]
    _, idx = lax.top_k(jnp.where(in_group, sel, -jnp.inf), TOP_K)
    w = jnp.take_along_axis(scores, idx, axis=-1)
    w = w / jnp.sum(w, axis=-1, keepdims=True)
    gates = jnp.sum(jax.nn.one_hot(idx, N_EXPERTS, dtype=F32) * w[..., None], axis=1)
    hid = jax.nn.silu(jnp.einsum('nd,edf->enf', n, w_gate)) * jnp.einsum('nd,edf->enf', n, w_up)
    hid = hid * gates.T[:, :, None].astype(hid.dtype)
    return jnp.einsum('enf,efd->nd', hid, w_down).reshape(b, s, d)


def setup_inputs(seed: int = 0) -> dict:
    key = jax.random.key(seed)
    ks = iter(jax.random.split(key, 40))

    def nrm(shape, scale=1.0):
        return jax.random.normal(next(ks), shape, jnp.float32) * scale

    def gain(shape):
        return 1.0 + nrm(shape, 0.02)

    D = D_MODEL
    return {
        'x_prompt': nrm((BATCH, SEQ, D)),
        'x_sample': nrm((DEC_BATCH, DEC_SEQ, D)),
        'cache_a_k': nrm((DEC_BATCH, N_EVEN, PAST_LEN, A_KV_HEADS, HEAD_DIM)),
        'cache_a_v': nrm((DEC_BATCH, N_EVEN, PAST_LEN, A_KV_HEADS, HEAD_DIM)),
        'cache_b_k': nrm((DEC_BATCH, N_EVEN, PAST_LEN, B_HEADS, 2, HEAD_DIM)),
        'cache_b_v': nrm((DEC_BATCH, N_EVEN, PAST_LEN, B_HEADS, B_V_DIM)),
        'cache_c_k': nrm((DEC_BATCH, N_ODD, PAST_LEN, C_HEADS, HEAD_DIM)),
        'cache_c_v': nrm((DEC_BATCH, N_ODD, PAST_LEN, C_HEADS, HEAD_DIM)),
        'c': nrm((DEC_BATCH, D)),
        'c_ctx': nrm((D,)),
        'w_mod': nrm((DEPTH, D, 6 * D), 0.5 * D ** -0.5),
        'b_mod': nrm((DEPTH, 6 * D), 0.02),
        'norm_mix': gain((DEPTH, D)),
        'norm_ffn': gain((DEPTH, D)),
        'w_in_ab': nrm((N_EVEN, D, AB_IN), D ** -0.5),
        'w_out_ab': nrm((N_EVEN, AB_OUT, D), AB_OUT ** -0.5),
        'a_q_norm': gain((N_EVEN, HEAD_DIM)),
        'a_k_norm': gain((N_EVEN, HEAD_DIM)),
        'b_q_norm': gain((N_EVEN, HEAD_DIM)),
        'b_k_norm': gain((N_EVEN, HEAD_DIM)),
        'lam_q1': nrm((N_EVEN, HEAD_DIM), 0.1),
        'lam_k1': nrm((N_EVEN, HEAD_DIM), 0.1),
        'lam_q2': nrm((N_EVEN, HEAD_DIM), 0.1),
        'lam_k2': nrm((N_EVEN, HEAD_DIM), 0.1),
        'b_subln': gain((N_EVEN, B_V_DIM)),
        'w_in_c': nrm((N_ODD, D, 3 * C_WIDTH), D ** -0.5),
        'w_out_c': nrm((N_ODD, C_WIDTH, D), C_WIDTH ** -0.5),
        'c_q_norm': gain((N_ODD, HEAD_DIM)),
        'c_k_norm': gain((N_ODD, HEAD_DIM)),
        'c_rpb': nrm((N_ODD, C_HEADS, 2 * NA_ROWS - 1, 2 * NA_COLS - 1), 0.1),
        'w_router': nrm((D, N_EXPERTS), D ** -0.5),
        'b_router': nrm((N_EXPERTS,), 0.01),
        'w_gate': nrm((DEPTH, N_EXPERTS, D, D_EXPERT), D ** -0.5),
        'w_up': nrm((DEPTH, N_EXPERTS, D, D_EXPERT), D ** -0.5),
        'w_down': nrm((DEPTH, N_EXPERTS, D_EXPERT, D), D_EXPERT ** -0.5),
    }


def reference(x_prompt, x_sample, cache_a_k, cache_a_v, cache_b_k, cache_b_v, cache_c_k, cache_c_v,
              c, c_ctx, w_mod, b_mod, norm_mix, norm_ffn, w_in_ab, w_out_ab, a_q_norm, a_k_norm,
              b_q_norm, b_k_norm, lam_q1, lam_k1, lam_q2, lam_k2, b_subln, w_in_c, w_out_c,
              c_q_norm, c_k_norm, c_rpb, w_router, b_router, w_gate, w_up, w_down):
    xp, xs = x_prompt, x_sample
    cond_ctx = c_ctx[None, :]
    new_a_k, new_a_v, new_b_k, new_b_v, new_c_k, new_c_v = [], [], [], [], [], []
    for l in range(DEPTH):
        sh_p, sc_p, g_p, shf_p, scf_p, gf_p = _modulation(cond_ctx, w_mod[l], b_mod[l])
        sh_s, sc_s, g_s, shf_s, scf_s, gf_s = _modulation(c, w_mod[l], b_mod[l])
        hp = _adaln(xp, norm_mix[l], sh_p, sc_p)
        hs = _adaln(xs, norm_mix[l], sh_s, sc_s)
        if l % 2 == 0:
            e = l // 2
            lam_init = 0.8 - 0.6 * math.exp(-0.3 * l)
            lam = _diff_lambda(lam_q1[e], lam_k1[e], lam_q2[e], lam_k2[e], lam_init)
            aq, ak, av, bq, bk, bv = _ab_project(hp, w_in_ab[e], a_q_norm[e], a_k_norm[e],
                                                 b_q_norm[e], b_k_norm[e], False)
            new_a_k.append(ak)
            new_a_v.append(av)
            new_b_k.append(bk)
            new_b_v.append(bv)
            op = _ab_mix(aq, ak, av, bq, bk, bv, lam, lam_init, b_subln[e], w_out_ab[e])
            aq, ak, av, bq, bk, bv = _ab_project(hs, w_in_ab[e], a_q_norm[e], a_k_norm[e],
                                                 b_q_norm[e], b_k_norm[e], True)
            os_ = _ab_mix(aq,
                          jnp.concatenate([cache_a_k[:, e], ak], axis=1),
                          jnp.concatenate([cache_a_v[:, e], av], axis=1),
                          bq,
                          jnp.concatenate([cache_b_k[:, e], bk], axis=1),
                          jnp.concatenate([cache_b_v[:, e], bv], axis=1),
                          lam, lam_init, b_subln[e], w_out_ab[e])
        else:
            o = l // 2
            q, k, v = _c_project(hp, w_in_c[o], c_q_norm[o], c_k_norm[o])
            new_c_k.append(k)
            new_c_v.append(v)
            op = _gqa_attend(q[:, :, :, None, :], k, v).reshape(xp.shape[0], xp.shape[1], C_WIDTH) @ w_out_c[o]
            q, k, v = _c_project(hs, w_in_c[o], c_q_norm[o], c_k_norm[o])
            os_ = _na_attend(q, k, v, cache_c_k[:, o], cache_c_v[:, o], c_rpb[o]) @ w_out_c[o]
        xp = xp + g_p * op
        xs = xs + g_s * os_
        xp = xp + gf_p * _moe(_adaln(xp, norm_ffn[l], shf_p, scf_p), w_router, b_router,
                              w_gate[l], w_up[l], w_down[l])
        xs = xs + gf_s * _moe(_adaln(xs, norm_ffn[l], shf_s, scf_s), w_router, b_router,
                              w_gate[l], w_up[l], w_down[l])
    return (xp, xs,
            jnp.stack(new_a_k, axis=1), jnp.stack(new_a_v, axis=1),
            jnp.stack(new_b_k, axis=1), jnp.stack(new_b_v, axis=1),
            jnp.stack(new_c_k, axis=1), jnp.stack(new_c_v, axis=1))
```

```python
import functools
import math

import numpy as np
import jax
import jax.numpy as jnp
from jax import lax
from jax.experimental import pallas as pl
from jax.experimental.pallas import tpu as pltpu

F32 = jnp.float32
BF16 = jnp.bfloat16
HIGHEST = lax.Precision.HIGHEST

D_MODEL = 1024
BATCH = 16
SEQ = 256
DEPTH = 2
DEC_BATCH = 8
DEC_SEQ = 1024
PAST_LEN = 256
GRID_W = 64
HEAD_DIM = 64
ROPE_THETA = 10000.0
RMS_EPS = 1e-6
A_Q_HEADS = 8
A_KV_HEADS = 2
A_GROUP = A_Q_HEADS // A_KV_HEADS
B_HEADS = 4
B_V_DIM = 2 * HEAD_DIM
A_Q_W = A_Q_HEADS * HEAD_DIM
A_KV_W = A_KV_HEADS * HEAD_DIM
B_QK_W = B_HEADS * 2 * HEAD_DIM
B_V_W = B_HEADS * B_V_DIM
AB_IN = A_Q_W + 2 * A_KV_W + 2 * B_QK_W + B_V_W
C_HEADS = 16
C_WIDTH = C_HEADS * HEAD_DIM
NA_ROWS = 8
NA_COLS = 16
N_EXPERTS = 16
N_GROUPS = 4
EXPERTS_PER_GROUP = N_EXPERTS // N_GROUPS
D_EXPERT = 1024

LANES = 128
N_PROMPT = BATCH * SEQ
N_SAMPLE = DEC_BATCH * DEC_SEQ
N_TOK = N_PROMPT + N_SAMPLE
TM = 256
N_TILES = N_TOK // TM
N_PROMPT_TILES = N_PROMPT // TM
COND_ROWS = 16
CTX_COND_ROW = DEC_BATCH
MOE_TM = 256
MOE_ROWS = 2 * N_TOK + N_EXPERTS * MOE_TM
MOE_TILES = MOE_ROWS // MOE_TM
GRID_ROWS = DEC_SEQ // GRID_W
NEG_BIG = -1e30
VMEM_LIMIT = 56 * 1024 * 1024


def _silu(x):
    return x * (1.0 / (1.0 + jnp.exp(-x)))


def _dot(a, b):
    return jnp.dot(a, b, preferred_element_type=F32)


def _dot_nt(a, b):
    return lax.dot_general(a, b, (((1,), (1,)), ((), ())), preferred_element_type=F32)


def _mod_kernel(c_ref, w_ref, b_ref, o_ref):
    s = _silu(c_ref[...])
    o_ref[...] = jnp.dot(s, w_ref[...], precision=HIGHEST, preferred_element_type=F32) + b_ref[...]


def _modulation(cond, w_mod, b_mod):
    tn = 1536
    return pl.pallas_call(
        _mod_kernel,
        out_shape=jax.ShapeDtypeStruct((DEPTH, COND_ROWS, 6 * D_MODEL), F32),
        grid=(DEPTH, 6 * D_MODEL // tn),
        in_specs=[
            pl.BlockSpec((COND_ROWS, D_MODEL), lambda l, j: (0, 0)),
            pl.BlockSpec((None, D_MODEL, tn), lambda l, j: (l, 0, j)),
            pl.BlockSpec((None, 1, tn), lambda l, j: (l, 0, j)),
        ],
        out_specs=pl.BlockSpec((None, COND_ROWS, tn), lambda l, j: (l, 0, j)),
        compiler_params=pltpu.CompilerParams(
            dimension_semantics=("arbitrary", "arbitrary"), vmem_limit_bytes=VMEM_LIMIT),
        name="modulation",
    )(cond, w_mod, b_mod.reshape(DEPTH, 1, 6 * D_MODEL))


def _rope_tables():
    pos = np.arange(DEC_SEQ)
    rows = (pos // GRID_W).astype(np.float64)
    cols = (pos % GRID_W).astype(np.float64)
    nfreq = HEAD_DIM // 4
    inv = ROPE_THETA ** (-np.arange(nfreq, dtype=np.float64) / nfreq)
    d = np.arange(HEAD_DIM)
    dd = d % (HEAD_DIM // 2)
    p = np.where((d >= HEAD_DIM // 2)[None, :], cols[:, None], rows[:, None])
    ang = p * inv[dd % nfreq][None, :]
    cos, sin = np.cos(ang), np.sin(ang)
    second = (dd >= nfreq)[None, :]
    sa = np.where(second, sin, 0.0)
    sb = np.where(second, 0.0, -sin)

    def full(t, ident):
        t = np.concatenate([t, np.full((TM, HEAD_DIM), ident)], axis=0)
        return jnp.asarray(np.tile(t, (1, LANES // HEAD_DIM)), dtype=F32)

    return full(cos, 1.0), full(sa, 0.0), full(sb, 0.0)


def _lnproj_kernel(tile_cond_ref, tile_rope_ref, x_ref, mod_ref, g_ref, w_ref, hg_ref, gmat_ref,
                   cos_ref, sa_ref, sb_ref, *out_refs, chunks, use_rope):
    i = pl.program_id(0)
    row = tile_cond_ref[i]
    x = x_ref[...]
    ms = jnp.mean(x * x, axis=-1, keepdims=True)
    xn = x * lax.rsqrt(ms + RMS_EPS) * g_ref[...]
    shift = mod_ref[pl.ds(row, 1), 0:D_MODEL]
    scale = mod_ref[pl.ds(row, 1), D_MODEL:2 * D_MODEL]
    h = (xn * (1.0 + scale) + shift).astype(BF16)
    gmat = gmat_ref[...]
    for piece in range(len(chunks) // 2):
        col0 = piece * 2 * LANES
        y2 = _dot(h, w_ref[:, col0:col0 + 2 * LANES])
        for sub in range(2):
            c = piece * 2 + sub
            normed, dst, dst_col, f32_dst, f32_col = chunks[c]
            y = y2[:, sub * LANES:(sub + 1) * LANES]
            if normed:
                sq = y * y
                sq_hi = sq.astype(BF16)
                sq_lo = (sq - sq_hi.astype(F32)).astype(BF16)
                gs = _dot(sq_hi, gmat) + _dot(sq_lo, gmat)
                y = y * lax.rsqrt(gs * (1.0 / HEAD_DIM) + RMS_EPS)
                if f32_dst is not None:
                    kf = y * hg_ref[:, c * LANES:(c + 1) * LANES]

                    @pl.when(i < N_PROMPT_TILES)
                    def _():
                        out_refs[f32_dst][:, f32_col:f32_col + LANES] = kf
                y = y * hg_ref[:, c * LANES:(c + 1) * LANES]
                if use_rope:
                    y = (y * cos_ref[...] + pltpu.roll(y, HEAD_DIM // 4, 1) * sa_ref[...]
                         + pltpu.roll(y, LANES - HEAD_DIM // 4, 1) * sb_ref[...])
            elif f32_dst is not None:
                yv = y

                @pl.when(i < N_PROMPT_TILES)
                def _():
                    out_refs[f32_dst][:, f32_col:f32_col + LANES] = yv
            out_refs[dst][:, dst_col:dst_col + LANES] = y.astype(BF16)


def _lnproj(x, mod_l, gain, w_bf16, head_gain, chunks, q_w, kv_w, f32_widths, use_rope, tile_cond,
            tile_rope, rope):
    dout = w_bf16.shape[1]
    cos, sa, sb = rope
    gmat = jnp.asarray(np.kron(np.eye(LANES // HEAD_DIM), np.ones((HEAD_DIM, HEAD_DIM))), dtype=BF16)
    const = lambda i, *_: (0, 0)
    tok = lambda i, *_: (i, 0)
    rope_map = lambda i, tc, tr: (tr[i], 0)
    prm = lambda i, *_: (jnp.minimum(i, N_PROMPT_TILES - 1), 0)
    out_shape = [jax.ShapeDtypeStruct((N_TOK, q_w), BF16), jax.ShapeDtypeStruct((N_TOK, kv_w), BF16)]
    out_specs = [pl.BlockSpec((TM, q_w), tok), pl.BlockSpec((TM, kv_w), tok)]
    for wd in f32_widths:
        out_shape.append(jax.ShapeDtypeStruct((N_PROMPT, wd), F32))
        out_specs.append(pl.BlockSpec((TM, wd), prm))
    return pl.pallas_call(
        functools.partial(_lnproj_kernel, chunks=chunks, use_rope=use_rope),
        out_shape=out_shape,
        grid_spec=pltpu.PrefetchScalarGridSpec(
            num_scalar_prefetch=2,
            grid=(N_TILES,),
            in_specs=[
                pl.BlockSpec((TM, D_MODEL), tok),
                pl.BlockSpec((COND_ROWS, 6 * D_MODEL), const),
                pl.BlockSpec((1, D_MODEL), const),
                pl.BlockSpec((D_MODEL, dout), const),
                pl.BlockSpec((1, dout), const),
                pl.BlockSpec((LANES, LANES), const),
                pl.BlockSpec((TM, LANES), rope_map),
                pl.BlockSpec((TM, LANES), rope_map),
                pl.BlockSpec((TM, LANES), rope_map),
            ],
            out_specs=out_specs,
        ),
        compiler_params=pltpu.CompilerParams(
            dimension_semantics=("arbitrary",), vmem_limit_bytes=VMEM_LIMIT),
        name="lnproj",
    )(tile_cond, tile_rope, x, mod_l, gain.reshape(1, D_MODEL), w_bf16, head_gain, gmat, cos, sa, sb)


def _softmax_parts(q, k_new, k_ctx):
    s_n = _dot_nt(q, k_new)
    m = jnp.max(s_n, axis=-1, keepdims=True)
    if k_ctx is not None:
        s_c = _dot_nt(q, k_ctx)
        m = jnp.maximum(m, jnp.max(s_c, axis=-1, keepdims=True))
    p_n = jnp.exp(s_n - m)
    l = jnp.sum(p_n, axis=-1, keepdims=True)
    p_c = None
    if k_ctx is not None:
        p_c = jnp.exp(s_c - m)
        l = l + jnp.sum(p_c, axis=-1, keepdims=True)
    return p_n, p_c, l


def _gqa_heads(q_ref, q_col0, kv_ref, k_col0, v_col0, ck_ref, cv_ref, n_kv, group, tq):
    outs = []
    for h in range(n_kv):
        ks = slice(k_col0 + h * HEAD_DIM, k_col0 + (h + 1) * HEAD_DIM)
        vs = slice(v_col0 + h * HEAD_DIM, v_col0 + (h + 1) * HEAD_DIM)
        cs = slice(h * HEAD_DIM, (h + 1) * HEAD_DIM)
        k_n, v_n = kv_ref[:, ks], kv_ref[:, vs]
        k_c = v_c = None
        if ck_ref is not None:
            k_c = ck_ref[:, cs].astype(BF16)
            v_c = cv_ref[:, cs].astype(BF16)
        qs = [q_ref[:, q_col0 + (h * group + g) * HEAD_DIM:q_col0 + (h * group + g + 1) * HEAD_DIM]
              for g in range(group)]
        q = qs[0] if group == 1 else jnp.concatenate(qs, axis=0)
        p_n, p_c, l = _softmax_parts(q, k_n, k_c)
        o = _dot(p_n.astype(BF16), v_n)
        if p_c is not None:
            o = o + _dot(p_c.astype(BF16), v_c)
        o = o * (1.0 / l)
        for g in range(group):
            outs.append(o[g * tq:(g + 1) * tq])
    return outs


def _attn_ab_kernel(*refs, has_cache, tq, lam_init):
    if has_cache:
        q_ref, kv_ref, cak_ref, cav_ref, cbk_ref, cbv_ref, lamv_ref, subln_ref, o_ref = refs
    else:
        q_ref, kv_ref, lamv_ref, subln_ref, o_ref = refs
        cak_ref = cav_ref = cbk_ref = cbv_ref = None
    outs = _gqa_heads(q_ref, 0, kv_ref, 0, A_KV_W, cak_ref, cav_ref, A_KV_HEADS, A_GROUP, tq)

    lv = lamv_ref[...]
    l1 = jnp.sum(lv[0:1] * lv[1:2], axis=-1, keepdims=True)
    l2 = jnp.sum(lv[2:3] * lv[3:4], axis=-1, keepdims=True)
    lam = jnp.exp(l1) - jnp.exp(l2) + lam_init
    bk0 = 2 * A_KV_W
    bv0 = bk0 + B_QK_W
    for h in range(B_HEADS):
        c1 = slice(h * 2 * HEAD_DIM, h * 2 * HEAD_DIM + HEAD_DIM)
        c2 = slice(h * 2 * HEAD_DIM + HEAD_DIM, (h + 1) * 2 * HEAD_DIM)
        cv = slice(h * B_V_DIM, (h + 1) * B_V_DIM)
        q1 = q_ref[:, A_Q_W + c1.start:A_Q_W + c1.stop]
        q2 = q_ref[:, A_Q_W + c2.start:A_Q_W + c2.stop]
        k1 = kv_ref[:, bk0 + c1.start:bk0 + c1.stop]
        k2 = kv_ref[:, bk0 + c2.start:bk0 + c2.stop]
        v_n = kv_ref[:, bv0 + cv.start:bv0 + cv.stop]
        k1c = k2c = v_c = None
        if has_cache:
            k1c = cbk_ref[:, c1].astype(BF16)
            k2c = cbk_ref[:, c2].astype(BF16)
            v_c = cbv_ref[:, cv].astype(BF16)
        p1n, p1c, l1s = _softmax_parts(q1, k1, k1c)
        p2n, p2c, l2s = _softmax_parts(q2, k2, k2c)
        r1 = 1.0 / l1s
        r2 = lam / l2s
        o = _dot((p1n * r1 - p2n * r2).astype(BF16), v_n)
        if has_cache:
            o = o + _dot((p1c * r1 - p2c * r2).astype(BF16), v_c)
        ms = jnp.mean(o * o, axis=-1, keepdims=True)
        o = o * lax.rsqrt(ms + RMS_EPS) * subln_ref[...] * (1.0 - lam_init)
        outs.append(o)
    o_ref[...] = jnp.concatenate(outs, axis=1).astype(BF16)


def _attn_ab(q, kv, lamv, subln, lam_init, caches):
    kv_w = kv.shape[1]
    kern = functools.partial(_attn_ab_kernel, lam_init=lam_init)
    const = lambda b, j: (0, 0)
    cp = pltpu.CompilerParams(dimension_semantics=("arbitrary", "arbitrary"), vmem_limit_bytes=VMEM_LIMIT)
    if caches is None:
        tq = SEQ
        return pl.pallas_call(
            functools.partial(kern, has_cache=False, tq=tq),
            out_shape=jax.ShapeDtypeStruct((N_PROMPT, D_MODEL), BF16),
            grid=(BATCH, 1),
            in_specs=[
                pl.BlockSpec((tq, D_MODEL), lambda b, j: (b, 0)),
                pl.BlockSpec((SEQ, kv_w), lambda b, j: (b, 0)),
                pl.BlockSpec((4, HEAD_DIM), const),
                pl.BlockSpec((1, B_V_DIM), const),
            ],
            out_specs=pl.BlockSpec((tq, D_MODEL), lambda b, j: (b, 0)),
            compiler_params=cp,
            name="attn_ab_prompt",
        )(q, kv, lamv, subln)
    tq = 128
    nq = DEC_SEQ // tq
    q0 = N_PROMPT // tq
    kv0 = N_PROMPT // DEC_SEQ
    cak, cav, cbk, cbv = caches
    cspec = lambda w: pl.BlockSpec((None, PAST_LEN, w), lambda b, j: (b, 0, 0))
    return pl.pallas_call(
        functools.partial(kern, has_cache=True, tq=tq),
        out_shape=jax.ShapeDtypeStruct((N_SAMPLE, D_MODEL), BF16),
        grid=(DEC_BATCH, nq),
        in_specs=[
            pl.BlockSpec((tq, D_MODEL), lambda b, j: (q0 + b * nq + j, 0)),
            pl.BlockSpec((DEC_SEQ, kv_w), lambda b, j: (kv0 + b, 0)),
            cspec(A_KV_W), cspec(A_KV_W), cspec(B_QK_W), cspec(B_V_W),
            pl.BlockSpec((4, HEAD_DIM), const),
            pl.BlockSpec((1, B_V_DIM), const),
        ],
        out_specs=pl.BlockSpec((tq, D_MODEL), lambda b, j: (b * nq + j, 0)),
        compiler_params=cp,
        name="attn_ab_sample",
    )(q, kv, cak, cav, cbk, cbv, lamv, subln)


def _attn_c_prompt_kernel(q_ref, kv_ref, o_ref):
    outs = _gqa_heads(q_ref, 0, kv_ref, 0, C_WIDTH, None, None, C_HEADS, 1, SEQ)
    o_ref[...] = jnp.concatenate(outs, axis=1).astype(BF16)


def _attn_c_prompt(q, kv):
    return pl.pallas_call(
        _attn_c_prompt_kernel,
        out_shape=jax.ShapeDtypeStruct((N_PROMPT, D_MODEL), BF16),
        grid=(BATCH,),
        in_specs=[
            pl.BlockSpec((SEQ, C_WIDTH), lambda b: (b, 0)),
            pl.BlockSpec((SEQ, 2 * C_WIDTH), lambda b: (b, 0)),
        ],
        out_specs=pl.BlockSpec((SEQ, C_WIDTH), lambda b: (b, 0)),
        compiler_params=pltpu.CompilerParams(dimension_semantics=("arbitrary",), vmem_limit_bytes=VMEM_LIMIT),
        name="attn_c_prompt",
    )(q, kv)


def _na_kernel(q_ref, kv_ref, ck_ref, cv_ref, tp_ref, o_ref):
    r = pl.program_id(1)
    kh = min(NA_ROWS, GRID_ROWS)
    rs = jnp.clip(r - kh // 2, 0, GRID_ROWS - kh)
    ro0 = rs - r + (NA_ROWS - 1)
    start = pl.multiple_of(rs * GRID_W, GRID_W)
    win = kh * GRID_W
    outs = []
    for h in range(C_HEADS):
        cs = slice(h * HEAD_DIM, (h + 1) * HEAD_DIM)
        q = q_ref[:, cs]
        kw = kv_ref[pl.ds(start, win), h * HEAD_DIM:(h + 1) * HEAD_DIM]
        vw = kv_ref[pl.ds(start, win), C_WIDTH + h * HEAD_DIM:C_WIDTH + (h + 1) * HEAD_DIM]
        kc = ck_ref[:, cs].astype(BF16)
        vc = cv_ref[:, cs].astype(BF16)
        bias = jnp.concatenate([tp_ref[h, ro0 + 2 * j] for j in range(kh // 2)], axis=1)
        s_l = _dot_nt(q, kw) + bias
        s_c = _dot_nt(q, kc)
        m = jnp.maximum(jnp.max(s_l, axis=-1, keepdims=True), jnp.max(s_c, axis=-1, keepdims=True))
        p_l = jnp.exp(s_l - m)
        p_c = jnp.exp(s_c - m)
        l = jnp.sum(p_l, axis=-1, keepdims=True) + jnp.sum(p_c, axis=-1, keepdims=True)
        o = _dot(p_c.astype(BF16), vc) + _dot(p_l.astype(BF16), vw)
        outs.append(o * (1.0 / l))
    o_ref[...] = jnp.concatenate(outs, axis=1).astype(BF16)


def _na_bias_table(rpb):
    cols = np.arange(GRID_W)
    col_start = np.clip(cols - NA_COLS // 2, 0, GRID_W - NA_COLS)
    col_in = (cols[None, :] >= col_start[:, None]) & (cols[None, :] < col_start[:, None] + NA_COLS)
    col_off = np.clip(cols[None, :] - cols[:, None], -(NA_COLS - 1), NA_COLS - 1) + (NA_COLS - 1)
    t = rpb.astype(F32)[:, :, col_off]
    t = jnp.where(jnp.asarray(col_in)[None, None], t, NEG_BIG)
    return jnp.concatenate([t[:, :-1], t[:, 1:]], axis=-1)


def _attn_na(q, kv, ck, cv, tp):
    q0 = N_PROMPT // GRID_W
    kv0 = N_PROMPT // DEC_SEQ
    return pl.pallas_call(
        _na_kernel,
        out_shape=jax.ShapeDtypeStruct((N_SAMPLE, D_MODEL), BF16),
        grid=(DEC_BATCH, GRID_ROWS),
        in_specs=[
            pl.BlockSpec((GRID_W, C_WIDTH), lambda b, r: (q0 + b * GRID_ROWS + r, 0)),
            pl.BlockSpec((DEC_SEQ, 2 * C_WIDTH), lambda b, r: (kv0 + b, 0)),
            pl.BlockSpec((None, PAST_LEN, C_WIDTH), lambda b, r: (b, 0, 0)),
            pl.BlockSpec((None, PAST_LEN, C_WIDTH), lambda b, r: (b, 0, 0)),
            pl.BlockSpec(tp.shape, lambda b, r: (0, 0, 0, 0)),
        ],
        out_specs=pl.BlockSpec((GRID_W, C_WIDTH), lambda b, r: (b * GRID_ROWS + r, 0)),
        compiler_params=pltpu.CompilerParams(
            dimension_semantics=("arbitrary", "arbitrary"), vmem_limit_bytes=VMEM_LIMIT),
        name="attn_na_sample",
    )(q, kv, ck, cv, tp)


def _first_wins_ranks(vals):
    ranks = []
    for i in range(len(vals)):
        r = jnp.zeros_like(vals[i])
        for j in range(len(vals)):
            if j == i:
                continue
            beats = (vals[j] >= vals[i]) if j < i else (vals[j] > vals[i])
            r = r + jnp.where(beats, 1.0, 0.0)
        ranks.append(r)
    return ranks


def _outproj_kernel(tile_cond_ref, op_ref, os_ref, x_ref, w_ref, mod_ref, g2_ref, wrt_ref, br_ref, tri_ref,
                    x1_ref, h2_ref, gates_ref, rank_ref, carry_ref):
    i = pl.program_id(0)
    row = tile_cond_ref[i]

    @pl.when(i == 0)
    def _():
        carry_ref[...] = jnp.zeros_like(carry_ref)

    o = jnp.where(i < N_PROMPT_TILES, op_ref[...], os_ref[...])
    acc = _dot(o, w_ref[...])
    gate = mod_ref[pl.ds(row, 1), 2 * D_MODEL:3 * D_MODEL]
    x1 = x_ref[...] + gate * acc
    x1_ref[...] = x1
    ms = jnp.mean(x1 * x1, axis=-1, keepdims=True)
    xn = x1 * lax.rsqrt(ms + RMS_EPS) * g2_ref[...]
    shift = mod_ref[pl.ds(row, 1), 3 * D_MODEL:4 * D_MODEL]
    scale = mod_ref[pl.ds(row, 1), 4 * D_MODEL:5 * D_MODEL]
    h2 = xn * (1.0 + scale) + shift
    h2_ref[...] = h2

    logits = lax.dot_general(wrt_ref[...], h2, (((1,), (1,)), ((), ())), precision=HIGHEST,
                             preferred_element_type=F32)
    e = jnp.exp(logits - jnp.max(logits, axis=0, keepdims=True))
    scores = e * (1.0 / jnp.sum(e, axis=0, keepdims=True))
    sel = scores + br_ref[...]
    sel_rows = [sel[k:k + 1, :] for k in range(N_EXPERTS)]
    in_top2 = []
    group_sum = []
    for g in range(N_GROUPS):
        vals = sel_rows[g * EXPERTS_PER_GROUP:(g + 1) * EXPERTS_PER_GROUP]
        ranks = _first_wins_ranks(vals)
        top = [rk < 2.0 for rk in ranks]
        in_top2.extend(top)
        s = jnp.zeros_like(vals[0])
        for v, t in zip(vals, top):
            s = s + jnp.where(t, v, 0.0)
        group_sum.append(s)
    group_rank = _first_wins_ranks(group_sum)
    mask_rows = []
    for k in range(N_EXPERTS):
        chosen = jnp.where(in_top2[k], 1.0, 0.0) * jnp.where(group_rank[k // EXPERTS_PER_GROUP] < 1.0, 1.0, 0.0)
        mask_rows.append(chosen)
    mask = jnp.concatenate(mask_rows, axis=0)
    picked = scores * mask
    gates_ref[...] = picked * (1.0 / jnp.sum(picked, axis=0, keepdims=True))
    prefix = _dot(mask.astype(BF16), tri_ref[...])
    rank_ref[...] = jnp.where(mask > 0.0, prefix + carry_ref[...], -1.0)
    carry_ref[...] = carry_ref[...] + jnp.sum(mask, axis=1, keepdims=True)


def _outproj(o_prompt, o_sample, x, w_bf16, mod_l, gain2, w_router, b_router, tile_cond):
    const = lambda i, *_: (0, 0)
    tok = lambda i, *_: (i, 0)
    tokT = lambda i, *_: (0, i)
    tok_p = lambda i, *_: (jnp.minimum(i, N_PROMPT_TILES - 1), 0)
    tok_s = lambda i, *_: (jnp.maximum(i - N_PROMPT_TILES, 0), 0)
    tri = jnp.asarray(np.triu(np.ones((TM, TM)), k=1), dtype=BF16)
    return pl.pallas_call(
        _outproj_kernel,
        out_shape=[
            jax.ShapeDtypeStruct((N_TOK, D_MODEL), F32),
            jax.ShapeDtypeStruct((N_TOK, D_MODEL), F32),
            jax.ShapeDtypeStruct((N_EXPERTS, N_TOK), F32),
            jax.ShapeDtypeStruct((N_EXPERTS, N_TOK), F32),
        ],
        grid_spec=pltpu.PrefetchScalarGridSpec(
            num_scalar_prefetch=1,
            grid=(N_TILES,),
            in_specs=[
                pl.BlockSpec((TM, D_MODEL), tok_p),
                pl.BlockSpec((TM, D_MODEL), tok_s),
                pl.BlockSpec((TM, D_MODEL), tok),
                pl.BlockSpec((D_MODEL, D_MODEL), const),
                pl.BlockSpec((COND_ROWS, 6 * D_MODEL), const),
                pl.BlockSpec((1, D_MODEL), const),
                pl.BlockSpec((N_EXPERTS, D_MODEL), const),
                pl.BlockSpec((N_EXPERTS, 1), const),
                pl.BlockSpec((TM, TM), const),
            ],
            out_specs=[
                pl.BlockSpec((TM, D_MODEL), tok),
                pl.BlockSpec((TM, D_MODEL), tok),
                pl.BlockSpec((N_EXPERTS, TM), tokT),
                pl.BlockSpec((N_EXPERTS, TM), tokT),
            ],
            scratch_shapes=[pltpu.VMEM((N_EXPERTS, 1), F32)],
        ),
        compiler_params=pltpu.CompilerParams(
            dimension_semantics=("arbitrary",), vmem_limit_bytes=VMEM_LIMIT),
        name="outproj_router",
    )(tile_cond, o_prompt, o_sample, x, w_bf16, mod_l, gain2.reshape(1, D_MODEL), w_router.T,
      b_router.reshape(N_EXPERTS, 1), tri)


def _route_plan(gates, rank):
    sel = rank >= 0.0
    counts = jnp.sum(sel, axis=1).astype(jnp.int32)
    padded = ((counts + MOE_TM - 1) // MOE_TM) * MOE_TM
    ends = jnp.cumsum(padded)
    offs = ends - padded
    dest = jnp.where(sel, offs[:, None] + rank.astype(jnp.int32), -1)
    d1 = jnp.max(dest, axis=0)
    d0 = jnp.min(jnp.where(sel, dest, MOE_ROWS), axis=0)
    w0 = jnp.sum(jnp.where(dest == d0[None, :], gates, 0.0), axis=0)
    w1 = jnp.sum(jnp.where(dest == d1[None, :], gates, 0.0), axis=0)
    n_tiles = (ends[-1] // MOE_TM).astype(jnp.int32).reshape(1)
    tile_start = jnp.arange(MOE_TILES, dtype=jnp.int32) * MOE_TM
    tile_expert = jnp.sum(tile_start[:, None] >= ends[None, :], axis=1).astype(jnp.int32)
    tile_expert = jnp.minimum(tile_expert, N_EXPERTS - 1)
    dests = jnp.concatenate([d0, d1]).astype(jnp.int32)
    weights = jnp.stack([w0, w1], axis=1)
    return dests, weights, tile_expert, n_tiles


def _scatter_kernel(d_ref, h_ref, _, xs_ref, sem):
    base = pl.program_id(0) * TM

    def copy(r, t):
        return pltpu.make_async_copy(h_ref.at[pl.ds(r, 1)], xs_ref.at[pl.ds(t, 1)], sem)

    def issue(r, c):
        copy(r, d_ref[base + r]).start()
        copy(r, d_ref[N_TOK + base + r]).start()
        return c

    lax.fori_loop(0, TM, issue, 0)

    def drain(r, c):
        copy(r, 0).wait()
        copy(r, 0).wait()
        return c

    lax.fori_loop(0, TM, drain, 0)


def _moe_scatter(dests, h2):
    xs0 = jnp.zeros((MOE_ROWS, D_MODEL), F32)
    return pl.pallas_call(
        _scatter_kernel,
        out_shape=jax.ShapeDtypeStruct((MOE_ROWS, D_MODEL), F32),
        grid_spec=pltpu.PrefetchScalarGridSpec(
            num_scalar_prefetch=1,
            grid=(N_TILES,),
            in_specs=[
                pl.BlockSpec((TM, D_MODEL), lambda i, d: (i, 0)),
                pl.BlockSpec(memory_space=pl.ANY),
            ],
            out_specs=pl.BlockSpec(memory_space=pl.ANY),
            scratch_shapes=[pltpu.SemaphoreType.DMA],
        ),
        input_output_aliases={2: 0},
        compiler_params=pltpu.CompilerParams(
            dimension_semantics=("arbitrary",), vmem_limit_bytes=VMEM_LIMIT, has_side_effects=True),
        name="moe_scatter",
    )(dests, h2, xs0)


def _mlp_kernel(te_ref, nv_ref, xs_ref, wg_ref, wu_ref, wd_ref, y_ref, wgb, wub, wdb, hb):
    t = pl.program_id(0)

    @pl.when(t < nv_ref[0])
    def _():
        e = te_ref[t]
        prev = te_ref[jnp.maximum(t - 1, 0)]

        @pl.when((t == 0) | (e != prev))
        def _():
            wgb[...] = wg_ref[...].astype(BF16)
            wub[...] = wu_ref[...].astype(BF16)
            wdb[...] = wd_ref[...].astype(BF16)

        x = xs_ref[...].astype(BF16)
        step = 512
        for c in range(0, D_EXPERT, step):
            g = _dot(x, wgb[:, c:c + step])
            u = _dot(x, wub[:, c:c + step])
            hb[:, c:c + step] = (_silu(g) * u).astype(BF16)
        y_ref[...] = _dot(hb[...], wdb[...])

    @pl.when(t >= nv_ref[0])
    def _():
        y_ref[...] = jnp.zeros_like(y_ref)


def _moe_mlp(layer, tile_expert, n_tiles, xs, w_gate, w_up, w_down):
    def tile_map(t, te, nv):
        return (jnp.minimum(t, nv[0] - 1), 0)

    def out_map(t, te, nv):
        return (t, 0)

    def w_map(t, te, nv):
        return (layer, te[jnp.minimum(t, nv[0] - 1)], 0, 0)

    wspec = lambda a, b: pl.BlockSpec((None, None, a, b), w_map)
    return pl.pallas_call(
        _mlp_kernel,
        out_shape=jax.ShapeDtypeStruct((MOE_ROWS, D_MODEL), F32),
        grid_spec=pltpu.PrefetchScalarGridSpec(
            num_scalar_prefetch=2,
            grid=(MOE_TILES,),
            in_specs=[
                pl.BlockSpec((MOE_TM, D_MODEL), tile_map),
                wspec(D_MODEL, D_EXPERT), wspec(D_MODEL, D_EXPERT), wspec(D_EXPERT, D_MODEL),
            ],
            out_specs=pl.BlockSpec((MOE_TM, D_MODEL), out_map),
            scratch_shapes=[
                pltpu.VMEM((D_MODEL, D_EXPERT), BF16),
                pltpu.VMEM((D_MODEL, D_EXPERT), BF16),
                pltpu.VMEM((D_EXPERT, D_MODEL), BF16),
                pltpu.VMEM((MOE_TM, D_EXPERT), BF16),
            ],
        ),
        compiler_params=pltpu.CompilerParams(
            dimension_semantics=("arbitrary",), vmem_limit_bytes=VMEM_LIMIT),
        name="moe_mlp",
    )(tile_expert, n_tiles, xs, w_gate, w_up, w_down)


def _combine_kernel(d_ref, tile_cond_ref, x1_ref, w_ref, mod_ref, y_ref, out_ref, buf, sem):
    i = pl.program_id(0)
    base = i * TM
    row = tile_cond_ref[i]

    def copy(k, r, t):
        return pltpu.make_async_copy(y_ref.at[pl.ds(t, 1)], buf.at[k, pl.ds(r, 1)], sem)

    def issue(r, c):
        copy(0, r, d_ref[base + r]).start()
        copy(1, r, d_ref[N_TOK + base + r]).start()
        return c

    lax.fori_loop(0, TM, issue, 0)

    def drain(r, c):
        copy(0, r, 0).wait()
        copy(1, r, 0).wait()
        return c

    lax.fori_loop(0, TM, drain, 0)
    w = w_ref[...]
    gate = mod_ref[pl.ds(row, 1), 5 * D_MODEL:6 * D_MODEL]
    out_ref[...] = x1_ref[...] + gate * (w[:, 0:1] * buf[0] + w[:, 1:2] * buf[1])


def _moe_combine(dests, tile_cond, x1, weights, mod_l, y):
    return pl.pallas_call(
        _combine_kernel,
        out_shape=jax.ShapeDtypeStruct((N_TOK, D_MODEL), F32),
        grid_spec=pltpu.PrefetchScalarGridSpec(
            num_scalar_prefetch=2,
            grid=(N_TILES,),
            in_specs=[
                pl.BlockSpec((TM, D_MODEL), lambda i, *_: (i, 0)),
                pl.BlockSpec((TM, 2), lambda i, *_: (i, 0)),
                pl.BlockSpec((COND_ROWS, 6 * D_MODEL), lambda i, *_: (0, 0)),
                pl.BlockSpec(memory_space=pl.ANY),
            ],
            out_specs=pl.BlockSpec((TM, D_MODEL), lambda i, *_: (i, 0)),
            scratch_shapes=[pltpu.VMEM((2, TM, D_MODEL), F32), pltpu.SemaphoreType.DMA],
        ),
        compiler_params=pltpu.CompilerParams(
            dimension_semantics=("arbitrary",), vmem_limit_bytes=VMEM_LIMIT),
        name="moe_combine",
    )(dests, tile_cond, x1, weights, mod_l, y)


def _chunk_plan(segments):
    chunks = []
    for width, normed, dst, dst_col0, f32_dst in segments:
        for k in range(width // LANES):
            chunks.append((normed, dst, dst_col0 + k * LANES, f32_dst, k * LANES))
    return tuple(chunks)


def _head_gain(parts):
    cols = []
    for width, g, mult in parts:
        if g is None:
            cols.append(jnp.ones((width,), F32))
        else:
            cols.append(jnp.tile(g.astype(F32) * mult, width // HEAD_DIM))
    return jnp.concatenate(cols).reshape(1, -1)


def kernel(x_prompt, x_sample, cache_a_k, cache_a_v, cache_b_k, cache_b_v, cache_c_k, cache_c_v, c, c_ctx, w_mod, b_mod, norm_mix, norm_ffn, w_in_ab, w_out_ab, a_q_norm, a_k_norm, b_q_norm, b_k_norm, lam_q1, lam_k1, lam_q2, lam_k2, b_subln, w_in_c, w_out_c, c_q_norm, c_k_norm, c_rpb, w_router, b_router, w_gate, w_up, w_down):
    scale = HEAD_DIM ** -0.5
    x = jnp.concatenate([x_prompt.reshape(N_PROMPT, D_MODEL), x_sample.reshape(N_SAMPLE, D_MODEL)], axis=0)
    cond = jnp.concatenate(
        [c, c_ctx[None, :], jnp.zeros((COND_ROWS - DEC_BATCH - 1, D_MODEL), F32)], axis=0)
    mod = _modulation(cond, w_mod, b_mod)

    tiles = np.arange(N_TILES)
    samp = np.maximum(tiles - N_PROMPT_TILES, 0)
    per_seq = DEC_SEQ // TM
    tile_cond = jnp.asarray(np.where(tiles < N_PROMPT_TILES, CTX_COND_ROW, samp // per_seq), jnp.int32)
    tile_rope = jnp.asarray(np.where(tiles < N_PROMPT_TILES, per_seq, samp % per_seq), jnp.int32)
    rope = _rope_tables()

    new_caches = []
    for l in range(DEPTH):
        mod_l = mod[l]
        if l % 2 == 0:
            e = l // 2
            lam_init = 0.8 - 0.6 * math.exp(-0.3 * l)
            segments = (
                (A_Q_W, True, 0, 0, None),
                (A_KV_W, True, 1, 0, 2),
                (A_KV_W, False, 1, A_KV_W, 3),
                (B_QK_W, True, 0, A_Q_W, None),
                (B_QK_W, True, 1, 2 * A_KV_W, 4),
                (B_V_W, False, 1, 2 * A_KV_W + B_QK_W, 5),
            )
            hg = _head_gain((
                (A_Q_W, a_q_norm[e], scale), (A_KV_W, a_k_norm[e], 1.0), (A_KV_W, None, 1.0),
                (B_QK_W, b_q_norm[e], scale), (B_QK_W, b_k_norm[e], 1.0), (B_V_W, None, 1.0)))
            outs = _lnproj(x, mod_l, norm_mix[l], w_in_ab[e].astype(BF16), hg, _chunk_plan(segments),
                           A_Q_W + B_QK_W, 2 * A_KV_W + B_QK_W + B_V_W,
                           (A_KV_W, A_KV_W, B_QK_W, B_V_W), True, tile_cond, tile_rope, rope)
            q, kv, ak, av, bk, bv = outs
            new_caches.append((
                ak.reshape(BATCH, SEQ, A_KV_HEADS, HEAD_DIM), av.reshape(BATCH, SEQ, A_KV_HEADS, HEAD_DIM),
                bk.reshape(BATCH, SEQ, B_HEADS, 2, HEAD_DIM), bv.reshape(BATCH, SEQ, B_HEADS, B_V_DIM)))
            lamv = jnp.stack([lam_q1[e], lam_k1[e], lam_q2[e], lam_k2[e]]).astype(F32)
            subln = b_subln[e].reshape(1, B_V_DIM)
            o_p = _attn_ab(q, kv, lamv, subln, lam_init, None)
            caches = (cache_a_k[:, e].reshape(DEC_BATCH, PAST_LEN, A_KV_W),
                      cache_a_v[:, e].reshape(DEC_BATCH, PAST_LEN, A_KV_W),
                      cache_b_k[:, e].reshape(DEC_BATCH, PAST_LEN, B_QK_W),
                      cache_b_v[:, e].reshape(DEC_BATCH, PAST_LEN, B_V_W))
            o_s = _attn_ab(q, kv, lamv, subln, lam_init, caches)
            w_out = w_out_ab[e].astype(BF16)
        else:
            oi = l // 2
            segments = (
                (C_WIDTH, True, 0, 0, None),
                (C_WIDTH, True, 1, 0, 2),
                (C_WIDTH, False, 1, C_WIDTH, 3),
            )
            hg = _head_gain(((C_WIDTH, c_q_norm[oi], scale), (C_WIDTH, c_k_norm[oi], 1.0), (C_WIDTH, None, 1.0)))
            outs = _lnproj(x, mod_l, norm_mix[l], w_in_c[oi].astype(BF16), hg, _chunk_plan(segments),
                           C_WIDTH, 2 * C_WIDTH, (C_WIDTH, C_WIDTH), False, tile_cond, tile_rope, rope)
            q, kv, ck_new, cv_new = outs
            new_caches.append((ck_new.reshape(BATCH, SEQ, C_HEADS, HEAD_DIM),
                               cv_new.reshape(BATCH, SEQ, C_HEADS, HEAD_DIM)))
            o_p = _attn_c_prompt(q, kv)
            o_s = _attn_na(q, kv,
                           cache_c_k[:, oi].reshape(DEC_BATCH, PAST_LEN, C_WIDTH),
                           cache_c_v[:, oi].reshape(DEC_BATCH, PAST_LEN, C_WIDTH),
                           _na_bias_table(c_rpb[oi]))
            w_out = w_out_c[oi].astype(BF16)

        x1, h2, gates, rank = _outproj(o_p, o_s, x, w_out, mod_l, norm_ffn[l], w_router, b_router, tile_cond)
        dests, weights, tile_expert, n_tiles = _route_plan(gates, rank)
        xs = _moe_scatter(dests, h2)
        y = _moe_mlp(l, tile_expert, n_tiles, xs, w_gate, w_up, w_down)
        x = _moe_combine(dests, tile_cond, x1, weights, mod_l, y)

    even = [nc for i, nc in enumerate(new_caches) if i % 2 == 0]
    odd = [nc for i, nc in enumerate(new_caches) if i % 2 == 1]
    stack = lambda items, k: jnp.stack([it[k] for it in items], axis=1)
    return (x[:N_PROMPT].reshape(BATCH, SEQ, D_MODEL), x[N_PROMPT:].reshape(DEC_BATCH, DEC_SEQ, D_MODEL),
            stack(even, 0), stack(even, 1), stack(even, 2), stack(even, 3),
            stack(odd, 0), stack(odd, 1))
```

```python
import functools
import math

import numpy as np
import jax
import jax.numpy as jnp
from jax import lax
from jax.experimental import pallas as pl
from jax.experimental.pallas import tpu as pltpu

F32 = jnp.float32
BF16 = jnp.bfloat16
HIGHEST = lax.Precision.HIGHEST

D_MODEL = 1024
BATCH = 16
SEQ = 256
DEPTH = 2
DEC_BATCH = 8
DEC_SEQ = 1024
PAST_LEN = 256
GRID_W = 64
HEAD_DIM = 64
ROPE_THETA = 10000.0
RMS_EPS = 1e-6
A_Q_HEADS = 8
A_KV_HEADS = 2
A_GROUP = A_Q_HEADS // A_KV_HEADS
B_HEADS = 4
B_V_DIM = 2 * HEAD_DIM
A_Q_W = A_Q_HEADS * HEAD_DIM
A_KV_W = A_KV_HEADS * HEAD_DIM
B_QK_W = B_HEADS * 2 * HEAD_DIM
B_V_W = B_HEADS * B_V_DIM
C_HEADS = 16
C_WIDTH = C_HEADS * HEAD_DIM
NA_ROWS = 8
NA_COLS = 16
N_EXPERTS = 16
N_GROUPS = 4
EXPERTS_PER_GROUP = N_EXPERTS // N_GROUPS
D_EXPERT = 1024

LANES = 128
SUBLANES = 8
ROW_TILES = D_MODEL // LANES
N_PROMPT = BATCH * SEQ
N_SAMPLE = DEC_BATCH * DEC_SEQ
N_TOK = N_PROMPT + N_SAMPLE
TM = 256
N_TILES = N_TOK // TM
N_PROMPT_TILES = N_PROMPT // TM
TM_LN = 512
COND_ROWS = 16
CTX_COND_ROW = DEC_BATCH
MOE_TM = 256
MOE_ROWS = 2 * N_TOK + N_EXPERTS * MOE_TM
MOE_TILES = MOE_ROWS // MOE_TM
GRID_ROWS = DEC_SEQ // GRID_W
NEG_BIG = -1e30
VMEM_LIMIT = 56 * 1024 * 1024


def _silu(x):
    return x * (1.0 / (1.0 + jnp.exp(-x)))


def _dot(a, b):
    return jnp.dot(a, b, preferred_element_type=F32)


def _dot_nt(a, b):
    return lax.dot_general(a, b, (((1,), (1,)), ((), ())), preferred_element_type=F32)


def _low_half(shape):
    return lax.broadcasted_iota(jnp.int32, shape, len(shape) - 1) < HEAD_DIM


def _swap_halves(x):
    return jnp.concatenate([x[:, HEAD_DIM:], x[:, :HEAD_DIM]], axis=1)


def _tile_cond(tm):
    tiles = np.arange(N_TOK // tm)
    npt = N_PROMPT // tm
    samp = np.maximum(tiles - npt, 0)
    per_seq = DEC_SEQ // tm
    cond = np.where(tiles < npt, CTX_COND_ROW, samp // per_seq)
    rope = np.where(tiles < npt, per_seq, samp % per_seq)
    return jnp.asarray(cond, jnp.int32), jnp.asarray(rope, jnp.int32)


def _split_specs(tm, merged):
    npt = N_PROMPT // tm
    base = npt if merged else 0
    a = pl.BlockSpec((tm, D_MODEL), lambda i, *_: (jnp.minimum(i, npt - 1), 0))
    b = pl.BlockSpec((tm, D_MODEL), lambda i, *_: (jnp.maximum(i - npt, 0) + base, 0))
    return a, b


def _mod_kernel(c_ref, w_ref, b_ref, o_ref):
    s = _silu(c_ref[...])
    o_ref[...] = jnp.dot(s, w_ref[...], precision=HIGHEST, preferred_element_type=F32) + b_ref[...]


def _modulation(cond, w_mod, b_mod):
    tn = 1536
    return pl.pallas_call(
        _mod_kernel,
        out_shape=jax.ShapeDtypeStruct((DEPTH, COND_ROWS, 6 * D_MODEL), F32),
        grid=(DEPTH, 6 * D_MODEL // tn),
        in_specs=[
            pl.BlockSpec((COND_ROWS, D_MODEL), lambda l, j: (0, 0)),
            pl.BlockSpec((None, D_MODEL, tn), lambda l, j: (l, 0, j)),
            pl.BlockSpec((None, 1, tn), lambda l, j: (l, 0, j)),
        ],
        out_specs=pl.BlockSpec((None, COND_ROWS, tn), lambda l, j: (l, 0, j)),
        compiler_params=pltpu.CompilerParams(
            dimension_semantics=("arbitrary", "arbitrary"), vmem_limit_bytes=VMEM_LIMIT),
        name="modulation",
    )(cond, w_mod, b_mod.reshape(DEPTH, 1, 6 * D_MODEL))


def _rope_tables(tm):
    pos = np.arange(DEC_SEQ)
    rows = (pos // GRID_W).astype(np.float64)
    cols = (pos % GRID_W).astype(np.float64)
    nfreq = HEAD_DIM // 4
    inv = ROPE_THETA ** (-np.arange(nfreq, dtype=np.float64) / nfreq)
    d = np.arange(HEAD_DIM)
    dd = d % (HEAD_DIM // 2)
    p = np.where((d >= HEAD_DIM // 2)[None, :], cols[:, None], rows[:, None])
    ang = p * inv[dd % nfreq][None, :]
    cos, sin = np.cos(ang), np.sin(ang)
    second = (dd >= nfreq)[None, :]
    sa = np.where(second, sin, 0.0)
    sb = np.where(second, 0.0, -sin)

    def full(t, ident):
        t = np.concatenate([t, np.full((tm, HEAD_DIM), ident)], axis=0)
        return jnp.asarray(np.tile(t, (1, LANES // HEAD_DIM)), dtype=F32)

    return full(cos, 1.0), full(sa, 0.0), full(sb, 0.0)


def _lnproj_kernel(tile_cond_ref, tile_rope_ref, xa_ref, xb_ref, mod_ref, g_ref, w_ref, hg_ref, gmat_ref,
                   cos_ref, sa_ref, sb_ref, *out_refs, chunks, use_rope):
    i = pl.program_id(0)
    npt = N_PROMPT // TM_LN
    row = tile_cond_ref[i]
    x = jnp.where(i < npt, xa_ref[...], xb_ref[...])
    ms = jnp.mean(x * x, axis=-1, keepdims=True)
    xn = x * lax.rsqrt(ms + RMS_EPS) * g_ref[...]
    shift = mod_ref[pl.ds(row, 1), 0:D_MODEL]
    scale = mod_ref[pl.ds(row, 1), D_MODEL:2 * D_MODEL]
    h = (xn * (1.0 + scale) + shift).astype(BF16)
    for piece in range(len(chunks) // 2):
        col0 = piece * 2 * LANES
        y2 = _dot(h, w_ref[:, col0:col0 + 2 * LANES])
        if chunks[2 * piece][0] or chunks[2 * piece + 1][0]:
            gs = _dot((y2 * y2).astype(BF16), gmat_ref[...])
            yn2 = y2 * lax.rsqrt(gs * (1.0 / HEAD_DIM) + RMS_EPS)
        for sub in range(2):
            c = piece * 2 + sub
            normed, dst, dst_col, f32_dst, f32_col = chunks[c]
            lanes = slice(sub * LANES, (sub + 1) * LANES)
            if normed:
                y = yn2[:, lanes] * hg_ref[:, c * LANES:(c + 1) * LANES]
            else:
                y = y2[:, lanes]
            if f32_dst is not None:
                yf = y

                @pl.when(i < npt)
                def _():
                    out_refs[f32_dst][:, f32_col:f32_col + LANES] = yf
            if normed and use_rope:
                y = (y * cos_ref[...] + pltpu.roll(y, HEAD_DIM // 4, 1) * sa_ref[...]
                     + pltpu.roll(y, LANES - HEAD_DIM // 4, 1) * sb_ref[...])
            out_refs[dst][:, dst_col:dst_col + LANES] = y.astype(BF16)


def _lnproj(xa, xb, merged, mod_l, gain, w_bf16, head_gain, chunks, q_w, kv_w, f32_widths, use_rope):
    dout = w_bf16.shape[1]
    tm = TM_LN
    npt = N_PROMPT // tm
    tile_cond, tile_rope = _tile_cond(tm)
    cos, sa, sb = _rope_tables(tm)
    gmat = jnp.asarray(np.kron(np.eye(2 * LANES // HEAD_DIM), np.ones((HEAD_DIM, HEAD_DIM))), dtype=BF16)
    const = lambda i, *_: (0, 0)
    tok = lambda i, *_: (i, 0)
    rope_map = lambda i, tc, tr: (tr[i], 0)
    prm = lambda i, *_: (jnp.minimum(i, npt - 1), 0)
    xa_spec, xb_spec = _split_specs(tm, merged)
    out_shape = [jax.ShapeDtypeStruct((N_TOK, q_w), BF16), jax.ShapeDtypeStruct((N_TOK, kv_w), BF16)]
    out_specs = [pl.BlockSpec((tm, q_w), tok), pl.BlockSpec((tm, kv_w), tok)]
    for wd in f32_widths:
        out_shape.append(jax.ShapeDtypeStruct((N_PROMPT, wd), F32))
        out_specs.append(pl.BlockSpec((tm, wd), prm))
    return pl.pallas_call(
        functools.partial(_lnproj_kernel, chunks=chunks, use_rope=use_rope),
        out_shape=out_shape,
        grid_spec=pltpu.PrefetchScalarGridSpec(
            num_scalar_prefetch=2,
            grid=(N_TOK // tm,),
            in_specs=[
                xa_spec, xb_spec,
                pl.BlockSpec((COND_ROWS, 6 * D_MODEL), const),
                pl.BlockSpec((1, D_MODEL), const),
                pl.BlockSpec((D_MODEL, dout), const),
                pl.BlockSpec((1, dout), const),
                pl.BlockSpec((2 * LANES, 2 * LANES), const),
                pl.BlockSpec((tm, LANES), rope_map),
                pl.BlockSpec((tm, LANES), rope_map),
                pl.BlockSpec((tm, LANES), rope_map),
            ],
            out_specs=out_specs,
        ),
        compiler_params=pltpu.CompilerParams(
            dimension_semantics=("arbitrary",), vmem_limit_bytes=VMEM_LIMIT),
        name="lnproj",
    )(tile_cond, tile_rope, xa, xb, mod_l, gain.reshape(1, D_MODEL), w_bf16, head_gain, gmat, cos, sa, sb)


def _softmax_parts(q, k_new, k_ctx):
    s_n = _dot_nt(q, k_new)
    m = jnp.max(s_n, axis=-1, keepdims=True)
    if k_ctx is not None:
        s_c = _dot_nt(q, k_ctx)
        m = jnp.maximum(m, jnp.max(s_c, axis=-1, keepdims=True))
    p_n = jnp.exp(s_n - m)
    l = jnp.sum(p_n, axis=-1, keepdims=True)
    p_c = None
    if k_ctx is not None:
        p_c = jnp.exp(s_c - m)
        l = l + jnp.sum(p_c, axis=-1, keepdims=True)
    return p_n, p_c, l


def _gqa_pairs(q_ref, q_col0, kv_ref, k_col0, v_col0, ck_ref, cv_ref, n_kv, group, tq):
    low = _low_half((tq, LANES))
    head_out = [None] * (n_kv * group)
    for pair in range(n_kv // 2):
        lanes = slice(pair * LANES, (pair + 1) * LANES)
        k_n = kv_ref[:, k_col0 + lanes.start:k_col0 + lanes.stop]
        v_n = kv_ref[:, v_col0 + lanes.start:v_col0 + lanes.stop]
        k_c = v_c = None
        if ck_ref is not None:
            k_c = ck_ref[:, lanes].astype(BF16)
            v_c = cv_ref[:, lanes].astype(BF16)
        for half in range(2):
            kvh = 2 * pair + half
            keep = low if half == 0 else jnp.logical_not(low)
            qs = []
            for g in range(group):
                head = kvh * group + g
                blk = q_ref[:, q_col0 + (head // 2) * LANES:q_col0 + (head // 2 + 1) * LANES]
                if head % 2 != half:
                    blk = _swap_halves(blk)
                qs.append(jnp.where(keep, blk, jnp.zeros_like(blk)))
            q = qs[0] if group == 1 else jnp.concatenate(qs, axis=0)
            p_n, p_c, l = _softmax_parts(q, k_n, k_c)
            o = _dot(p_n.astype(BF16), v_n)
            if p_c is not None:
                o = o + _dot(p_c.astype(BF16), v_c)
            o = o * (1.0 / l)
            for g in range(group):
                head = kvh * group + g
                og = o[g * tq:(g + 1) * tq]
                if head % 2 != half:
                    og = pltpu.roll(og, HEAD_DIM, 1)
                head_out[head] = og
    return [jnp.where(low, head_out[2 * k], head_out[2 * k + 1]) for k in range(n_kv * group // 2)]


def _attn_ab_kernel(*refs, has_cache, tq, lam_init):
    if has_cache:
        q_ref, kv_ref, cak_ref, cav_ref, cbk_ref, cbv_ref, lamv_ref, subln_ref, o_ref = refs
    else:
        q_ref, kv_ref, lamv_ref, subln_ref, o_ref = refs
        cak_ref = cav_ref = cbk_ref = cbv_ref = None
    outs = _gqa_pairs(q_ref, 0, kv_ref, 0, A_KV_W, cak_ref, cav_ref, A_KV_HEADS, A_GROUP, tq)

    lv = lamv_ref[...]
    l1 = jnp.sum(lv[0:1] * lv[1:2], axis=-1, keepdims=True)
    l2 = jnp.sum(lv[2:3] * lv[3:4], axis=-1, keepdims=True)
    lam = jnp.exp(l1) - jnp.exp(l2) + lam_init
    bk0 = 2 * A_KV_W
    bv0 = bk0 + B_QK_W
    low = _low_half((tq, LANES))
    for h in range(B_HEADS):
        lanes = slice(h * LANES, (h + 1) * LANES)
        qp = q_ref[:, A_Q_W + lanes.start:A_Q_W + lanes.stop]
        zero = jnp.zeros_like(qp)
        q = jnp.concatenate([jnp.where(low, qp, zero), jnp.where(low, zero, qp)], axis=0)
        k_n = kv_ref[:, bk0 + lanes.start:bk0 + lanes.stop]
        v_n = kv_ref[:, bv0 + lanes.start:bv0 + lanes.stop]
        k_c = v_c = None
        if has_cache:
            k_c = cbk_ref[:, lanes].astype(BF16)
            v_c = cbv_ref[:, lanes].astype(BF16)
        p_n, p_c, l = _softmax_parts(q, k_n, k_c)
        r = 1.0 / l
        r1 = r[:tq]
        r2 = lam * r[tq:]
        o = _dot((p_n[:tq] * r1 - p_n[tq:] * r2).astype(BF16), v_n)
        if has_cache:
            o = o + _dot((p_c[:tq] * r1 - p_c[tq:] * r2).astype(BF16), v_c)
        ms = jnp.mean(o * o, axis=-1, keepdims=True)
        o = o * lax.rsqrt(ms + RMS_EPS) * subln_ref[...] * (1.0 - lam_init)
        outs.append(o)
    o_ref[...] = jnp.concatenate(outs, axis=1).astype(BF16)


def _attn_ab(q, kv, lamv, subln, lam_init, caches):
    kv_w = kv.shape[1]
    kern = functools.partial(_attn_ab_kernel, lam_init=lam_init)
    const = lambda b, j: (0, 0)
    cp = pltpu.CompilerParams(dimension_semantics=("arbitrary", "arbitrary"), vmem_limit_bytes=VMEM_LIMIT)
    if caches is None:
        tq = SEQ
        return pl.pallas_call(
            functools.partial(kern, has_cache=False, tq=tq),
            out_shape=jax.ShapeDtypeStruct((N_PROMPT, D_MODEL), BF16),
            grid=(BATCH, 1),
            in_specs=[
                pl.BlockSpec((tq, D_MODEL), lambda b, j: (b, 0)),
                pl.BlockSpec((SEQ, kv_w), lambda b, j: (b, 0)),
                pl.BlockSpec((4, HEAD_DIM), const),
                pl.BlockSpec((1, B_V_DIM), const),
            ],
            out_specs=pl.BlockSpec((tq, D_MODEL), lambda b, j: (b, 0)),
            compiler_params=cp,
            name="attn_ab_prompt",
        )(q, kv, lamv, subln)
    tq = 128
    nq = DEC_SEQ // tq
    q0 = N_PROMPT // tq
    kv0 = N_PROMPT // DEC_SEQ
    cak, cav, cbk, cbv = caches
    cspec = lambda w: pl.BlockSpec((None, PAST_LEN, w), lambda b, j: (b, 0, 0))
    return pl.pallas_call(
        functools.partial(kern, has_cache=True, tq=tq),
        out_shape=jax.ShapeDtypeStruct((N_SAMPLE, D_MODEL), BF16),
        grid=(DEC_BATCH, nq),
        in_specs=[
            pl.BlockSpec((tq, D_MODEL), lambda b, j: (q0 + b * nq + j, 0)),
            pl.BlockSpec((DEC_SEQ, kv_w), lambda b, j: (kv0 + b, 0)),
            cspec(A_KV_W), cspec(A_KV_W), cspec(B_QK_W), cspec(B_V_W),
            pl.BlockSpec((4, HEAD_DIM), const),
            pl.BlockSpec((1, B_V_DIM), const),
        ],
        out_specs=pl.BlockSpec((tq, D_MODEL), lambda b, j: (b * nq + j, 0)),
        compiler_params=cp,
        name="attn_ab_sample",
    )(q, kv, cak, cav, cbk, cbv, lamv, subln)


def _attn_c_prompt_kernel(q_ref, kv_ref, o_ref):
    outs = _gqa_pairs(q_ref, 0, kv_ref, 0, C_WIDTH, None, None, C_HEADS, 1, SEQ)
    o_ref[...] = jnp.concatenate(outs, axis=1).astype(BF16)


def _attn_c_prompt(q, kv):
    return pl.pallas_call(
        _attn_c_prompt_kernel,
        out_shape=jax.ShapeDtypeStruct((N_PROMPT, D_MODEL), BF16),
        grid=(BATCH,),
        in_specs=[
            pl.BlockSpec((SEQ, C_WIDTH), lambda b: (b, 0)),
            pl.BlockSpec((SEQ, 2 * C_WIDTH), lambda b: (b, 0)),
        ],
        out_specs=pl.BlockSpec((SEQ, C_WIDTH), lambda b: (b, 0)),
        compiler_params=pltpu.CompilerParams(dimension_semantics=("arbitrary",), vmem_limit_bytes=VMEM_LIMIT),
        name="attn_c_prompt",
    )(q, kv)


def _na_kernel(q_ref, kv_ref, ck_ref, cv_ref, tp_ref, o_ref, ckb, cvb):
    r = pl.program_id(1)

    @pl.when(r == 0)
    def _():
        ckb[...] = ck_ref[...].astype(BF16)
        cvb[...] = cv_ref[...].astype(BF16)

    kh = min(NA_ROWS, GRID_ROWS)
    rs = jnp.clip(r - kh // 2, 0, GRID_ROWS - kh)
    ro0 = rs - r + (NA_ROWS - 1)
    start = pl.multiple_of(rs * GRID_W, GRID_W)
    win = kh * GRID_W
    low = _low_half((GRID_W, LANES))
    n_pair = C_HEADS // 2
    s_lat, s_ctx = [], []
    for j in range(n_pair):
        lanes = slice(j * LANES, (j + 1) * LANES)
        qp = q_ref[:, lanes]
        zero = jnp.zeros_like(qp)
        q = jnp.concatenate([jnp.where(low, qp, zero), jnp.where(low, zero, qp)], axis=0)
        kw = kv_ref[pl.ds(start, win), lanes]
        bias = jnp.concatenate(
            [jnp.concatenate([tp_ref[2 * j + hh, ro0 + 2 * t] for t in range(kh // 2)], axis=1)
             for hh in range(2)], axis=0)
        s_lat.append(_dot_nt(q, kw) + bias)
        s_ctx.append(_dot_nt(q, ckb[:, lanes]))
    outs = []
    for j in range(n_pair):
        lanes = slice(j * LANES, (j + 1) * LANES)
        s_l, s_c = s_lat[j], s_ctx[j]
        m = jnp.maximum(jnp.max(s_l, axis=-1, keepdims=True), jnp.max(s_c, axis=-1, keepdims=True))
        p_l = jnp.exp(s_l - m)
        p_c = jnp.exp(s_c - m)
        l = jnp.sum(p_l, axis=-1, keepdims=True) + jnp.sum(p_c, axis=-1, keepdims=True)
        vw = kv_ref[pl.ds(start, win), C_WIDTH + lanes.start:C_WIDTH + lanes.stop]
        o = _dot(p_c.astype(BF16), cvb[:, lanes]) + _dot(p_l.astype(BF16), vw)
        o = o * (1.0 / l)
        outs.append(jnp.where(low, o[:GRID_W], o[GRID_W:]))
    o_ref[...] = jnp.concatenate(outs, axis=1).astype(BF16)


def _na_bias_table(rpb):
    cols = np.arange(GRID_W)
    col_start = np.clip(cols - NA_COLS // 2, 0, GRID_W - NA_COLS)
    col_in = (cols[None, :] >= col_start[:, None]) & (cols[None, :] < col_start[:, None] + NA_COLS)
    col_off = np.clip(cols[None, :] - cols[:, None], -(NA_COLS - 1), NA_COLS - 1) + (NA_COLS - 1)
    t = rpb.astype(F32)[:, :, col_off]
    t = jnp.where(jnp.asarray(col_in)[None, None], t, NEG_BIG)
    return jnp.concatenate([t[:, :-1], t[:, 1:]], axis=-1)


def _attn_na(q, kv, ck, cv, tp):
    q0 = N_PROMPT // GRID_W
    kv0 = N_PROMPT // DEC_SEQ
    return pl.pallas_call(
        _na_kernel,
        out_shape=jax.ShapeDtypeStruct((N_SAMPLE, D_MODEL), BF16),
        grid=(DEC_BATCH, GRID_ROWS),
        in_specs=[
            pl.BlockSpec((GRID_W, C_WIDTH), lambda b, r: (q0 + b * GRID_ROWS + r, 0)),
            pl.BlockSpec((DEC_SEQ, 2 * C_WIDTH), lambda b, r: (kv0 + b, 0)),
            pl.BlockSpec((None, PAST_LEN, C_WIDTH), lambda b, r: (b, 0, 0)),
            pl.BlockSpec((None, PAST_LEN, C_WIDTH), lambda b, r: (b, 0, 0)),
            pl.BlockSpec(tp.shape, lambda b, r: (0, 0, 0, 0)),
        ],
        out_specs=pl.BlockSpec((GRID_W, C_WIDTH), lambda b, r: (b * GRID_ROWS + r, 0)),
        scratch_shapes=[pltpu.VMEM((PAST_LEN, C_WIDTH), BF16), pltpu.VMEM((PAST_LEN, C_WIDTH), BF16)],
        compiler_params=pltpu.CompilerParams(
            dimension_semantics=("arbitrary", "arbitrary"), vmem_limit_bytes=VMEM_LIMIT),
        name="attn_na_sample",
    )(q, kv, ck, cv, tp)


def _first_wins_ranks(vals):
    ranks = []
    for i in range(len(vals)):
        r = jnp.zeros_like(vals[i])
        for j in range(len(vals)):
            if j == i:
                continue
            beats = (vals[j] >= vals[i]) if j < i else (vals[j] > vals[i])
            r = r + jnp.where(beats, 1.0, 0.0)
        ranks.append(r)
    return ranks


def _outproj_kernel(tile_cond_ref, op_ref, os_ref, xa_ref, xb_ref, w_ref, mod_ref, g2_ref, wrt_ref, br_ref,
                    tri_ref, x1_ref, h2_ref, gates_ref, rank_ref, carry_ref):
    i = pl.program_id(0)
    row = tile_cond_ref[i]
    is_prompt = i < N_PROMPT_TILES

    @pl.when(i == 0)
    def _():
        carry_ref[...] = jnp.zeros_like(carry_ref)

    o = jnp.where(is_prompt, op_ref[...], os_ref[...])
    x = jnp.where(is_prompt, xa_ref[...], xb_ref[...])
    acc = _dot(o, w_ref[...])
    gate = mod_ref[pl.ds(row, 1), 2 * D_MODEL:3 * D_MODEL]
    x1 = x + gate * acc
    x1_ref[...] = x1
    ms = jnp.mean(x1 * x1, axis=-1, keepdims=True)
    xn = x1 * lax.rsqrt(ms + RMS_EPS) * g2_ref[...]
    shift = mod_ref[pl.ds(row, 1), 3 * D_MODEL:4 * D_MODEL]
    scale = mod_ref[pl.ds(row, 1), 4 * D_MODEL:5 * D_MODEL]
    h2 = xn * (1.0 + scale) + shift
    for s in range(ROW_TILES):
        h2_ref[pl.ds(s, TM, stride=ROW_TILES), :] = h2[:, s * LANES:(s + 1) * LANES]

    logits = lax.dot_general(wrt_ref[...], h2, (((1,), (1,)), ((), ())), precision=HIGHEST,
                             preferred_element_type=F32)
    e = jnp.exp(logits - jnp.max(logits, axis=0, keepdims=True))
    scores = e * (1.0 / jnp.sum(e, axis=0, keepdims=True))
    sel = scores + br_ref[...]
    sel_rows = [sel[k:k + 1, :] for k in range(N_EXPERTS)]
    in_top2 = []
    group_sum = []
    for g in range(N_GROUPS):
        vals = sel_rows[g * EXPERTS_PER_GROUP:(g + 1) * EXPERTS_PER_GROUP]
        ranks = _first_wins_ranks(vals)
        top = [rk < 2.0 for rk in ranks]
        in_top2.extend(top)
        s = jnp.zeros_like(vals[0])
        for v, t in zip(vals, top):
            s = s + jnp.where(t, v, 0.0)
        group_sum.append(s)
    group_rank = _first_wins_ranks(group_sum)
    mask_rows = []
    for k in range(N_EXPERTS):
        chosen = jnp.where(in_top2[k], 1.0, 0.0) * jnp.where(group_rank[k // EXPERTS_PER_GROUP] < 1.0, 1.0, 0.0)
        mask_rows.append(chosen)
    mask = jnp.concatenate(mask_rows, axis=0)
    picked = scores * mask
    gates_ref[...] = picked * (1.0 / jnp.sum(picked, axis=0, keepdims=True))
    prefix = _dot(mask.astype(BF16), tri_ref[...])
    rank_ref[...] = jnp.where(mask > 0.0, prefix + carry_ref[...], -1.0)
    carry_ref[...] = carry_ref[...] + jnp.sum(mask, axis=1, keepdims=True)


def _outproj(o_prompt, o_sample, xa, xb, merged, w_bf16, mod_l, gain2, w_router, b_router):
    const = lambda i, *_: (0, 0)
    tok = lambda i, *_: (i, 0)
    tokT = lambda i, *_: (0, i)
    tile_cond, _ = _tile_cond(TM)
    op_spec, os_spec = _split_specs(TM, False)
    xa_spec, xb_spec = _split_specs(TM, merged)
    tri = jnp.asarray(np.triu(np.ones((TM, TM)), k=1), dtype=BF16)
    return pl.pallas_call(
        _outproj_kernel,
        out_shape=[
            jax.ShapeDtypeStruct((N_TOK, D_MODEL), F32),
            jax.ShapeDtypeStruct((N_TOK * ROW_TILES, LANES), F32),
            jax.ShapeDtypeStruct((N_EXPERTS, N_TOK), F32),
            jax.ShapeDtypeStruct((N_EXPERTS, N_TOK), F32),
        ],
        grid_spec=pltpu.PrefetchScalarGridSpec(
            num_scalar_prefetch=1,
            grid=(N_TILES,),
            in_specs=[
                op_spec, os_spec, xa_spec, xb_spec,
                pl.BlockSpec((D_MODEL, D_MODEL), const),
                pl.BlockSpec((COND_ROWS, 6 * D_MODEL), const),
                pl.BlockSpec((1, D_MODEL), const),
                pl.BlockSpec((N_EXPERTS, D_MODEL), const),
                pl.BlockSpec((N_EXPERTS, 1), const),
                pl.BlockSpec((TM, TM), const),
            ],
            out_specs=[
                pl.BlockSpec((TM, D_MODEL), tok),
                pl.BlockSpec((TM * ROW_TILES, LANES), tok),
                pl.BlockSpec((N_EXPERTS, TM), tokT),
                pl.BlockSpec((N_EXPERTS, TM), tokT),
            ],
            scratch_shapes=[pltpu.VMEM((N_EXPERTS, 1), F32)],
        ),
        compiler_params=pltpu.CompilerParams(
            dimension_semantics=("arbitrary",), vmem_limit_bytes=VMEM_LIMIT),
        name="outproj_router",
    )(tile_cond, o_prompt, o_sample, xa, xb, w_bf16, mod_l, gain2.reshape(1, D_MODEL), w_router.T,
      b_router.reshape(N_EXPERTS, 1), tri)


def _route_plan(gates, rank):
    sel = rank >= 0.0
    counts = jnp.sum(sel, axis=1).astype(jnp.int32)
    padded = ((counts + MOE_TM - 1) // MOE_TM) * MOE_TM
    ends = jnp.cumsum(padded).astype(jnp.int32)
    offs = ends - padded
    dest = jnp.where(sel, offs[:, None] + rank.astype(jnp.int32), -1)
    d1 = jnp.max(dest, axis=0)
    d0 = jnp.min(jnp.where(sel, dest, MOE_ROWS), axis=0)
    w0 = jnp.sum(jnp.where(dest == d0[None, :], gates, 0.0), axis=0)
    w1 = jnp.sum(jnp.where(dest == d1[None, :], gates, 0.0), axis=0)
    n_tiles = (ends[-1] // MOE_TM).astype(jnp.int32).reshape(1)
    tile_start = jnp.arange(MOE_TILES, dtype=jnp.int32) * MOE_TM
    tile_expert = jnp.sum(tile_start[:, None] >= ends[None, :], axis=1).astype(jnp.int32)
    tile_expert = jnp.minimum(tile_expert, N_EXPERTS - 1)
    dests = (jnp.concatenate([d0, d1]) * ROW_TILES).astype(jnp.int32)
    weights = jnp.stack([w0, w1], axis=1)
    last_tile = jnp.where(padded > 0, ends - MOE_TM, -1)
    spare = n_tiles[0] + jnp.arange(N_EXPERTS, dtype=jnp.int32)
    spare = jnp.where(spare < MOE_TILES, spare * MOE_TM, -1)
    clear = (jnp.concatenate([last_tile, spare]) * ROW_TILES).astype(jnp.int32)
    return dests, weights, tile_expert, n_tiles, clear


def _scatter_kernel(d_ref, last_ref, h_ref, xs_ref, zeros, sem, zsem):
    i = pl.program_id(0)
    base = i * TM
    rows = MOE_TM * ROW_TILES

    @pl.when(i == 0)
    def _():
        zeros[...] = jnp.zeros_like(zeros)
        for e in range(2 * N_EXPERTS):
            @pl.when(last_ref[e] >= 0)
            def _():
                at = pl.multiple_of(last_ref[e], SUBLANES)
                pltpu.make_async_copy(zeros, xs_ref.at[pl.ds(at, rows)], zsem).start()
        for e in range(2 * N_EXPERTS):
            @pl.when(last_ref[e] >= 0)
            def _():
                pltpu.make_async_copy(zeros, xs_ref.at[pl.ds(0, rows)], zsem).wait()

    def copy(r, t):
        src = h_ref.at[pl.ds(pl.multiple_of(r * ROW_TILES, SUBLANES), ROW_TILES)]
        dst = xs_ref.at[pl.ds(pl.multiple_of(t, SUBLANES), ROW_TILES)]
        return pltpu.make_async_copy(src, dst, sem)

    def issue(r, c):
        copy(r, d_ref[base + r]).start()
        copy(r, d_ref[N_TOK + base + r]).start()
        return c

    lax.fori_loop(0, TM, issue, 0, unroll=8)

    def drain(r, c):
        copy(r, 0).wait()
        copy(r, 0).wait()
        return c

    lax.fori_loop(0, TM, drain, 0, unroll=8)


def _moe_scatter(dests, last_tile, h2):
    return pl.pallas_call(
        _scatter_kernel,
        out_shape=jax.ShapeDtypeStruct((MOE_ROWS * ROW_TILES, LANES), F32),
        grid_spec=pltpu.PrefetchScalarGridSpec(
            num_scalar_prefetch=2,
            grid=(N_TILES,),
            in_specs=[pl.BlockSpec((TM * ROW_TILES, LANES), lambda i, *_: (i, 0))],
            out_specs=pl.BlockSpec(memory_space=pl.ANY),
            scratch_shapes=[
                pltpu.VMEM((MOE_TM * ROW_TILES, LANES), F32),
                pltpu.SemaphoreType.DMA,
                pltpu.SemaphoreType.DMA,
            ],
        ),
        compiler_params=pltpu.CompilerParams(
            dimension_semantics=("arbitrary",), vmem_limit_bytes=VMEM_LIMIT, has_side_effects=True),
        name="moe_scatter",
    )(dests, last_tile, h2)


def _mlp_kernel(te_ref, nv_ref, xs_ref, wg_ref, wu_ref, wd_ref, y_ref, wgb, wub, wdb, hb):
    t = pl.program_id(0)

    @pl.when(t < nv_ref[0])
    def _():
        e = te_ref[t]
        prev = te_ref[jnp.maximum(t - 1, 0)]

        @pl.when((t == 0) | (e != prev))
        def _():
            wgb[...] = wg_ref[...].astype(BF16)
            wub[...] = wu_ref[...].astype(BF16)
            wdb[...] = wd_ref[...].astype(BF16)

        x = jnp.concatenate(
            [xs_ref[pl.ds(s, MOE_TM, stride=ROW_TILES), :] for s in range(ROW_TILES)], axis=1).astype(BF16)
        step = 512
        for c in range(0, D_EXPERT, step):
            g = _dot(x, wgb[:, c:c + step])
            u = _dot(x, wub[:, c:c + step])
            hb[:, c:c + step] = (_silu(g) * u).astype(BF16)
        y = _dot(hb[...], wdb[...])
        for s in range(ROW_TILES):
            y_ref[pl.ds(s, MOE_TM, stride=ROW_TILES), :] = y[:, s * LANES:(s + 1) * LANES]

    @pl.when(t >= nv_ref[0])
    def _():
        y_ref[...] = jnp.zeros_like(y_ref)


def _moe_mlp(layer, tile_expert, n_tiles, xs, w_gate, w_up, w_down):
    def tile_map(t, te, nv):
        return (jnp.minimum(t, nv[0] - 1), 0)

    def out_map(t, te, nv):
        return (t, 0)

    def w_map(t, te, nv):
        return (layer, te[jnp.minimum(t, nv[0] - 1)], 0, 0)

    wspec = lambda a, b: pl.BlockSpec((None, None, a, b), w_map)
    return pl.pallas_call(
        _mlp_kernel,
        out_shape=jax.ShapeDtypeStruct((MOE_ROWS * ROW_TILES, LANES), F32),
        grid_spec=pltpu.PrefetchScalarGridSpec(
            num_scalar_prefetch=2,
            grid=(MOE_TILES,),
            in_specs=[
                pl.BlockSpec((MOE_TM * ROW_TILES, LANES), tile_map),
                wspec(D_MODEL, D_EXPERT), wspec(D_MODEL, D_EXPERT), wspec(D_EXPERT, D_MODEL),
            ],
            out_specs=pl.BlockSpec((MOE_TM * ROW_TILES, LANES), out_map),
            scratch_shapes=[
                pltpu.VMEM((D_MODEL, D_EXPERT), BF16),
                pltpu.VMEM((D_MODEL, D_EXPERT), BF16),
                pltpu.VMEM((D_EXPERT, D_MODEL), BF16),
                pltpu.VMEM((MOE_TM, D_EXPERT), BF16),
            ],
        ),
        compiler_params=pltpu.CompilerParams(
            dimension_semantics=("arbitrary",), vmem_limit_bytes=VMEM_LIMIT),
        name="moe_mlp",
    )(tile_expert, n_tiles, xs, w_gate, w_up, w_down)


def _combine_kernel(d_ref, tile_cond_ref, x1_ref, w_ref, mod_ref, y_ref, *rest, split_out):
    if split_out:
        outp_ref, outs_ref, buf, sem = rest
    else:
        out_ref, buf, sem = rest
    i = pl.program_id(0)
    base = i * TM
    row = tile_cond_ref[i]

    def copy(k, r, t):
        src = y_ref.at[pl.ds(pl.multiple_of(t, SUBLANES), ROW_TILES)]
        dst = buf.at[k, pl.ds(pl.multiple_of(r * ROW_TILES, SUBLANES), ROW_TILES)]
        return pltpu.make_async_copy(src, dst, sem)

    def issue(r, c):
        copy(0, r, d_ref[base + r]).start()
        copy(1, r, d_ref[N_TOK + base + r]).start()
        return c

    lax.fori_loop(0, TM, issue, 0, unroll=8)

    def drain(r, c):
        copy(0, r, 0).wait()
        copy(1, r, 0).wait()
        return c

    lax.fori_loop(0, TM, drain, 0, unroll=8)
    w = w_ref[...]
    w0 = w[:, 0:1]
    w1 = w[:, 1:2]
    parts = []
    for s in range(ROW_TILES):
        y0 = buf[0, pl.ds(s, TM, stride=ROW_TILES), :]
        y1 = buf[1, pl.ds(s, TM, stride=ROW_TILES), :]
        parts.append(w0 * y0 + w1 * y1)
    gate = mod_ref[pl.ds(row, 1), 5 * D_MODEL:6 * D_MODEL]
    out = x1_ref[...] + gate * jnp.concatenate(parts, axis=1)
    if split_out:
        @pl.when(i < N_PROMPT_TILES)
        def _():
            outp_ref[...] = out

        @pl.when(i >= N_PROMPT_TILES)
        def _():
            outs_ref[...] = out
    else:
        out_ref[...] = out


def _moe_combine(dests, x1, weights, mod_l, y, split_out):
    tile_cond, _ = _tile_cond(TM)
    if split_out:
        out_shape = [jax.ShapeDtypeStruct((N_PROMPT, D_MODEL), F32), jax.ShapeDtypeStruct((N_SAMPLE, D_MODEL), F32)]
        out_specs = list(_split_specs(TM, False))
    else:
        out_shape = jax.ShapeDtypeStruct((N_TOK, D_MODEL), F32)
        out_specs = pl.BlockSpec((TM, D_MODEL), lambda i, *_: (i, 0))
    return pl.pallas_call(
        functools.partial(_combine_kernel, split_out=split_out),
        out_shape=out_shape,
        grid_spec=pltpu.PrefetchScalarGridSpec(
            num_scalar_prefetch=2,
            grid=(N_TILES,),
            in_specs=[
                pl.BlockSpec((TM, D_MODEL), lambda i, *_: (i, 0)),
                pl.BlockSpec((TM, 2), lambda i, *_: (i, 0)),
                pl.BlockSpec((COND_ROWS, 6 * D_MODEL), lambda i, *_: (0, 0)),
                pl.BlockSpec(memory_space=pl.ANY),
            ],
            out_specs=out_specs,
            scratch_shapes=[pltpu.VMEM((2, TM * ROW_TILES, LANES), F32), pltpu.SemaphoreType.DMA],
        ),
        compiler_params=pltpu.CompilerParams(
            dimension_semantics=("arbitrary",), vmem_limit_bytes=VMEM_LIMIT),
        name="moe_combine",
    )(dests, tile_cond, x1, weights, mod_l, y)


def _chunk_plan(segments):
    chunks = []
    for width, normed, dst, dst_col0, f32_dst in segments:
        for k in range(width // LANES):
            chunks.append((normed, dst, dst_col0 + k * LANES, f32_dst, k * LANES))
    return tuple(chunks)


def _head_gain(parts):
    cols = []
    for width, g, mult in parts:
        if g is None:
            cols.append(jnp.ones((width,), F32))
        else:
            cols.append(jnp.tile(g.astype(F32) * mult, width // HEAD_DIM))
    return jnp.concatenate(cols).reshape(1, -1)


def kernel(x_prompt, x_sample, cache_a_k, cache_a_v, cache_b_k, cache_b_v, cache_c_k, cache_c_v, c, c_ctx, w_mod, b_mod, norm_mix, norm_ffn, w_in_ab, w_out_ab, a_q_norm, a_k_norm, b_q_norm, b_k_norm, lam_q1, lam_k1, lam_q2, lam_k2, b_subln, w_in_c, w_out_c, c_q_norm, c_k_norm, c_rpb, w_router, b_router, w_gate, w_up, w_down):
    scale = HEAD_DIM ** -0.5
    cond = jnp.concatenate(
        [c, c_ctx[None, :], jnp.zeros((COND_ROWS - DEC_BATCH - 1, D_MODEL), F32)], axis=0)
    mod = _modulation(cond, w_mod, b_mod)

    xa = x_prompt.reshape(N_PROMPT, D_MODEL)
    xb = x_sample.reshape(N_SAMPLE, D_MODEL)
    merged = False
    new_caches = []
    for l in range(DEPTH):
        mod_l = mod[l]
        if l % 2 == 0:
            e = l // 2
            lam_init = 0.8 - 0.6 * math.exp(-0.3 * l)
            segments = (
                (A_Q_W, True, 0, 0, None),
                (A_KV_W, True, 1, 0, 2),
                (A_KV_W, False, 1, A_KV_W, 3),
                (B_QK_W, True, 0, A_Q_W, None),
                (B_QK_W, True, 1, 2 * A_KV_W, 4),
                (B_V_W, False, 1, 2 * A_KV_W + B_QK_W, 5),
            )
            hg = _head_gain((
                (A_Q_W, a_q_norm[e], scale), (A_KV_W, a_k_norm[e], 1.0), (A_KV_W, None, 1.0),
                (B_QK_W, b_q_norm[e], scale), (B_QK_W, b_k_norm[e], 1.0), (B_V_W, None, 1.0)))
            outs = _lnproj(xa, xb, merged, mod_l, norm_mix[l], w_in_ab[e].astype(BF16), hg, _chunk_plan(segments),
                           A_Q_W + B_QK_W, 2 * A_KV_W + B_QK_W + B_V_W,
                           (A_KV_W, A_KV_W, B_QK_W, B_V_W), True)
            q, kv, ak, av, bk, bv = outs
            new_caches.append((
                ak.reshape(BATCH, SEQ, A_KV_HEADS, HEAD_DIM), av.reshape(BATCH, SEQ, A_KV_HEADS, HEAD_DIM),
                bk.reshape(BATCH, SEQ, B_HEADS, 2, HEAD_DIM), bv.reshape(BATCH, SEQ, B_HEADS, B_V_DIM)))
            lamv = jnp.stack([lam_q1[e], lam_k1[e], lam_q2[e], lam_k2[e]]).astype(F32)
            subln = b_subln[e].reshape(1, B_V_DIM)
            o_p = _attn_ab(q, kv, lamv, subln, lam_init, None)
            caches = (cache_a_k[:, e].reshape(DEC_BATCH, PAST_LEN, A_KV_W),
                      cache_a_v[:, e].reshape(DEC_BATCH, PAST_LEN, A_KV_W),
                      cache_b_k[:, e].reshape(DEC_BATCH, PAST_LEN, B_QK_W),
                      cache_b_v[:, e].reshape(DEC_BATCH, PAST_LEN, B_V_W))
            o_s = _attn_ab(q, kv, lamv, subln, lam_init, caches)
            w_out = w_out_ab[e].astype(BF16)
        else:
            oi = l // 2
            segments = (
                (C_WIDTH, True, 0, 0, None),
                (C_WIDTH, True, 1, 0, 2),
                (C_WIDTH, False, 1, C_WIDTH, 3),
            )
            hg = _head_gain(((C_WIDTH, c_q_norm[oi], scale), (C_WIDTH, c_k_norm[oi], 1.0), (C_WIDTH, None, 1.0)))
            outs = _lnproj(xa, xb, merged, mod_l, norm_mix[l], w_in_c[oi].astype(BF16), hg, _chunk_plan(segments),
                           C_WIDTH, 2 * C_WIDTH, (C_WIDTH, C_WIDTH), False)
            q, kv, ck_new, cv_new = outs
            new_caches.append((ck_new.reshape(BATCH, SEQ, C_HEADS, HEAD_DIM),
                               cv_new.reshape(BATCH, SEQ, C_HEADS, HEAD_DIM)))
            o_p = _attn_c_prompt(q, kv)
            o_s = _attn_na(q, kv,
                           cache_c_k[:, oi].reshape(DEC_BATCH, PAST_LEN, C_WIDTH),
                           cache_c_v[:, oi].reshape(DEC_BATCH, PAST_LEN, C_WIDTH),
                           _na_bias_table(c_rpb[oi]))
            w_out = w_out_c[oi].astype(BF16)

        x1, h2, gates, rank = _outproj(o_p, o_s, xa, xb, merged, w_out, mod_l, norm_ffn[l], w_router, b_router)
        dests, weights, tile_expert, n_tiles, last_tile = _route_plan(gates, rank)
        xs = _moe_scatter(dests, last_tile, h2)
        y = _moe_mlp(l, tile_expert, n_tiles, xs, w_gate, w_up, w_down)
        last = l == DEPTH - 1
        out = _moe_combine(dests, x1, weights, mod_l, y, last)
        if last:
            y_prompt, y_sample = out
        else:
            xa = xb = out
            merged = True

    even = [nc for i, nc in enumerate(new_caches) if i % 2 == 0]
    odd = [nc for i, nc in enumerate(new_caches) if i % 2 == 1]
    stack = lambda items, k: jnp.stack([it[k] for it in items], axis=1)
    return (y_prompt.reshape(BATCH, SEQ, D_MODEL), y_sample.reshape(DEC_BATCH, DEC_SEQ, D_MODEL),
            stack(even, 0), stack(even, 1), stack(even, 2), stack(even, 3),
            stack(odd, 0), stack(odd, 1))
```

```python
import functools
import math

import numpy as np
import jax
import jax.numpy as jnp
from jax import lax
from jax.experimental import pallas as pl
from jax.experimental.pallas import tpu as pltpu

F32 = jnp.float32
BF16 = jnp.bfloat16
HIGHEST = lax.Precision.HIGHEST

D_MODEL = 1024
BATCH = 16
SEQ = 256
DEPTH = 2
DEC_BATCH = 8
DEC_SEQ = 1024
PAST_LEN = 256
GRID_W = 64
HEAD_DIM = 64
ROPE_THETA = 10000.0
RMS_EPS = 1e-6
A_Q_HEADS = 8
A_KV_HEADS = 2
A_GROUP = A_Q_HEADS // A_KV_HEADS
B_HEADS = 4
B_V_DIM = 2 * HEAD_DIM
A_Q_W = A_Q_HEADS * HEAD_DIM
A_KV_W = A_KV_HEADS * HEAD_DIM
B_QK_W = B_HEADS * 2 * HEAD_DIM
B_V_W = B_HEADS * B_V_DIM
C_HEADS = 16
C_WIDTH = C_HEADS * HEAD_DIM
NA_ROWS = 8
NA_COLS = 16
N_EXPERTS = 16
N_GROUPS = 4
EXPERTS_PER_GROUP = N_EXPERTS // N_GROUPS
D_EXPERT = 1024

LANES = 128
SUBLANES = 8
ROW_TILES = D_MODEL // LANES
N_PROMPT = BATCH * SEQ
N_SAMPLE = DEC_BATCH * DEC_SEQ
N_TOK = N_PROMPT + N_SAMPLE
TM = 256
N_TILES = N_TOK // TM
N_PROMPT_TILES = N_PROMPT // TM
TM_LN = 512
COND_ROWS = 16
CTX_COND_ROW = DEC_BATCH
MOE_TM = 256
MOE_ROWS = 2 * N_TOK + N_EXPERTS * MOE_TM
MOE_TILES = MOE_ROWS // MOE_TM
GRID_ROWS = DEC_SEQ // GRID_W
NEG_BIG = -1e30
LOG2E = math.log2(math.e)
VMEM_LIMIT = 56 * 1024 * 1024


def _silu(x):
    return x * (1.0 / (1.0 + jnp.exp(-x)))


def _dot(a, b):
    return jnp.dot(a, b, preferred_element_type=F32)


def _dot_nt(a, b):
    return lax.dot_general(a, b, (((1,), (1,)), ((), ())), preferred_element_type=F32)


def _low_half(shape):
    return lax.broadcasted_iota(jnp.int32, shape, len(shape) - 1) < HEAD_DIM


def _swap_halves(x):
    return jnp.concatenate([x[:, HEAD_DIM:], x[:, :HEAD_DIM]], axis=1)


def _tile_cond(tm):
    tiles = np.arange(N_TOK // tm)
    npt = N_PROMPT // tm
    samp = np.maximum(tiles - npt, 0)
    per_seq = DEC_SEQ // tm
    cond = np.where(tiles < npt, CTX_COND_ROW, samp // per_seq)
    rope = np.where(tiles < npt, per_seq, samp % per_seq)
    return jnp.asarray(cond, jnp.int32), jnp.asarray(rope, jnp.int32)


def _split_specs(tm, merged):
    npt = N_PROMPT // tm
    base = npt if merged else 0
    a = pl.BlockSpec((tm, D_MODEL), lambda i, *_: (jnp.minimum(i, npt - 1), 0))
    b = pl.BlockSpec((tm, D_MODEL), lambda i, *_: (jnp.maximum(i - npt, 0) + base, 0))
    return a, b


def _mod_kernel(c_ref, w_ref, b_ref, o_ref):
    s = _silu(c_ref[...])
    o_ref[...] = jnp.dot(s, w_ref[...], precision=HIGHEST, preferred_element_type=F32) + b_ref[...]


def _modulation(cond, w_mod, b_mod):
    tn = 1536
    return pl.pallas_call(
        _mod_kernel,
        out_shape=jax.ShapeDtypeStruct((DEPTH, COND_ROWS, 6 * D_MODEL), F32),
        grid=(DEPTH, 6 * D_MODEL // tn),
        in_specs=[
            pl.BlockSpec((COND_ROWS, D_MODEL), lambda l, j: (0, 0)),
            pl.BlockSpec((None, D_MODEL, tn), lambda l, j: (l, 0, j)),
            pl.BlockSpec((None, 1, tn), lambda l, j: (l, 0, j)),
        ],
        out_specs=pl.BlockSpec((None, COND_ROWS, tn), lambda l, j: (l, 0, j)),
        compiler_params=pltpu.CompilerParams(
            dimension_semantics=("arbitrary", "arbitrary"), vmem_limit_bytes=VMEM_LIMIT),
        name="modulation",
    )(cond, w_mod, b_mod.reshape(DEPTH, 1, 6 * D_MODEL))


def _rope_tables(tm):
    pos = np.arange(DEC_SEQ)
    rows = (pos // GRID_W).astype(np.float64)
    cols = (pos % GRID_W).astype(np.float64)
    nfreq = HEAD_DIM // 4
    inv = ROPE_THETA ** (-np.arange(nfreq, dtype=np.float64) / nfreq)
    d = np.arange(HEAD_DIM)
    dd = d % (HEAD_DIM // 2)
    p = np.where((d >= HEAD_DIM // 2)[None, :], cols[:, None], rows[:, None])
    ang = p * inv[dd % nfreq][None, :]
    cos, sin = np.cos(ang), np.sin(ang)
    second = (dd >= nfreq)[None, :]
    sa = np.where(second, sin, 0.0)
    sb = np.where(second, 0.0, -sin)

    def full(t, ident):
        t = np.concatenate([t, np.full((tm, HEAD_DIM), ident)], axis=0)
        return jnp.asarray(np.tile(t, (1, LANES // HEAD_DIM)), dtype=F32)

    return full(cos, 1.0), full(sa, 0.0), full(sb, 0.0)


def _lnproj_kernel(tile_cond_ref, tile_rope_ref, xa_ref, xb_ref, mod_ref, g_ref, w_ref, hg_ref, gmat_ref,
                   cos_ref, sa_ref, sb_ref, *out_refs, chunks, use_rope):
    i = pl.program_id(0)
    npt = N_PROMPT // TM_LN
    row = tile_cond_ref[i]
    x = jnp.where(i < npt, xa_ref[...], xb_ref[...])
    ms = jnp.mean(x * x, axis=-1, keepdims=True)
    xn = x * lax.rsqrt(ms + RMS_EPS) * g_ref[...]
    shift = mod_ref[pl.ds(row, 1), 0:D_MODEL]
    scale = mod_ref[pl.ds(row, 1), D_MODEL:2 * D_MODEL]
    h = (xn * (1.0 + scale) + shift).astype(BF16)
    for piece in range(len(chunks) // 2):
        col0 = piece * 2 * LANES
        y2 = _dot(h, w_ref[:, col0:col0 + 2 * LANES])
        if chunks[2 * piece][0] or chunks[2 * piece + 1][0]:
            gs = _dot((y2 * y2).astype(BF16), gmat_ref[...])
            yn2 = y2 * lax.rsqrt(gs * (1.0 / HEAD_DIM) + RMS_EPS)
        for sub in range(2):
            c = piece * 2 + sub
            normed, dst, dst_col, f32_dst, f32_col = chunks[c]
            lanes = slice(sub * LANES, (sub + 1) * LANES)
            if normed:
                y = yn2[:, lanes] * hg_ref[:, c * LANES:(c + 1) * LANES]
            else:
                y = y2[:, lanes]
            if f32_dst is not None:
                out_refs[f32_dst][:, f32_col:f32_col + LANES] = y
            if normed and use_rope:
                y = (y * cos_ref[...] + pltpu.roll(y, HEAD_DIM // 4, 1) * sa_ref[...]
                     + pltpu.roll(y, LANES - HEAD_DIM // 4, 1) * sb_ref[...])
            out_refs[dst][:, dst_col:dst_col + LANES] = y.astype(BF16)


def _lnproj(xa, xb, merged, mod_l, gain, w_bf16, head_gain, chunks, q_w, kv_w, f32_widths, use_rope):
    dout = w_bf16.shape[1]
    tm = TM_LN
    npt = N_PROMPT // tm
    tile_cond, tile_rope = _tile_cond(tm)
    cos, sa, sb = _rope_tables(tm)
    gmat = jnp.asarray(np.kron(np.eye(2 * LANES // HEAD_DIM), np.ones((HEAD_DIM, HEAD_DIM))), dtype=BF16)
    const = lambda i, *_: (0, 0)
    tok = lambda i, *_: (i, 0)
    rope_map = lambda i, tc, tr: (tr[i], 0)
    prm = lambda i, *_: (jnp.minimum(i, npt), 0)
    xa_spec, xb_spec = _split_specs(tm, merged)
    out_shape = [jax.ShapeDtypeStruct((N_TOK, q_w), BF16), jax.ShapeDtypeStruct((N_TOK, kv_w), BF16)]
    out_specs = [pl.BlockSpec((tm, q_w), tok), pl.BlockSpec((tm, kv_w), tok)]
    for wd in f32_widths:
        out_shape.append(jax.ShapeDtypeStruct((N_PROMPT + tm, wd), F32))
        out_specs.append(pl.BlockSpec((tm, wd), prm))
    return pl.pallas_call(
        functools.partial(_lnproj_kernel, chunks=chunks, use_rope=use_rope),
        out_shape=out_shape,
        grid_spec=pltpu.PrefetchScalarGridSpec(
            num_scalar_prefetch=2,
            grid=(N_TOK // tm,),
            in_specs=[
                xa_spec, xb_spec,
                pl.BlockSpec((COND_ROWS, 6 * D_MODEL), const),
                pl.BlockSpec((1, D_MODEL), const),
                pl.BlockSpec((D_MODEL, dout), const),
                pl.BlockSpec((1, dout), const),
                pl.BlockSpec((2 * LANES, 2 * LANES), const),
                pl.BlockSpec((tm, LANES), rope_map),
                pl.BlockSpec((tm, LANES), rope_map),
                pl.BlockSpec((tm, LANES), rope_map),
            ],
            out_specs=out_specs,
        ),
        compiler_params=pltpu.CompilerParams(
            dimension_semantics=("arbitrary",), vmem_limit_bytes=VMEM_LIMIT),
        name="lnproj",
    )(tile_cond, tile_rope, xa, xb, mod_l, gain.reshape(1, D_MODEL), w_bf16, head_gain, gmat, cos, sa, sb)


def _softmax_parts(q, k_new, k_ctx, want_sum=True):
    s_n = _dot_nt(q, k_new)
    m = jnp.max(s_n, axis=-1, keepdims=True)
    if k_ctx is not None:
        s_c = _dot_nt(q, k_ctx)
        m = jnp.maximum(m, jnp.max(s_c, axis=-1, keepdims=True))
    p_n = jnp.exp2(s_n - m)
    l = jnp.sum(p_n, axis=-1, keepdims=True) if want_sum else None
    p_c = None
    if k_ctx is not None:
        p_c = jnp.exp2(s_c - m)
        if want_sum:
            l = l + jnp.sum(p_c, axis=-1, keepdims=True)
    return p_n, p_c, l


def _gqa_pairs(q_ref, q_col0, kv_ref, k_col0, v_col0, ck_ref, cv_ref, n_kv, group, tq):
    low = _low_half((tq, LANES))
    mxu_sums = group > 1
    head_out = [None] * (n_kv * group)
    for pair in range(n_kv // 2):
        lanes = slice(pair * LANES, (pair + 1) * LANES)
        k_n = kv_ref[:, k_col0 + lanes.start:k_col0 + lanes.stop]
        v_pair = kv_ref[:, v_col0 + lanes.start:v_col0 + lanes.stop]
        k_c = vc_pair = None
        if ck_ref is not None:
            k_c = ck_ref[:, lanes].astype(BF16)
            vc_pair = cv_ref[:, lanes].astype(BF16)
        for half in range(2):
            kvh = 2 * pair + half
            keep = low if half == 0 else jnp.logical_not(low)
            v_n, v_c = v_pair, vc_pair
            if mxu_sums:
                own = _low_half(v_pair.shape) if half == 0 else jnp.logical_not(_low_half(v_pair.shape))
                v_n = jnp.where(own, v_pair, jnp.ones_like(v_pair))
                if vc_pair is not None:
                    own_c = _low_half(vc_pair.shape) if half == 0 else jnp.logical_not(_low_half(vc_pair.shape))
                    v_c = jnp.where(own_c, vc_pair, jnp.ones_like(vc_pair))
            qs = []
            for g in range(group):
                head = kvh * group + g
                blk = q_ref[:, q_col0 + (head // 2) * LANES:q_col0 + (head // 2 + 1) * LANES]
                if head % 2 != half:
                    blk = _swap_halves(blk)
                qs.append(jnp.where(keep, blk, jnp.zeros_like(blk)))
            q = qs[0] if group == 1 else jnp.concatenate(qs, axis=0)
            p_n, p_c, l = _softmax_parts(q, k_n, k_c, want_sum=not mxu_sums)
            o = _dot(p_n.astype(BF16), v_n)
            if p_c is not None:
                o = o + _dot(p_c.astype(BF16), v_c)
            o = o * (1.0 / (pltpu.roll(o, HEAD_DIM, 1) if mxu_sums else l))
            for g in range(group):
                head = kvh * group + g
                og = o[g * tq:(g + 1) * tq]
                if head % 2 != half:
                    og = pltpu.roll(og, HEAD_DIM, 1)
                head_out[head] = og
    return [jnp.where(low, head_out[2 * k], head_out[2 * k + 1]) for k in range(n_kv * group // 2)]


def _attn_ab_kernel(*refs, has_cache, tq, lam_init):
    if has_cache:
        q_ref, kv_ref, cak_ref, cav_ref, cbk_ref, cbv_ref, lamv_ref, subln_ref, o_ref = refs
    else:
        q_ref, kv_ref, lamv_ref, subln_ref, o_ref = refs
        cak_ref = cav_ref = cbk_ref = cbv_ref = None
    outs = _gqa_pairs(q_ref, 0, kv_ref, 0, A_KV_W, cak_ref, cav_ref, A_KV_HEADS, A_GROUP, tq)

    lv = lamv_ref[...]
    l1 = jnp.sum(lv[0:1] * lv[1:2], axis=-1, keepdims=True)
    l2 = jnp.sum(lv[2:3] * lv[3:4], axis=-1, keepdims=True)
    lam = jnp.exp(l1) - jnp.exp(l2) + lam_init
    bk0 = 2 * A_KV_W
    bv0 = bk0 + B_QK_W
    low = _low_half((tq, LANES))
    for h in range(B_HEADS):
        lanes = slice(h * LANES, (h + 1) * LANES)
        qp = q_ref[:, A_Q_W + lanes.start:A_Q_W + lanes.stop]
        zero = jnp.zeros_like(qp)
        q = jnp.concatenate([jnp.where(low, qp, zero), jnp.where(low, zero, qp)], axis=0)
        k_n = kv_ref[:, bk0 + lanes.start:bk0 + lanes.stop]
        v_n = kv_ref[:, bv0 + lanes.start:bv0 + lanes.stop]
        k_c = v_c = None
        if has_cache:
            k_c = cbk_ref[:, lanes].astype(BF16)
            v_c = cbv_ref[:, lanes].astype(BF16)
        p_n, p_c, l = _softmax_parts(q, k_n, k_c)
        r = 1.0 / l
        r1 = r[:tq]
        r2 = lam * r[tq:]
        o = _dot((p_n[:tq] * r1 - p_n[tq:] * r2).astype(BF16), v_n)
        if has_cache:
            o = o + _dot((p_c[:tq] * r1 - p_c[tq:] * r2).astype(BF16), v_c)
        ms = jnp.mean(o * o, axis=-1, keepdims=True)
        o = o * lax.rsqrt(ms + RMS_EPS) * subln_ref[...] * (1.0 - lam_init)
        outs.append(o)
    o_ref[...] = jnp.concatenate(outs, axis=1).astype(BF16)


def _attn_ab(q, kv, lamv, subln, lam_init, caches):
    kv_w = kv.shape[1]
    kern = functools.partial(_attn_ab_kernel, lam_init=lam_init)
    const = lambda b, j: (0, 0)
    cp = pltpu.CompilerParams(dimension_semantics=("arbitrary", "arbitrary"), vmem_limit_bytes=VMEM_LIMIT)
    if caches is None:
        tq = SEQ
        return pl.pallas_call(
            functools.partial(kern, has_cache=False, tq=tq),
            out_shape=jax.ShapeDtypeStruct((N_PROMPT, D_MODEL), BF16),
            grid=(BATCH, 1),
            in_specs=[
                pl.BlockSpec((tq, D_MODEL), lambda b, j: (b, 0)),
                pl.BlockSpec((SEQ, kv_w), lambda b, j: (b, 0)),
                pl.BlockSpec((4, HEAD_DIM), const),
                pl.BlockSpec((1, B_V_DIM), const),
            ],
            out_specs=pl.BlockSpec((tq, D_MODEL), lambda b, j: (b, 0)),
            compiler_params=cp,
            name="attn_ab_prompt",
        )(q, kv, lamv, subln)
    tq = 256
    nq = DEC_SEQ // tq
    q0 = N_PROMPT // tq
    kv0 = N_PROMPT // DEC_SEQ
    cak, cav, cbk, cbv = caches
    cspec = lambda w: pl.BlockSpec((None, PAST_LEN, w), lambda b, j: (b, 0, 0))
    return pl.pallas_call(
        functools.partial(kern, has_cache=True, tq=tq),
        out_shape=jax.ShapeDtypeStruct((N_SAMPLE, D_MODEL), BF16),
        grid=(DEC_BATCH, nq),
        in_specs=[
            pl.BlockSpec((tq, D_MODEL), lambda b, j: (q0 + b * nq + j, 0)),
            pl.BlockSpec((DEC_SEQ, kv_w), lambda b, j: (kv0 + b, 0)),
            cspec(A_KV_W), cspec(A_KV_W), cspec(B_QK_W), cspec(B_V_W),
            pl.BlockSpec((4, HEAD_DIM), const),
            pl.BlockSpec((1, B_V_DIM), const),
        ],
        out_specs=pl.BlockSpec((tq, D_MODEL), lambda b, j: (b * nq + j, 0)),
        compiler_params=cp,
        name="attn_ab_sample",
    )(q, kv, cak, cav, cbk, cbv, lamv, subln)


def _attn_c_prompt_kernel(q_ref, kv_ref, o_ref):
    outs = _gqa_pairs(q_ref, 0, kv_ref, 0, C_WIDTH, None, None, C_HEADS, 1, SEQ)
    o_ref[...] = jnp.concatenate(outs, axis=1).astype(BF16)


def _attn_c_prompt(q, kv):
    return pl.pallas_call(
        _attn_c_prompt_kernel,
        out_shape=jax.ShapeDtypeStruct((N_PROMPT, D_MODEL), BF16),
        grid=(BATCH,),
        in_specs=[
            pl.BlockSpec((SEQ, C_WIDTH), lambda b: (b, 0)),
            pl.BlockSpec((SEQ, 2 * C_WIDTH), lambda b: (b, 0)),
        ],
        out_specs=pl.BlockSpec((SEQ, C_WIDTH), lambda b: (b, 0)),
        compiler_params=pltpu.CompilerParams(dimension_semantics=("arbitrary",), vmem_limit_bytes=VMEM_LIMIT),
        name="attn_c_prompt",
    )(q, kv)


def _na_kernel(q_ref, kv_ref, ck_ref, cv_ref, tp_ref, o_ref, ckb, cvb):
    r = pl.program_id(1)

    @pl.when(r == 0)
    def _():
        ckb[...] = ck_ref[...].astype(BF16)
        cvb[...] = cv_ref[...].astype(BF16)

    kh = min(NA_ROWS, GRID_ROWS)
    rs = jnp.clip(r - kh // 2, 0, GRID_ROWS - kh)
    ro0 = rs - r + (NA_ROWS - 1)
    start = pl.multiple_of(rs * GRID_W, GRID_W)
    win = kh * GRID_W
    low = _low_half((GRID_W, LANES))
    n_pair = C_HEADS // 2
    s_lat, s_ctx = [], []
    for j in range(n_pair):
        lanes = slice(j * LANES, (j + 1) * LANES)
        qp = q_ref[:, lanes]
        zero = jnp.zeros_like(qp)
        q = jnp.concatenate([jnp.where(low, qp, zero), jnp.where(low, zero, qp)], axis=0)
        kw = kv_ref[pl.ds(start, win), lanes]
        bias = jnp.concatenate(
            [jnp.concatenate([tp_ref[2 * j + hh, ro0 + 2 * t] for t in range(kh // 2)], axis=1)
             for hh in range(2)], axis=0)
        s_lat.append(_dot_nt(q, kw) + bias)
        s_ctx.append(_dot_nt(q, ckb[:, lanes]))
    outs = []
    for j in range(n_pair):
        lanes = slice(j * LANES, (j + 1) * LANES)
        s_l, s_c = s_lat[j], s_ctx[j]
        m = jnp.maximum(jnp.max(s_l, axis=-1, keepdims=True), jnp.max(s_c, axis=-1, keepdims=True))
        p_l = jnp.exp2(s_l - m)
        p_c = jnp.exp2(s_c - m)
        l = jnp.sum(p_l, axis=-1, keepdims=True) + jnp.sum(p_c, axis=-1, keepdims=True)
        vw = kv_ref[pl.ds(start, win), C_WIDTH + lanes.start:C_WIDTH + lanes.stop]
        o = _dot(p_c.astype(BF16), cvb[:, lanes]) + _dot(p_l.astype(BF16), vw)
        o = o * (1.0 / l)
        outs.append(jnp.where(low, o[:GRID_W], o[GRID_W:]))
    o_ref[...] = jnp.concatenate(outs, axis=1).astype(BF16)


NA_BIAS_PAD = GRID_W - NA_COLS


def _na_bias_kernel(w_ref, mask_ref, tp_ref):
    low = _low_half((GRID_W, LANES))
    keep = mask_ref[...] > 0.0
    n_off = 2 * NA_ROWS - 1
    left, right = [], []
    for ro in range(n_off):
        row = jnp.broadcast_to(w_ref[ro:ro + 1, :] * LOG2E, (GRID_W, LANES))
        left.append(pltpu.roll(row, LANES - GRID_W + 1, 1, stride=1, stride_axis=0))
        right.append(pltpu.roll(row, 1, 1, stride=1, stride_axis=0))
    for t in range(n_off - 1):
        tp_ref[t] = jnp.where(keep, jnp.where(low, left[t], right[t + 1]), NEG_BIG)


def _na_bias_table(rpb):
    cols = np.arange(GRID_W)
    col_start = np.clip(cols - NA_COLS // 2, 0, GRID_W - NA_COLS)
    col_in = (cols[None, :] >= col_start[:, None]) & (cols[None, :] < col_start[:, None] + NA_COLS)
    assert np.abs((cols[None, :] - cols[:, None])[col_in]).max() <= NA_COLS - 1
    mask = jnp.asarray(np.tile(col_in, (1, LANES // GRID_W)), dtype=F32)
    n_off = 2 * NA_ROWS - 1
    n_rel = 2 * NA_COLS - 1
    w = jnp.pad(rpb.astype(F32), ((0, 0), (0, 0), (NA_BIAS_PAD, LANES - NA_BIAS_PAD - n_rel)))
    return pl.pallas_call(
        _na_bias_kernel,
        out_shape=jax.ShapeDtypeStruct((C_HEADS, n_off - 1, GRID_W, LANES), F32),
        grid=(C_HEADS,),
        in_specs=[
            pl.BlockSpec((None, n_off, LANES), lambda h: (h, 0, 0)),
            pl.BlockSpec((GRID_W, LANES), lambda h: (0, 0)),
        ],
        out_specs=pl.BlockSpec((None, n_off - 1, GRID_W, LANES), lambda h: (h, 0, 0, 0)),
        compiler_params=pltpu.CompilerParams(dimension_semantics=("arbitrary",)),
        name="na_bias",
    )(w, mask)


def _attn_na(q, kv, ck, cv, tp):
    q0 = N_PROMPT // GRID_W
    kv0 = N_PROMPT // DEC_SEQ
    return pl.pallas_call(
        _na_kernel,
        out_shape=jax.ShapeDtypeStruct((N_SAMPLE, D_MODEL), BF16),
        grid=(DEC_BATCH, GRID_ROWS),
        in_specs=[
            pl.BlockSpec((GRID_W, C_WIDTH), lambda b, r: (q0 + b * GRID_ROWS + r, 0)),
            pl.BlockSpec((DEC_SEQ, 2 * C_WIDTH), lambda b, r: (kv0 + b, 0)),
            pl.BlockSpec((None, PAST_LEN, C_WIDTH), lambda b, r: (b, 0, 0)),
            pl.BlockSpec((None, PAST_LEN, C_WIDTH), lambda b, r: (b, 0, 0)),
            pl.BlockSpec(tp.shape, lambda b, r: (0, 0, 0, 0)),
        ],
        out_specs=pl.BlockSpec((GRID_W, C_WIDTH), lambda b, r: (b * GRID_ROWS + r, 0)),
        scratch_shapes=[pltpu.VMEM((PAST_LEN, C_WIDTH), BF16), pltpu.VMEM((PAST_LEN, C_WIDTH), BF16)],
        compiler_params=pltpu.CompilerParams(
            dimension_semantics=("arbitrary", "arbitrary"), vmem_limit_bytes=VMEM_LIMIT),
        name="attn_na_sample",
    )(q, kv, ck, cv, tp)


def _first_wins_ranks(vals):
    ranks = []
    for i in range(len(vals)):
        r = jnp.zeros_like(vals[i])
        for j in range(len(vals)):
            if j == i:
                continue
            beats = (vals[j] >= vals[i]) if j < i else (vals[j] > vals[i])
            r = r + jnp.where(beats, 1.0, 0.0)
        ranks.append(r)
    return ranks


def _outproj_kernel(tile_cond_ref, op_ref, os_ref, xa_ref, xb_ref, w_ref, mod_ref, g2_ref, wrt_ref, br_ref,
                    tri_ref, x1_ref, h2_ref, gates_ref, rank_ref, carry_ref):
    i = pl.program_id(0)
    row = tile_cond_ref[i]
    is_prompt = i < N_PROMPT_TILES

    @pl.when(i == 0)
    def _():
        carry_ref[...] = jnp.zeros_like(carry_ref)

    o = jnp.where(is_prompt, op_ref[...], os_ref[...])
    x = jnp.where(is_prompt, xa_ref[...], xb_ref[...])
    acc = _dot(o, w_ref[...])
    gate = mod_ref[pl.ds(row, 1), 2 * D_MODEL:3 * D_MODEL]
    x1 = x + gate * acc
    x1_ref[...] = x1
    ms = jnp.mean(x1 * x1, axis=-1, keepdims=True)
    xn = x1 * lax.rsqrt(ms + RMS_EPS) * g2_ref[...]
    shift = mod_ref[pl.ds(row, 1), 3 * D_MODEL:4 * D_MODEL]
    scale = mod_ref[pl.ds(row, 1), 4 * D_MODEL:5 * D_MODEL]
    h2 = xn * (1.0 + scale) + shift
    for s in range(ROW_TILES):
        h2_ref[pl.ds(s, TM, stride=ROW_TILES), :] = h2[:, s * LANES:(s + 1) * LANES]

    h_hi = h2.astype(BF16)
    h_lo = (h2 - h_hi.astype(F32)).astype(BF16)
    part = _dot_nt(wrt_ref[...], h_hi)
    logits = part[:N_EXPERTS] + part[N_EXPERTS:] + _dot_nt(wrt_ref[:N_EXPERTS, :], h_lo)
    e = jnp.exp(logits - jnp.max(logits, axis=0, keepdims=True))
    scores = e * (1.0 / jnp.sum(e, axis=0, keepdims=True))
    sel = scores + br_ref[...]
    sel_rows = [sel[k:k + 1, :] for k in range(N_EXPERTS)]
    in_top2 = []
    group_sum = []
    for g in range(N_GROUPS):
        vals = sel_rows[g * EXPERTS_PER_GROUP:(g + 1) * EXPERTS_PER_GROUP]
        ranks = _first_wins_ranks(vals)
        top = [rk < 2.0 for rk in ranks]
        in_top2.extend(top)
        s = jnp.zeros_like(vals[0])
        for v, t in zip(vals, top):
            s = s + jnp.where(t, v, 0.0)
        group_sum.append(s)
    group_rank = _first_wins_ranks(group_sum)
    mask_rows = []
    for k in range(N_EXPERTS):
        chosen = jnp.where(in_top2[k], 1.0, 0.0) * jnp.where(group_rank[k // EXPERTS_PER_GROUP] < 1.0, 1.0, 0.0)
        mask_rows.append(chosen)
    mask = jnp.concatenate(mask_rows, axis=0)
    picked = scores * mask
    gates_ref[...] = picked * (1.0 / jnp.sum(picked, axis=0, keepdims=True))
    prefix = _dot(mask.astype(BF16), tri_ref[...])
    rank_ref[...] = jnp.where(mask > 0.0, prefix + carry_ref[...], -1.0)
    carry_ref[...] = carry_ref[...] + jnp.sum(mask, axis=1, keepdims=True)


def _outproj(o_prompt, o_sample, xa, xb, merged, w_bf16, mod_l, gain2, w_router, b_router):
    const = lambda i, *_: (0, 0)
    tok = lambda i, *_: (i, 0)
    tokT = lambda i, *_: (0, i)
    tile_cond, _ = _tile_cond(TM)
    op_spec, os_spec = _split_specs(TM, False)
    xa_spec, xb_spec = _split_specs(TM, merged)
    tri = jnp.asarray(np.triu(np.ones((TM, TM)), k=1), dtype=BF16)
    wrt = w_router.T.astype(F32)
    wrt_hi = wrt.astype(BF16)
    wrt_split = jnp.concatenate([wrt_hi, (wrt - wrt_hi.astype(F32)).astype(BF16)], axis=0)
    return pl.pallas_call(
        _outproj_kernel,
        out_shape=[
            jax.ShapeDtypeStruct((N_TOK, D_MODEL), F32),
            jax.ShapeDtypeStruct((N_TOK * ROW_TILES, LANES), F32),
            jax.ShapeDtypeStruct((N_EXPERTS, N_TOK), F32),
            jax.ShapeDtypeStruct((N_EXPERTS, N_TOK), F32),
        ],
        grid_spec=pltpu.PrefetchScalarGridSpec(
            num_scalar_prefetch=1,
            grid=(N_TILES,),
            in_specs=[
                op_spec, os_spec, xa_spec, xb_spec,
                pl.BlockSpec((D_MODEL, D_MODEL), const),
                pl.BlockSpec((COND_ROWS, 6 * D_MODEL), const),
                pl.BlockSpec((1, D_MODEL), const),
                pl.BlockSpec((2 * N_EXPERTS, D_MODEL), const),
                pl.BlockSpec((N_EXPERTS, 1), const),
                pl.BlockSpec((TM, TM), const),
            ],
            out_specs=[
                pl.BlockSpec((TM, D_MODEL), tok),
                pl.BlockSpec((TM * ROW_TILES, LANES), tok),
                pl.BlockSpec((N_EXPERTS, TM), tokT),
                pl.BlockSpec((N_EXPERTS, TM), tokT),
            ],
            scratch_shapes=[pltpu.VMEM((N_EXPERTS, 1), F32)],
        ),
        compiler_params=pltpu.CompilerParams(
            dimension_semantics=("arbitrary",), vmem_limit_bytes=VMEM_LIMIT),
        name="outproj_router",
    )(tile_cond, o_prompt, o_sample, xa, xb, w_bf16, mod_l, gain2.reshape(1, D_MODEL), wrt_split,
      b_router.reshape(N_EXPERTS, 1), tri)


def _route_plan(gates, rank):
    sel = rank >= 0.0
    counts = jnp.sum(sel, axis=1).astype(jnp.int32)
    padded = ((counts + MOE_TM - 1) // MOE_TM) * MOE_TM
    ends = jnp.cumsum(padded).astype(jnp.int32)
    offs = ends - padded
    dest = jnp.where(sel, offs[:, None] + rank.astype(jnp.int32), -1)
    d1 = jnp.max(dest, axis=0)
    d0 = jnp.min(jnp.where(sel, dest, MOE_ROWS), axis=0)
    w0 = jnp.sum(jnp.where(dest == d0[None, :], gates, 0.0), axis=0)
    w1 = jnp.sum(jnp.where(dest == d1[None, :], gates, 0.0), axis=0)
    n_tiles = (ends[-1] // MOE_TM).astype(jnp.int32).reshape(1)
    tile_start = jnp.arange(MOE_TILES, dtype=jnp.int32) * MOE_TM
    tile_expert = jnp.sum(tile_start[:, None] >= ends[None, :], axis=1).astype(jnp.int32)
    tile_expert = jnp.minimum(tile_expert, N_EXPERTS - 1)
    after = jnp.minimum(ends[tile_expert] // MOE_TM, n_tiles[0] - 1)
    next_expert = tile_expert[after]
    dests =(jnp.concatenate([d0, d1]) * ROW_TILES).astype(jnp.int32)
    weights = jnp.stack([w0, w1], axis=1)
    last_tile = jnp.where(padded > 0, ends - MOE_TM, -1)
    spare = n_tiles[0] + jnp.arange(N_EXPERTS, dtype=jnp.int32)
    spare = jnp.where(spare < MOE_TILES, spare * MOE_TM, -1)
    clear = (jnp.concatenate([last_tile, spare]) * ROW_TILES).astype(jnp.int32)
    return dests, weights, tile_expert, n_tiles, next_expert, clear


def _scatter_kernel(d_ref, last_ref, h_ref, xs_ref, zeros, sem, zsem):
    i = pl.program_id(0)
    base = i * TM
    rows = MOE_TM * ROW_TILES

    @pl.when(i == 0)
    def _():
        zeros[...] = jnp.zeros_like(zeros)
        for e in range(2 * N_EXPERTS):
            @pl.when(last_ref[e] >= 0)
            def _():
                at = pl.multiple_of(last_ref[e], SUBLANES)
                pltpu.make_async_copy(zeros, xs_ref.at[pl.ds(at, rows)], zsem).start()
        for e in range(2 * N_EXPERTS):
            @pl.when(last_ref[e] >= 0)
            def _():
                pltpu.make_async_copy(zeros, xs_ref.at[pl.ds(0, rows)], zsem).wait()

    def copy(r, t):
        src = h_ref.at[pl.ds(pl.multiple_of(r * ROW_TILES, SUBLANES), ROW_TILES)]
        dst = xs_ref.at[pl.ds(pl.multiple_of(t, SUBLANES), ROW_TILES)]
        return pltpu.make_async_copy(src, dst, sem)

    def issue(r, c):
        copy(r, d_ref[base + r]).start(priority=0)
        copy(r, d_ref[N_TOK + base + r]).start(priority=1)
        return c

    lax.fori_loop(0, TM, issue, 0, unroll=8)

    def drain(r, c):
        copy(r, 0).wait()
        copy(r, 0).wait()
        return c

    lax.fori_loop(0, TM, drain, 0, unroll=8)


def _moe_scatter(dests, last_tile, h2):
    return pl.pallas_call(
        _scatter_kernel,
        out_shape=jax.ShapeDtypeStruct((MOE_ROWS * ROW_TILES, LANES), F32),
        grid_spec=pltpu.PrefetchScalarGridSpec(
            num_scalar_prefetch=2,
            grid=(N_TILES,),
            in_specs=[pl.BlockSpec((TM * ROW_TILES, LANES), lambda i, *_: (i, 0))],
            out_specs=pl.BlockSpec(memory_space=pl.ANY),
            scratch_shapes=[
                pltpu.VMEM((MOE_TM * ROW_TILES, LANES), F32),
                pltpu.SemaphoreType.DMA,
                pltpu.SemaphoreType.DMA,
            ],
        ),
        compiler_params=pltpu.CompilerParams(
            dimension_semantics=("arbitrary",), vmem_limit_bytes=VMEM_LIMIT, has_side_effects=True),
        name="moe_scatter",
    )(dests, last_tile, h2)


def _mlp_kernel(te_ref, nv_ref, nxt_ref, xs_ref, wg_hbm, wu_hbm, wd_hbm, y_ref, stage, wgb, wub, wdb, hb, sem,
                *, layer):
    t = pl.program_id(0)

    def fetch(e):
        return [pltpu.make_async_copy(w.at[layer, e], stage.at[k], sem.at[k])
                for k, w in enumerate((wg_hbm, wu_hbm, wd_hbm))]

    @pl.when(t < nv_ref[0])
    def _():
        e = te_ref[t]
        prev = te_ref[jnp.maximum(t - 1, 0)]

        @pl.when(t == 0)
        def _():
            for cp in fetch(e):
                cp.start()

        @pl.when((t == 0) | (e != prev))
        def _():
            for cp in fetch(e):
                cp.wait()
            wgb[...] = stage[0].astype(BF16)
            wub[...] = stage[1].astype(BF16)
            wdb[...] = stage[2].astype(BF16)

            @pl.when(nxt_ref[t] != e)
            def _():
                for cp in fetch(nxt_ref[t]):
                    cp.start()

        x = jnp.concatenate(
            [xs_ref[pl.ds(s, MOE_TM, stride=ROW_TILES), :] for s in range(ROW_TILES)], axis=1).astype(BF16)
        step = 512
        for c in range(0, D_EXPERT, step):
            g = _dot(x, wgb[:, c:c + step])
            u = _dot(x, wub[:, c:c + step])
            hb[:, c:c + step] = (_silu(g) * u).astype(BF16)
        y = _dot(hb[...], wdb[...])
        for s in range(ROW_TILES):
            y_ref[pl.ds(s, MOE_TM, stride=ROW_TILES), :] = y[:, s * LANES:(s + 1) * LANES]

    @pl.when(t >= nv_ref[0])
    def _():
        y_ref[...] = jnp.zeros_like(y_ref)


def _moe_mlp(layer, tile_expert, n_tiles, next_expert, xs, w_gate, w_up, w_down):
    assert D_MODEL == D_EXPERT

    def tile_map(t, te, nv, nx):
        return (jnp.minimum(t, nv[0] - 1), 0)

    def out_map(t, te, nv, nx):
        return (t, 0)

    hbm = pl.BlockSpec(memory_space=pl.ANY)
    return pl.pallas_call(
        functools.partial(_mlp_kernel, layer=layer),
        out_shape=jax.ShapeDtypeStruct((MOE_ROWS * ROW_TILES, LANES), F32),
        grid_spec=pltpu.PrefetchScalarGridSpec(
            num_scalar_prefetch=3,
            grid=(MOE_TILES,),
            in_specs=[pl.BlockSpec((MOE_TM * ROW_TILES, LANES), tile_map), hbm, hbm, hbm],
            out_specs=pl.BlockSpec((MOE_TM * ROW_TILES, LANES), out_map),
            scratch_shapes=[
                pltpu.VMEM((3, D_MODEL, D_EXPERT), F32),
                pltpu.VMEM((D_MODEL, D_EXPERT), BF16),
                pltpu.VMEM((D_MODEL, D_EXPERT), BF16),
                pltpu.VMEM((D_EXPERT, D_MODEL), BF16),
                pltpu.VMEM((MOE_TM, D_EXPERT), BF16),
                pltpu.SemaphoreType.DMA((3,)),
            ],
        ),
        compiler_params=pltpu.CompilerParams(
            dimension_semantics=("arbitrary",), vmem_limit_bytes=VMEM_LIMIT),
        name="moe_mlp",
    )(tile_expert, n_tiles, next_expert, xs, w_gate, w_up, w_down)


def _combine_kernel(d_ref, tile_cond_ref, x1_ref, w_ref, mod_ref, y_ref, *rest, split_out):
    if split_out:
        outp_ref, outs_ref, buf, sem = rest
    else:
        out_ref, buf, sem = rest
    i = pl.program_id(0)
    row = tile_cond_ref[i]
    slot = i % 2

    def copy(sl, k, r, t):
        src = y_ref.at[pl.ds(pl.multiple_of(t, SUBLANES), ROW_TILES)]
        dst = buf.at[sl, k, pl.ds(pl.multiple_of(r * ROW_TILES, SUBLANES), ROW_TILES)]
        return pltpu.make_async_copy(src, dst, sem.at[sl])

    def gather_tile(tile, sl):
        base = tile * TM

        def issue(r, c):
            copy(sl, 0, r, d_ref[base + r]).start(priority=0)
            copy(sl, 1, r, d_ref[N_TOK + base + r]).start(priority=1)
            return c

        lax.fori_loop(0, TM, issue, 0, unroll=8)

    @pl.when(i == 0)
    def _():
        gather_tile(0, 0)

    @pl.when(i + 1 < N_TILES)
    def _():
        gather_tile(i + 1, 1 - slot)

    def drain(r, c):
        copy(slot, 0, r, 0).wait()
        copy(slot, 1, r, 0).wait()
        return c

    lax.fori_loop(0, TM, drain, 0, unroll=8)
    w = w_ref[...]
    w0 = w[:, 0:1]
    w1 = w[:, 1:2]
    parts = []
    for s in range(ROW_TILES):
        y0 = buf[slot, 0, pl.ds(s, TM, stride=ROW_TILES), :]
        y1 = buf[slot, 1, pl.ds(s, TM, stride=ROW_TILES), :]
        parts.append(w0 * y0 + w1 * y1)
    gate = mod_ref[pl.ds(row, 1), 5 * D_MODEL:6 * D_MODEL]
    out = x1_ref[...] + gate * jnp.concatenate(parts, axis=1)
    if split_out:
        @pl.when(i < N_PROMPT_TILES)
        def _():
            outp_ref[...] = out

        @pl.when(i >= N_PROMPT_TILES)
        def _():
            outs_ref[...] = out
    else:
        out_ref[...] = out


def _moe_combine(dests, x1, weights, mod_l, y, split_out):
    tile_cond, _ = _tile_cond(TM)
    if split_out:
        out_shape = [jax.ShapeDtypeStruct((N_PROMPT, D_MODEL), F32), jax.ShapeDtypeStruct((N_SAMPLE, D_MODEL), F32)]
        out_specs = list(_split_specs(TM, False))
    else:
        out_shape = jax.ShapeDtypeStruct((N_TOK, D_MODEL), F32)
        out_specs = pl.BlockSpec((TM, D_MODEL), lambda i, *_: (i, 0))
    return pl.pallas_call(
        functools.partial(_combine_kernel, split_out=split_out),
        out_shape=out_shape,
        grid_spec=pltpu.PrefetchScalarGridSpec(
            num_scalar_prefetch=2,
            grid=(N_TILES,),
            in_specs=[
                pl.BlockSpec((TM, D_MODEL), lambda i, *_: (i, 0)),
                pl.BlockSpec((TM, 2), lambda i, *_: (i, 0)),
                pl.BlockSpec((COND_ROWS, 6 * D_MODEL), lambda i, *_: (0, 0)),
                pl.BlockSpec(memory_space=pl.ANY),
            ],
            out_specs=out_specs,
            scratch_shapes=[pltpu.VMEM((2, 2, TM * ROW_TILES, LANES), F32), pltpu.SemaphoreType.DMA((2,))],
        ),
        compiler_params=pltpu.CompilerParams(
            dimension_semantics=("arbitrary",), vmem_limit_bytes=VMEM_LIMIT),
        name="moe_combine",
    )(dests, tile_cond, x1, weights, mod_l, y)


def _chunk_plan(segments):
    chunks = []
    for width, normed, dst, dst_col0, f32_dst in segments:
        for k in range(width // LANES):
            chunks.append((normed, dst, dst_col0 + k * LANES, f32_dst, k * LANES))
    return tuple(chunks)


def _head_gain(parts):
    cols = []
    for width, g, mult in parts:
        if g is None:
            cols.append(jnp.ones((width,), F32))
        else:
            cols.append(jnp.tile(g.astype(F32) * mult, width // HEAD_DIM))
    return jnp.concatenate(cols).reshape(1, -1)


def kernel(x_prompt, x_sample, cache_a_k, cache_a_v, cache_b_k, cache_b_v, cache_c_k, cache_c_v, c, c_ctx, w_mod, b_mod, norm_mix, norm_ffn, w_in_ab, w_out_ab, a_q_norm, a_k_norm, b_q_norm, b_k_norm, lam_q1, lam_k1, lam_q2, lam_k2, b_subln, w_in_c, w_out_c, c_q_norm, c_k_norm, c_rpb, w_router, b_router, w_gate, w_up, w_down):
    scale = LOG2E * HEAD_DIM ** -0.5
    cond = jnp.concatenate(
        [c, c_ctx[None, :], jnp.zeros((COND_ROWS - DEC_BATCH - 1, D_MODEL), F32)], axis=0)
    mod = _modulation(cond, w_mod, b_mod)

    xa = x_prompt.reshape(N_PROMPT, D_MODEL)
    xb = x_sample.reshape(N_SAMPLE, D_MODEL)
    merged = False
    new_caches = []
    for l in range(DEPTH):
        mod_l = mod[l]
        if l % 2 == 0:
            e = l // 2
            lam_init = 0.8 - 0.6 * math.exp(-0.3 * l)
            segments = (
                (A_Q_W, True, 0, 0, None),
                (A_KV_W, True, 1, 0, 2),
                (A_KV_W, False, 1, A_KV_W, 3),
                (B_QK_W, True, 0, A_Q_W, None),
                (B_QK_W, True, 1, 2 * A_KV_W, 4),
                (B_V_W, False, 1, 2 * A_KV_W + B_QK_W, 5),
            )
            hg = _head_gain((
                (A_Q_W, a_q_norm[e], scale), (A_KV_W, a_k_norm[e], 1.0), (A_KV_W, None, 1.0),
                (B_QK_W, b_q_norm[e], scale), (B_QK_W, b_k_norm[e], 1.0), (B_V_W, None, 1.0)))
            outs = _lnproj(xa, xb, merged, mod_l, norm_mix[l], w_in_ab[e].astype(BF16), hg, _chunk_plan(segments),
                           A_Q_W + B_QK_W, 2 * A_KV_W + B_QK_W + B_V_W,
                           (A_KV_W, A_KV_W, B_QK_W, B_V_W), True)
            q, kv = outs[:2]
            ak, av, bk, bv = (t[:N_PROMPT] for t in outs[2:])
            new_caches.append((
                ak.reshape(BATCH, SEQ, A_KV_HEADS, HEAD_DIM), av.reshape(BATCH, SEQ, A_KV_HEADS, HEAD_DIM),
                bk.reshape(BATCH, SEQ, B_HEADS, 2, HEAD_DIM), bv.reshape(BATCH, SEQ, B_HEADS, B_V_DIM)))
            lamv = jnp.stack([lam_q1[e], lam_k1[e], lam_q2[e], lam_k2[e]]).astype(F32)
            subln = b_subln[e].reshape(1, B_V_DIM)
            o_p = _attn_ab(q, kv, lamv, subln, lam_init, None)
            caches = (cache_a_k[:, e].reshape(DEC_BATCH, PAST_LEN, A_KV_W),
                      cache_a_v[:, e].reshape(DEC_BATCH, PAST_LEN, A_KV_W),
                      cache_b_k[:, e].reshape(DEC_BATCH, PAST_LEN, B_QK_W),
                      cache_b_v[:, e].reshape(DEC_BATCH, PAST_LEN, B_V_W))
            o_s = _attn_ab(q, kv, lamv, subln, lam_init, caches)
            w_out = w_out_ab[e].astype(BF16)
        else:
            oi = l // 2
            segments = (
                (C_WIDTH, True, 0, 0, None),
                (C_WIDTH, True, 1, 0, 2),
                (C_WIDTH, False, 1, C_WIDTH, 3),
            )
            hg = _head_gain(((C_WIDTH, c_q_norm[oi], scale), (C_WIDTH, c_k_norm[oi], 1.0), (C_WIDTH, None, 1.0)))
            outs = _lnproj(xa, xb, merged, mod_l, norm_mix[l], w_in_c[oi].astype(BF16), hg, _chunk_plan(segments),
                           C_WIDTH, 2 * C_WIDTH, (C_WIDTH, C_WIDTH), False)
            q, kv = outs[:2]
            ck_new, cv_new = (t[:N_PROMPT] for t in outs[2:])
            new_caches.append((ck_new.reshape(BATCH, SEQ, C_HEADS, HEAD_DIM),
                               cv_new.reshape(BATCH, SEQ, C_HEADS, HEAD_DIM)))
            o_p = _attn_c_prompt(q, kv)
            o_s = _attn_na(q, kv,
                           cache_c_k[:, oi].reshape(DEC_BATCH, PAST_LEN, C_WIDTH),
                           cache_c_v[:, oi].reshape(DEC_BATCH, PAST_LEN, C_WIDTH),
                           _na_bias_table(c_rpb[oi]))
            w_out = w_out_c[oi].astype(BF16)

        x1, h2, gates, rank = _outproj(o_p, o_s, xa, xb, merged, w_out, mod_l, norm_ffn[l], w_router, b_router)
        dests, weights, tile_expert, n_tiles, next_expert, clear = _route_plan(gates, rank)
        xs = _moe_scatter(dests, clear, h2)
        y = _moe_mlp(l, tile_expert, n_tiles, next_expert, xs, w_gate, w_up, w_down)
        last = l == DEPTH - 1
        out = _moe_combine(dests, x1, weights, mod_l, y, last)
        if last:
            y_prompt, y_sample = out
        else:
            xa = xb = out
            merged = True

    even = [nc for i, nc in enumerate(new_caches) if i % 2 == 0]
    odd = [nc for i, nc in enumerate(new_caches) if i % 2 == 1]
    stack = lambda items, k: jnp.stack([it[k] for it in items], axis=1)
    return (y_prompt.reshape(BATCH, SEQ, D_MODEL), y_sample.reshape(DEC_BATCH, DEC_SEQ, D_MODEL),
            stack(even, 0), stack(even, 1), stack(even, 2), stack(even, 3),
            stack(odd, 0), stack(odd, 1))
```

```python
import functools
import math

import numpy as np
import jax
import jax.numpy as jnp
from jax import lax
from jax.experimental import pallas as pl
from jax.experimental.pallas import tpu as pltpu

F32 = jnp.float32
BF16 = jnp.bfloat16

D_MODEL = 1024
BATCH = 16
SEQ = 256
DEPTH = 2
DEC_BATCH = 8
DEC_SEQ = 1024
PAST_LEN = 256
GRID_W = 64
HEAD_DIM = 64
ROPE_THETA = 10000.0
RMS_EPS = 1e-6
A_Q_HEADS = 8
A_KV_HEADS = 2
A_GROUP = A_Q_HEADS // A_KV_HEADS
B_HEADS = 4
B_V_DIM = 2 * HEAD_DIM
A_Q_W = A_Q_HEADS * HEAD_DIM
A_KV_W = A_KV_HEADS * HEAD_DIM
B_QK_W = B_HEADS * 2 * HEAD_DIM
B_V_W = B_HEADS * B_V_DIM
C_HEADS = 16
C_WIDTH = C_HEADS * HEAD_DIM
NA_ROWS = 8
NA_COLS = 16
N_EXPERTS = 16
N_GROUPS = 4
EXPERTS_PER_GROUP = N_EXPERTS // N_GROUPS
D_EXPERT = 1024

LANES = 128
SUBLANES = 8
ROW_TILES = D_MODEL // LANES
N_PROMPT = BATCH * SEQ
N_SAMPLE = DEC_BATCH * DEC_SEQ
N_TOK = N_PROMPT + N_SAMPLE
TM = 256
N_TILES = N_TOK // TM
N_PROMPT_TILES = N_PROMPT // TM
TM_LN = 512
TM_OUT = 512
COND_ROWS = 16
CTX_COND_ROW = DEC_BATCH
MOE_TM = 256
MOE_ROWS = 2 * N_TOK + N_EXPERTS * MOE_TM
MOE_TILES = MOE_ROWS // MOE_TM
GRID_ROWS = DEC_SEQ // GRID_W
NA_ROWS_PER_STEP = 4
NEG_BIG = -1e30
LOG2E = math.log2(math.e)
VMEM_LIMIT = 56 * 1024 * 1024


def _silu(x):
    return x * (1.0 / (1.0 + jnp.exp(-x)))


def _dot(a, b):
    return jnp.dot(a, b, preferred_element_type=F32)


def _dot_nt(a, b):
    return lax.dot_general(a, b, (((1,), (1,)), ((), ())), preferred_element_type=F32)


def _low_half(shape):
    return lax.broadcasted_iota(jnp.int32, shape, len(shape) - 1) < HEAD_DIM


def _swap_halves(x):
    return jnp.concatenate([x[:, HEAD_DIM:], x[:, :HEAD_DIM]], axis=1)


def _tile_cond(tm):
    tiles = np.arange(N_TOK // tm)
    npt = N_PROMPT // tm
    samp = np.maximum(tiles - npt, 0)
    per_seq = DEC_SEQ // tm
    cond = np.where(tiles < npt, CTX_COND_ROW, samp // per_seq)
    rope = np.where(tiles < npt, per_seq, samp % per_seq)
    return jnp.asarray(cond, jnp.int32), jnp.asarray(rope, jnp.int32)


def _split_specs(tm, merged):
    npt = N_PROMPT // tm
    base = npt if merged else 0
    a = pl.BlockSpec((tm, D_MODEL), lambda i, *_: (jnp.minimum(i, npt - 1), 0))
    b = pl.BlockSpec((tm, D_MODEL), lambda i, *_: (jnp.maximum(i - npt, 0) + base, 0))
    return a, b


def _mod_kernel(c_ref, w_ref, b_ref, o_ref):
    s = _silu(c_ref[...])
    s_hi = s.astype(BF16)
    s_lo = (s - s_hi.astype(F32)).astype(BF16)
    w = w_ref[...].astype(BF16)
    o_ref[...] = _dot(s_hi, w) + _dot(s_lo, w) + b_ref[...]


def _modulation(cond, w_mod, b_mod):
    tn = 1536
    return pl.pallas_call(
        _mod_kernel,
        out_shape=jax.ShapeDtypeStruct((DEPTH, COND_ROWS, 6 * D_MODEL), F32),
        grid=(DEPTH, 6 * D_MODEL // tn),
        in_specs=[
            pl.BlockSpec((COND_ROWS, D_MODEL), lambda l, j: (0, 0)),
            pl.BlockSpec((None, D_MODEL, tn), lambda l, j: (l, 0, j)),
            pl.BlockSpec((None, 1, tn), lambda l, j: (l, 0, j)),
        ],
        out_specs=pl.BlockSpec((None, COND_ROWS, tn), lambda l, j: (l, 0, j)),
        compiler_params=pltpu.CompilerParams(
            dimension_semantics=("arbitrary", "arbitrary"), vmem_limit_bytes=VMEM_LIMIT),
        name="modulation",
    )(cond, w_mod, b_mod.reshape(DEPTH, 1, 6 * D_MODEL))


def _rope_tables(tm):
    pos = np.arange(DEC_SEQ)
    rows = (pos // GRID_W).astype(np.float64)
    cols = (pos % GRID_W).astype(np.float64)
    nfreq = HEAD_DIM // 4
    inv = ROPE_THETA ** (-np.arange(nfreq, dtype=np.float64) / nfreq)
    d = np.arange(HEAD_DIM)
    dd = d % (HEAD_DIM // 2)
    p = np.where((d >= HEAD_DIM // 2)[None, :], cols[:, None], rows[:, None])
    ang = p * inv[dd % nfreq][None, :]
    cos, sin = np.cos(ang), np.sin(ang)
    second = (dd >= nfreq)[None, :]
    sa = np.where(second, sin, 0.0)
    sb = np.where(second, 0.0, -sin)

    def full(t, ident):
        t = np.concatenate([t, np.full((tm, HEAD_DIM), ident)], axis=0)
        return jnp.asarray(np.tile(t, (1, LANES // HEAD_DIM)), dtype=F32)

    return full(cos, 1.0), full(sa, 0.0), full(sb, 0.0)


def _lnproj_kernel(tile_cond_ref, tile_rope_ref, xa_ref, xb_ref, mod_ref, g_ref, w_ref, hg_ref, gmat_ref,
                   cos_ref, sa_ref, sb_ref, *out_refs, chunks, use_rope):
    i = pl.program_id(0)
    row = tile_cond_ref[i]
    x = jnp.where(i >= N_SAMPLE // TM_LN, xa_ref[...], xb_ref[...])
    ms = jnp.mean(x * x, axis=-1, keepdims=True)
    xn = x * lax.rsqrt(ms + RMS_EPS) * g_ref[...]
    shift = mod_ref[pl.ds(row, 1), 0:D_MODEL]
    scale = mod_ref[pl.ds(row, 1), D_MODEL:2 * D_MODEL]
    h = (xn * (1.0 + scale) + shift).astype(BF16)
    for piece in range(len(chunks) // 2):
        col0 = piece * 2 * LANES
        y2 = _dot(h, w_ref[:, col0:col0 + 2 * LANES])
        if chunks[2 * piece][0] or chunks[2 * piece + 1][0]:
            gs = _dot((y2 * y2).astype(BF16), gmat_ref[...])
            yn2 = y2 * lax.rsqrt(gs * (1.0 / HEAD_DIM) + RMS_EPS)
        for sub in range(2):
            c = piece * 2 + sub
            normed, dst, dst_col, f32_dst, f32_col = chunks[c]
            lanes = slice(sub * LANES, (sub + 1) * LANES)
            if normed:
                y = yn2[:, lanes] * hg_ref[:, c * LANES:(c + 1) * LANES]
            else:
                y = y2[:, lanes]
            if f32_dst is not None:
                out_refs[f32_dst][:, f32_col:f32_col + LANES] = y
            if normed and use_rope:
                y = (y * cos_ref[...] + pltpu.roll(y, HEAD_DIM // 4, 1) * sa_ref[...]
                     + pltpu.roll(y, LANES - HEAD_DIM // 4, 1) * sb_ref[...])
            out_refs[dst][:, dst_col:dst_col + LANES] = y.astype(BF16)


def _lnproj(xa, xb, merged, mod_l, gain, w_bf16, head_gain, chunks, q_w, kv_w, f32_widths, use_rope):
    dout = w_bf16.shape[1]
    tm = TM_LN
    npt = N_PROMPT // tm
    nst = N_SAMPLE // tm
    cond_tok, rope_tok = _tile_cond(tm)
    order = np.concatenate([np.arange(npt, npt + nst), np.arange(npt)])
    tile_cond, tile_rope = cond_tok[order], rope_tok[order]
    cos, sa, sb = _rope_tables(tm)
    gmat = jnp.asarray(np.kron(np.eye(2 * LANES // HEAD_DIM), np.ones((HEAD_DIM, HEAD_DIM))), dtype=BF16)
    const = lambda i, *_: (0, 0)
    tok = lambda i, *_: (jnp.where(i < nst, i + npt, i - nst), 0)
    rope_map = lambda i, tc, tr: (tr[i], 0)
    prm = lambda i, *_: (jnp.maximum(i - nst, 0), 0)
    base = npt if merged else 0
    xa_spec = pl.BlockSpec((tm, D_MODEL), prm)
    xb_spec = pl.BlockSpec((tm, D_MODEL), lambda i, *_: (jnp.minimum(i, nst - 1) + base, 0))
    out_shape = [jax.ShapeDtypeStruct((N_TOK, q_w), BF16), jax.ShapeDtypeStruct((N_TOK, kv_w), BF16)]
    out_specs = [pl.BlockSpec((tm, q_w), tok), pl.BlockSpec((tm, kv_w), tok)]
    for wd in f32_widths:
        out_shape.append(jax.ShapeDtypeStruct((N_PROMPT, wd), F32))
        out_specs.append(pl.BlockSpec((tm, wd), prm))
    return pl.pallas_call(
        functools.partial(_lnproj_kernel, chunks=chunks, use_rope=use_rope),
        out_shape=out_shape,
        grid_spec=pltpu.PrefetchScalarGridSpec(
            num_scalar_prefetch=2,
            grid=(N_TOK // tm,),
            in_specs=[
                xa_spec, xb_spec,
                pl.BlockSpec((COND_ROWS, 6 * D_MODEL), const),
                pl.BlockSpec((1, D_MODEL), const),
                pl.BlockSpec((D_MODEL, dout), const),
                pl.BlockSpec((1, dout), const),
                pl.BlockSpec((2 * LANES, 2 * LANES), const),
                pl.BlockSpec((tm, LANES), rope_map),
                pl.BlockSpec((tm, LANES), rope_map),
                pl.BlockSpec((tm, LANES), rope_map),
            ],
            out_specs=out_specs,
        ),
        compiler_params=pltpu.CompilerParams(
            dimension_semantics=("arbitrary",), vmem_limit_bytes=VMEM_LIMIT),
        name="lnproj",
    )(tile_cond, tile_rope, xa, xb, mod_l, gain.reshape(1, D_MODEL), w_bf16, head_gain, gmat, cos, sa, sb)


def _softmax_parts(q, k_new, k_ctx, want_sum=True):
    s_n = _dot_nt(q, k_new)
    m = jnp.max(s_n, axis=-1, keepdims=True)
    if k_ctx is not None:
        s_c = _dot_nt(q, k_ctx)
        m = jnp.maximum(m, jnp.max(s_c, axis=-1, keepdims=True))
    p_n = jnp.exp2(s_n - m)
    l = jnp.sum(p_n, axis=-1, keepdims=True) if want_sum else None
    p_c = None
    if k_ctx is not None:
        p_c = jnp.exp2(s_c - m)
        if want_sum:
            l = l + jnp.sum(p_c, axis=-1, keepdims=True)
    return p_n, p_c, l


def _gqa_pairs(q_ref, q_col0, kv_ref, k_col0, v_col0, ck_ref, cv_ref, n_kv, group, tq):
    low = _low_half((tq, LANES))
    mxu_sums = group > 1
    head_out = [None] * (n_kv * group)
    for pair in range(n_kv // 2):
        lanes = slice(pair * LANES, (pair + 1) * LANES)
        k_n = kv_ref[:, k_col0 + lanes.start:k_col0 + lanes.stop]
        v_pair = kv_ref[:, v_col0 + lanes.start:v_col0 + lanes.stop]
        k_c = vc_pair = None
        if ck_ref is not None:
            k_c = ck_ref[:, lanes].astype(BF16)
            vc_pair = cv_ref[:, lanes].astype(BF16)
        for half in range(2):
            kvh = 2 * pair + half
            keep = low if half == 0 else jnp.logical_not(low)
            v_n, v_c = v_pair, vc_pair
            if mxu_sums:
                own = _low_half(v_pair.shape) if half == 0 else jnp.logical_not(_low_half(v_pair.shape))
                v_n = jnp.where(own, v_pair, jnp.ones_like(v_pair))
                if vc_pair is not None:
                    own_c = _low_half(vc_pair.shape) if half == 0 else jnp.logical_not(_low_half(vc_pair.shape))
                    v_c = jnp.where(own_c, vc_pair, jnp.ones_like(vc_pair))
            qs = []
            for g in range(group):
                head = kvh * group + g
                blk = q_ref[:, q_col0 + (head // 2) * LANES:q_col0 + (head // 2 + 1) * LANES]
                if head % 2 != half:
                    blk = _swap_halves(blk)
                qs.append(jnp.where(keep, blk, jnp.zeros_like(blk)))
            q = qs[0] if group == 1 else jnp.concatenate(qs, axis=0)
            p_n, p_c, l = _softmax_parts(q, k_n, k_c, want_sum=not mxu_sums)
            o = _dot(p_n.astype(BF16), v_n)
            if p_c is not None:
                o = o + _dot(p_c.astype(BF16), v_c)
            o = o * (1.0 / (pltpu.roll(o, HEAD_DIM, 1) if mxu_sums else l))
            for g in range(group):
                head = kvh * group + g
                og = o[g * tq:(g + 1) * tq]
                if head % 2 != half:
                    og = pltpu.roll(og, HEAD_DIM, 1)
                head_out[head] = og
    return [jnp.where(low, head_out[2 * k], head_out[2 * k + 1]) for k in range(n_kv * group // 2)]


def _attn_ab_kernel(*refs, has_cache, tq, lam_init):
    if has_cache:
        q_ref, kv_ref, cak_ref, cav_ref, cbk_ref, cbv_ref, lamv_ref, subln_ref, o_ref = refs
    else:
        q_ref, kv_ref, lamv_ref, subln_ref, o_ref = refs
        cak_ref = cav_ref = cbk_ref = cbv_ref = None
    outs = _gqa_pairs(q_ref, 0, kv_ref, 0, A_KV_W, cak_ref, cav_ref, A_KV_HEADS, A_GROUP, tq)

    lv = lamv_ref[...]
    l1 = jnp.sum(lv[0:1] * lv[1:2], axis=-1, keepdims=True)
    l2 = jnp.sum(lv[2:3] * lv[3:4], axis=-1, keepdims=True)
    lam = jnp.exp(l1) - jnp.exp(l2) + lam_init
    bk0 = 2 * A_KV_W
    bv0 = bk0 + B_QK_W
    low = _low_half((tq, LANES))
    for h in range(B_HEADS):
        lanes = slice(h * LANES, (h + 1) * LANES)
        qp = q_ref[:, A_Q_W + lanes.start:A_Q_W + lanes.stop]
        zero = jnp.zeros_like(qp)
        q = jnp.concatenate([jnp.where(low, qp, zero), jnp.where(low, zero, qp)], axis=0)
        k_n = kv_ref[:, bk0 + lanes.start:bk0 + lanes.stop]
        v_n = kv_ref[:, bv0 + lanes.start:bv0 + lanes.stop]
        k_c = v_c = None
        if has_cache:
            k_c = cbk_ref[:, lanes].astype(BF16)
            v_c = cbv_ref[:, lanes].astype(BF16)
        p_n, p_c, l = _softmax_parts(q, k_n, k_c)
        r = 1.0 / l
        r1 = r[:tq]
        r2 = lam * r[tq:]
        o = _dot((p_n[:tq] * r1 - p_n[tq:] * r2).astype(BF16), v_n)
        if has_cache:
            o = o + _dot((p_c[:tq] * r1 - p_c[tq:] * r2).astype(BF16), v_c)
        ms = jnp.mean(o * o, axis=-1, keepdims=True)
        o = o * lax.rsqrt(ms + RMS_EPS) * subln_ref[...] * (1.0 - lam_init)
        outs.append(o)
    o_ref[...] = jnp.concatenate(outs, axis=1).astype(BF16)


def _attn_ab(q, kv, lamv, subln, lam_init, caches):
    kv_w = kv.shape[1]
    kern = functools.partial(_attn_ab_kernel, lam_init=lam_init)
    const = lambda b, j: (0, 0)
    cp = pltpu.CompilerParams(dimension_semantics=("arbitrary", "arbitrary"), vmem_limit_bytes=VMEM_LIMIT)
    if caches is None:
        tq = SEQ
        return pl.pallas_call(
            functools.partial(kern, has_cache=False, tq=tq),
            out_shape=jax.ShapeDtypeStruct((N_PROMPT, D_MODEL), BF16),
            grid=(BATCH, 1),
            in_specs=[
                pl.BlockSpec((tq, D_MODEL), lambda b, j: (b, 0)),
                pl.BlockSpec((SEQ, kv_w), lambda b, j: (b, 0)),
                pl.BlockSpec((4, HEAD_DIM), const),
                pl.BlockSpec((1, B_V_DIM), const),
            ],
            out_specs=pl.BlockSpec((tq, D_MODEL), lambda b, j: (b, 0)),
            compiler_params=cp,
            name="attn_ab_prompt",
        )(q, kv, lamv, subln)
    tq = 256
    nq = DEC_SEQ // tq
    q0 = N_PROMPT // tq
    kv0 = N_PROMPT // DEC_SEQ
    cak, cav, cbk, cbv = caches
    cspec = lambda w: pl.BlockSpec((None, PAST_LEN, w), lambda b, j: (b, 0, 0))
    return pl.pallas_call(
        functools.partial(kern, has_cache=True, tq=tq),
        out_shape=jax.ShapeDtypeStruct((N_SAMPLE, D_MODEL), BF16),
        grid=(DEC_BATCH, nq),
        in_specs=[
            pl.BlockSpec((tq, D_MODEL), lambda b, j: (q0 + b * nq + j, 0)),
            pl.BlockSpec((DEC_SEQ, kv_w), lambda b, j: (kv0 + b, 0)),
            cspec(A_KV_W), cspec(A_KV_W), cspec(B_QK_W), cspec(B_V_W),
            pl.BlockSpec((4, HEAD_DIM), const),
            pl.BlockSpec((1, B_V_DIM), const),
        ],
        out_specs=pl.BlockSpec((tq, D_MODEL), lambda b, j: (b * nq + j, 0)),
        compiler_params=cp,
        name="attn_ab_sample",
    )(q, kv, cak, cav, cbk, cbv, lamv, subln)


def _attn_c_prompt_kernel(q_ref, kv_ref, o_ref):
    outs = _gqa_pairs(q_ref, 0, kv_ref, 0, C_WIDTH, None, None, C_HEADS, 1, SEQ)
    o_ref[...] = jnp.concatenate(outs, axis=1).astype(BF16)


def _attn_c_prompt(q, kv):
    return pl.pallas_call(
        _attn_c_prompt_kernel,
        out_shape=jax.ShapeDtypeStruct((N_PROMPT, D_MODEL), BF16),
        grid=(BATCH,),
        in_specs=[
            pl.BlockSpec((SEQ, C_WIDTH), lambda b: (b, 0)),
            pl.BlockSpec((SEQ, 2 * C_WIDTH), lambda b: (b, 0)),
        ],
        out_specs=pl.BlockSpec((SEQ, C_WIDTH), lambda b: (b, 0)),
        compiler_params=pltpu.CompilerParams(dimension_semantics=("arbitrary",), vmem_limit_bytes=VMEM_LIMIT),
        name="attn_c_prompt",
    )(q, kv)


def _na_kernel(q_ref, kv_ref, ck_ref, cv_ref, tp_ref, o_ref, ckb, cvb):
    step = pl.program_id(1)

    @pl.when(step == 0)
    def _():
        ckb[...] = ck_ref[...].astype(BF16)
        cvb[...] = cv_ref[...].astype(BF16)

    kh = min(NA_ROWS, GRID_ROWS)
    win = kh * GRID_W
    low = _low_half((GRID_W, LANES))
    n_pair = C_HEADS // 2
    for rr in range(NA_ROWS_PER_STEP):
        r = step * NA_ROWS_PER_STEP + rr
        rows = slice(rr * GRID_W, (rr + 1) * GRID_W)
        rs = jnp.clip(r - kh // 2, 0, GRID_ROWS - kh)
        ro0 = rs - r + (NA_ROWS - 1)
        start = pl.multiple_of(rs * GRID_W, GRID_W)
        s_lat, s_ctx = [], []
        for j in range(n_pair):
            lanes = slice(j * LANES, (j + 1) * LANES)
            qp = q_ref[rows, lanes]
            zero = jnp.zeros_like(qp)
            q = jnp.concatenate([jnp.where(low, qp, zero), jnp.where(low, zero, qp)], axis=0)
            kw = kv_ref[pl.ds(start, win), lanes]
            bias = jnp.concatenate(
                [jnp.concatenate([tp_ref[2 * j + hh, ro0 + 2 * t] for t in range(kh // 2)], axis=1)
                 for hh in range(2)], axis=0)
            s_lat.append(_dot_nt(q, kw) + bias)
            s_ctx.append(_dot_nt(q, ckb[:, lanes]))
        outs = []
        for j in range(n_pair):
            lanes = slice(j * LANES, (j + 1) * LANES)
            s_l, s_c = s_lat[j], s_ctx[j]
            m = jnp.maximum(jnp.max(s_l, axis=-1, keepdims=True), jnp.max(s_c, axis=-1, keepdims=True))
            p_l = jnp.exp2(s_l - m)
            p_c = jnp.exp2(s_c - m)
            l = jnp.sum(p_l, axis=-1, keepdims=True) + jnp.sum(p_c, axis=-1, keepdims=True)
            vw = kv_ref[pl.ds(start, win), C_WIDTH + lanes.start:C_WIDTH + lanes.stop]
            o = _dot(p_c.astype(BF16), cvb[:, lanes]) + _dot(p_l.astype(BF16), vw)
            o = o * (1.0 / l)
            outs.append(jnp.where(low, o[:GRID_W], o[GRID_W:]))
        o_ref[rows, :] = jnp.concatenate(outs, axis=1).astype(BF16)


NA_BIAS_PAD = GRID_W - NA_COLS


def _na_bias_kernel(w_ref, mask_ref, tp_ref):
    low = _low_half((GRID_W, LANES))
    keep = mask_ref[...] > 0.0
    n_off = 2 * NA_ROWS - 1
    left, right = [], []
    for ro in range(n_off):
        row = jnp.broadcast_to(w_ref[ro:ro + 1, :] * LOG2E, (GRID_W, LANES))
        left.append(pltpu.roll(row, LANES - GRID_W + 1, 1, stride=1, stride_axis=0))
        right.append(pltpu.roll(row, 1, 1, stride=1, stride_axis=0))
    for t in range(n_off - 1):
        tp_ref[t] = jnp.where(keep, jnp.where(low, left[t], right[t + 1]), NEG_BIG)


def _na_bias_table(rpb):
    cols = np.arange(GRID_W)
    col_start = np.clip(cols - NA_COLS // 2, 0, GRID_W - NA_COLS)
    col_in = (cols[None, :] >= col_start[:, None]) & (cols[None, :] < col_start[:, None] + NA_COLS)
    assert np.abs((cols[None, :] - cols[:, None])[col_in]).max() <= NA_COLS - 1
    mask = jnp.asarray(np.tile(col_in, (1, LANES // GRID_W)), dtype=F32)
    n_off = 2 * NA_ROWS - 1
    n_rel = 2 * NA_COLS - 1
    w = jnp.pad(rpb.astype(F32), ((0, 0), (0, 0), (NA_BIAS_PAD, LANES - NA_BIAS_PAD - n_rel)))
    return pl.pallas_call(
        _na_bias_kernel,
        out_shape=jax.ShapeDtypeStruct((C_HEADS, n_off - 1, GRID_W, LANES), F32),
        grid=(C_HEADS,),
        in_specs=[
            pl.BlockSpec((None, n_off, LANES), lambda h: (h, 0, 0)),
            pl.BlockSpec((GRID_W, LANES), lambda h: (0, 0)),
        ],
        out_specs=pl.BlockSpec((None, n_off - 1, GRID_W, LANES), lambda h: (h, 0, 0, 0)),
        compiler_params=pltpu.CompilerParams(dimension_semantics=("arbitrary",)),
        name="na_bias",
    )(w, mask)


def _attn_na(q, kv, ck, cv, tp):
    tq = NA_ROWS_PER_STEP * GRID_W
    steps = GRID_ROWS // NA_ROWS_PER_STEP
    q0 = N_PROMPT // tq
    kv0 = N_PROMPT // DEC_SEQ
    return pl.pallas_call(
        _na_kernel,
        out_shape=jax.ShapeDtypeStruct((N_SAMPLE, D_MODEL), BF16),
        grid=(DEC_BATCH, steps),
        in_specs=[
            pl.BlockSpec((tq, C_WIDTH), lambda b, r: (q0 + b * steps + r, 0)),
            pl.BlockSpec((DEC_SEQ, 2 * C_WIDTH), lambda b, r: (kv0 + b, 0)),
            pl.BlockSpec((None, PAST_LEN, C_WIDTH), lambda b, r: (b, 0, 0)),
            pl.BlockSpec((None, PAST_LEN, C_WIDTH), lambda b, r: (b, 0, 0)),
            pl.BlockSpec(tp.shape, lambda b, r: (0, 0, 0, 0)),
        ],
        out_specs=pl.BlockSpec((tq, C_WIDTH), lambda b, r: (b * steps + r, 0)),
        scratch_shapes=[pltpu.VMEM((PAST_LEN, C_WIDTH), BF16), pltpu.VMEM((PAST_LEN, C_WIDTH), BF16)],
        compiler_params=pltpu.CompilerParams(
            dimension_semantics=("arbitrary", "arbitrary"), vmem_limit_bytes=VMEM_LIMIT),
        name="attn_na_sample",
    )(q, kv, ck, cv, tp)


def _first_wins_ranks(vals):
    ranks = []
    for i in range(len(vals)):
        r = jnp.zeros_like(vals[i])
        for j in range(len(vals)):
            if j == i:
                continue
            beats = (vals[j] >= vals[i]) if j < i else (vals[j] > vals[i])
            r = r + jnp.where(beats, 1.0, 0.0)
        ranks.append(r)
    return ranks


def _outproj_kernel(tile_cond_ref, op_ref, os_ref, xa_ref, xb_ref, w_ref, mod_ref, g2_ref, wrt_ref, br_ref,
                    tri_ref, x1_ref, h2_ref, gates_ref, rank_ref, carry_ref):
    i = pl.program_id(0)
    row = tile_cond_ref[i]
    is_prompt = i < N_PROMPT // TM_OUT

    @pl.when(i == 0)
    def _():
        carry_ref[...] = jnp.zeros_like(carry_ref)

    o = jnp.where(is_prompt, op_ref[...], os_ref[...])
    x = jnp.where(is_prompt, xa_ref[...], xb_ref[...])
    acc = _dot(o, w_ref[...])
    gate = mod_ref[pl.ds(row, 1), 2 * D_MODEL:3 * D_MODEL]
    x1 = x + gate * acc
    x1_ref[...] = x1
    ms = jnp.mean(x1 * x1, axis=-1, keepdims=True)
    xn = x1 * lax.rsqrt(ms + RMS_EPS) * g2_ref[...]
    shift = mod_ref[pl.ds(row, 1), 3 * D_MODEL:4 * D_MODEL]
    scale = mod_ref[pl.ds(row, 1), 4 * D_MODEL:5 * D_MODEL]
    h2 = xn * (1.0 + scale) + shift
    for s in range(ROW_TILES):
        h2_ref[pl.ds(s, TM_OUT, stride=ROW_TILES), :] = h2[:, s * LANES:(s + 1) * LANES]

    h_hi = h2.astype(BF16)
    h_lo = (h2 - h_hi.astype(F32)).astype(BF16)
    part = _dot_nt(wrt_ref[...], h_hi)
    logits = part[:N_EXPERTS] + part[N_EXPERTS:] + _dot_nt(wrt_ref[:N_EXPERTS, :], h_lo)
    e = jnp.exp(logits - jnp.max(logits, axis=0, keepdims=True))
    scores = e * (1.0 / jnp.sum(e, axis=0, keepdims=True))
    sel = scores + br_ref[...]
    sel_rows = [sel[k:k + 1, :] for k in range(N_EXPERTS)]
    in_top2 = []
    group_sum = []
    for g in range(N_GROUPS):
        vals = sel_rows[g * EXPERTS_PER_GROUP:(g + 1) * EXPERTS_PER_GROUP]
        ranks = _first_wins_ranks(vals)
        top = [rk < 2.0 for rk in ranks]
        in_top2.extend(top)
        s = jnp.zeros_like(vals[0])
        for v, t in zip(vals, top):
            s = s + jnp.where(t, v, 0.0)
        group_sum.append(s)
    group_rank = _first_wins_ranks(group_sum)
    mask_rows = []
    for k in range(N_EXPERTS):
        chosen = jnp.where(in_top2[k], 1.0, 0.0) * jnp.where(group_rank[k // EXPERTS_PER_GROUP] < 1.0, 1.0, 0.0)
        mask_rows.append(chosen)
    mask = jnp.concatenate(mask_rows, axis=0)
    picked = scores * mask
    gates_ref[...] = picked * (1.0 / jnp.sum(picked, axis=0, keepdims=True))
    prefix = _dot(mask.astype(BF16), tri_ref[...])
    rank_ref[...] = jnp.where(mask > 0.0, prefix + carry_ref[...], -1.0)
    carry_ref[...] = carry_ref[...] + jnp.sum(mask, axis=1, keepdims=True)


def _outproj(o_prompt, o_sample, xa, xb, merged, w_bf16, mod_l, gain2, w_router, b_router):
    const = lambda i, *_: (0, 0)
    tok = lambda i, *_: (i, 0)
    tokT = lambda i, *_: (0, i)
    tm = TM_OUT
    tile_cond, _ = _tile_cond(tm)
    op_spec, os_spec = _split_specs(tm, False)
    xa_spec, xb_spec = _split_specs(tm, merged)
    tri = jnp.asarray(np.triu(np.ones((tm, tm)), k=1), dtype=BF16)
    wrt = w_router.T.astype(F32)
    wrt_hi = wrt.astype(BF16)
    wrt_split = jnp.concatenate([wrt_hi, (wrt - wrt_hi.astype(F32)).astype(BF16)], axis=0)
    return pl.pallas_call(
        _outproj_kernel,
        out_shape=[
            jax.ShapeDtypeStruct((N_TOK, D_MODEL), F32),
            jax.ShapeDtypeStruct((N_TOK * ROW_TILES, LANES), F32),
            jax.ShapeDtypeStruct((N_EXPERTS, N_TOK), F32),
            jax.ShapeDtypeStruct((N_EXPERTS, N_TOK), F32),
        ],
        grid_spec=pltpu.PrefetchScalarGridSpec(
            num_scalar_prefetch=1,
            grid=(N_TOK // tm,),
            in_specs=[
                op_spec, os_spec, xa_spec, xb_spec,
                pl.BlockSpec((D_MODEL, D_MODEL), const),
                pl.BlockSpec((COND_ROWS, 6 * D_MODEL), const),
                pl.BlockSpec((1, D_MODEL), const),
                pl.BlockSpec((2 * N_EXPERTS, D_MODEL), const),
                pl.BlockSpec((N_EXPERTS, 1), const),
                pl.BlockSpec((tm, tm), const),
            ],
            out_specs=[
                pl.BlockSpec((tm, D_MODEL), tok),
                pl.BlockSpec((tm * ROW_TILES, LANES), tok),
                pl.BlockSpec((N_EXPERTS, tm), tokT),
                pl.BlockSpec((N_EXPERTS, tm), tokT),
            ],
            scratch_shapes=[pltpu.VMEM((N_EXPERTS, 1), F32)],
        ),
        compiler_params=pltpu.CompilerParams(
            dimension_semantics=("arbitrary",), vmem_limit_bytes=VMEM_LIMIT),
        name="outproj_router",
    )(tile_cond, o_prompt, o_sample, xa, xb, w_bf16, mod_l, gain2.reshape(1, D_MODEL), wrt_split,
      b_router.reshape(N_EXPERTS, 1), tri)


def _plan_kernel(gates_ref, rank_ref, dests_ref, wts_ref, last_ref, misc_ref):
    rank = rank_ref[...]
    gates = gates_ref[...]
    sel = rank >= 0.0
    hit = jnp.where(sel, 1.0, 0.0)
    tile = float(MOE_TM)

    def pad_up(c):
        return jnp.floor((c + (tile - 1.0)) * (1.0 / tile)) * tile

    counts_col = jnp.sum(hit, axis=1, keepdims=True)
    ones = jnp.ones((SUBLANES, hit.shape[1]), BF16)
    counts_row = _dot_nt(ones, hit.astype(BF16))[0:1]
    padded_col = pad_up(counts_col)
    padded_row = pad_up(counts_row)
    e_sub = lax.broadcasted_iota(jnp.int32, (N_EXPERTS, N_EXPERTS), 0)
    e_lane = lax.broadcasted_iota(jnp.int32, (N_EXPERTS, N_EXPERTS), 1)
    ends_col = jnp.sum(jnp.where(e_lane <= e_sub, padded_row, 0.0), axis=1, keepdims=True)
    ends_row = jnp.sum(jnp.where(e_sub <= e_lane, padded_col, 0.0), axis=0, keepdims=True)
    dest = jnp.where(sel, ends_col - padded_col + rank, -1.0)
    d1 = jnp.max(dest, axis=0, keepdims=True)
    d0 = jnp.min(jnp.where(sel, dest, float(MOE_ROWS)), axis=0, keepdims=True)
    w0 = jnp.sum(jnp.where(dest == d0, gates, 0.0), axis=0, keepdims=True)
    w1 = jnp.sum(jnp.where(dest == d1, gates, 0.0), axis=0, keepdims=True)
    dests_ref[...] = (jnp.concatenate([d0, d1], axis=0) * float(ROW_TILES)).astype(jnp.int32)
    wts_ref[...] = jnp.concatenate([w0, w1], axis=0)

    last = jnp.where(padded_col > 0.0, (ends_col - tile) * float(ROW_TILES), -1.0)
    last_ref[...] = jnp.broadcast_to(last, last_ref.shape).astype(jnp.int32)

    lane = lax.broadcasted_iota(jnp.int32, (1, LANES), 1).astype(F32)
    n_tiles = ends_row[:, N_EXPERTS - 1:N_EXPERTS] * (1.0 / tile)
    tile_expert = jnp.sum(jnp.where(lane * tile >= ends_col, 1.0, 0.0), axis=0, keepdims=True)
    tile_expert = jnp.minimum(tile_expert, float(N_EXPERTS - 1))
    e_col = lax.broadcasted_iota(jnp.int32, (N_EXPERTS, 1), 0).astype(F32)
    later = jnp.where((e_col > tile_expert) & (padded_col > 0.0), e_col, float(N_EXPERTS))
    nxt = jnp.min(later, axis=0, keepdims=True)
    nxt = jnp.where(nxt > float(N_EXPERTS - 1), tile_expert, nxt)
    spare = n_tiles + lane
    spare = jnp.where(spare < float(MOE_TILES), spare * (tile * ROW_TILES), -1.0)
    zero = jnp.zeros((1, LANES), F32)
    rows = [tile_expert, nxt, spare, n_tiles + zero] + [zero] * (SUBLANES - 4)
    misc_ref[...] = jnp.concatenate(rows, axis=0).astype(jnp.int32)


def _route_plan(gates, rank):
    assert MOE_TILES <= LANES and N_EXPERTS <= LANES
    full = lambda shape: pl.BlockSpec(shape, lambda: (0,) * len(shape))
    dests, wts, last, misc = pl.pallas_call(
        _plan_kernel,
        out_shape=[
            jax.ShapeDtypeStruct((2, N_TOK), jnp.int32),
            jax.ShapeDtypeStruct((2, N_TOK), F32),
            jax.ShapeDtypeStruct((N_EXPERTS, LANES), jnp.int32),
            jax.ShapeDtypeStruct((SUBLANES, LANES), jnp.int32),
        ],
        in_specs=[full((N_EXPERTS, N_TOK)), full((N_EXPERTS, N_TOK))],
        out_specs=[full((2, N_TOK)), full((2, N_TOK)), full((N_EXPERTS, LANES)), full((SUBLANES, LANES))],
        compiler_params=pltpu.CompilerParams(vmem_limit_bytes=VMEM_LIMIT),
        name="route_plan",
    )(gates, rank)
    tile_expert = misc[0, :MOE_TILES]
    next_expert = misc[1, :MOE_TILES]
    n_tiles = misc[3, :1]
    clear = jnp.concatenate([last[:, 0], misc[2, :N_EXPERTS]])
    return dests.reshape(2 * N_TOK), wts.T, tile_expert, n_tiles, next_expert, clear


def _scatter_kernel(d_ref, last_ref, h_ref, xs_ref, zeros, sem, zsem):
    i = pl.program_id(0)
    base = i * TM
    rows = MOE_TM * ROW_TILES

    @pl.when(i == 0)
    def _():
        zeros[...] = jnp.zeros_like(zeros)
        for e in range(2 * N_EXPERTS):
            @pl.when(last_ref[e] >= 0)
            def _():
                at = pl.multiple_of(last_ref[e], SUBLANES)
                pltpu.make_async_copy(zeros, xs_ref.at[pl.ds(at, rows)], zsem).start()
        for e in range(2 * N_EXPERTS):
            @pl.when(last_ref[e] >= 0)
            def _():
                pltpu.make_async_copy(zeros, xs_ref.at[pl.ds(0, rows)], zsem).wait()

    def copy(r, t):
        src = h_ref.at[pl.ds(pl.multiple_of(r * ROW_TILES, SUBLANES), ROW_TILES)]
        dst = xs_ref.at[pl.ds(pl.multiple_of(t, SUBLANES), ROW_TILES)]
        return pltpu.make_async_copy(src, dst, sem)

    def issue(r, c):
        copy(r, d_ref[base + r]).start(priority=0)
        copy(r, d_ref[N_TOK + base + r]).start(priority=1)
        return c

    lax.fori_loop(0, TM, issue, 0, unroll=8)

    def drain(r, c):
        copy(r, 0).wait()
        copy(r, 0).wait()
        return c

    lax.fori_loop(0, TM, drain, 0, unroll=8)


def _moe_scatter(dests, last_tile, h2):
    return pl.pallas_call(
        _scatter_kernel,
        out_shape=jax.ShapeDtypeStruct((MOE_ROWS * ROW_TILES, LANES), F32),
        grid_spec=pltpu.PrefetchScalarGridSpec(
            num_scalar_prefetch=2,
            grid=(N_TILES,),
            in_specs=[pl.BlockSpec((TM * ROW_TILES, LANES), lambda i, *_: (i, 0))],
            out_specs=pl.BlockSpec(memory_space=pl.ANY),
            scratch_shapes=[
                pltpu.VMEM((MOE_TM * ROW_TILES, LANES), F32),
                pltpu.SemaphoreType.DMA,
                pltpu.SemaphoreType.DMA,
            ],
        ),
        compiler_params=pltpu.CompilerParams(
            dimension_semantics=("arbitrary",), vmem_limit_bytes=VMEM_LIMIT, has_side_effects=True),
        name="moe_scatter",
    )(dests, last_tile, h2)


def _mlp_kernel(te_ref, nv_ref, nxt_ref, xs_ref, wg_hbm, wu_hbm, wd_hbm, y_ref, stage, wgb, wub, wdb, hb, sem,
                *, layer):
    t = pl.program_id(0)

    def fetch(e):
        return [pltpu.make_async_copy(w.at[layer, e], stage.at[k], sem.at[k])
                for k, w in enumerate((wg_hbm, wu_hbm, wd_hbm))]

    @pl.when(t < nv_ref[0])
    def _():
        e = te_ref[t]
        prev = te_ref[jnp.maximum(t - 1, 0)]

        @pl.when(t == 0)
        def _():
            for cp in fetch(e):
                cp.start()

        @pl.when((t == 0) | (e != prev))
        def _():
            for cp in fetch(e):
                cp.wait()
            wgb[...] = stage[0].astype(BF16)
            wub[...] = stage[1].astype(BF16)
            wdb[...] = stage[2].astype(BF16)

            @pl.when(nxt_ref[t] != e)
            def _():
                for cp in fetch(nxt_ref[t]):
                    cp.start()

        x = jnp.concatenate(
            [xs_ref[pl.ds(s, MOE_TM, stride=ROW_TILES), :] for s in range(ROW_TILES)], axis=1).astype(BF16)
        step = 512
        for c in range(0, D_EXPERT, step):
            g = _dot(x, wgb[:, c:c + step])
            u = _dot(x, wub[:, c:c + step])
            hb[:, c:c + step] = (_silu(g) * u).astype(BF16)
        y = _dot(hb[...], wdb[...])
        for s in range(ROW_TILES):
            y_ref[pl.ds(s, MOE_TM, stride=ROW_TILES), :] = y[:, s * LANES:(s + 1) * LANES]

    @pl.when(t >= nv_ref[0])
    def _():
        y_ref[...] = jnp.zeros_like(y_ref)


def _moe_mlp(layer, tile_expert, n_tiles, next_expert, xs, w_gate, w_up, w_down):
    assert D_MODEL == D_EXPERT

    def tile_map(t, te, nv, nx):
        return (jnp.minimum(t, nv[0] - 1), 0)

    def out_map(t, te, nv, nx):
        return (t, 0)

    hbm = pl.BlockSpec(memory_space=pl.ANY)
    return pl.pallas_call(
        functools.partial(_mlp_kernel, layer=layer),
        out_shape=jax.ShapeDtypeStruct((MOE_ROWS * ROW_TILES, LANES), F32),
        grid_spec=pltpu.PrefetchScalarGridSpec(
            num_scalar_prefetch=3,
            grid=(MOE_TILES,),
            in_specs=[pl.BlockSpec((MOE_TM * ROW_TILES, LANES), tile_map), hbm, hbm, hbm],
            out_specs=pl.BlockSpec((MOE_TM * ROW_TILES, LANES), out_map),
            scratch_shapes=[
                pltpu.VMEM((3, D_MODEL, D_EXPERT), F32),
                pltpu.VMEM((D_MODEL, D_EXPERT), BF16),
                pltpu.VMEM((D_MODEL, D_EXPERT), BF16),
                pltpu.VMEM((D_EXPERT, D_MODEL), BF16),
                pltpu.VMEM((MOE_TM, D_EXPERT), BF16),
                pltpu.SemaphoreType.DMA((3,)),
            ],
        ),
        compiler_params=pltpu.CompilerParams(
            dimension_semantics=("arbitrary",), vmem_limit_bytes=VMEM_LIMIT),
        name="moe_mlp",
    )(tile_expert, n_tiles, next_expert, xs, w_gate, w_up, w_down)


def _combine_kernel(d_ref, tile_cond_ref, x1_ref, w_ref, mod_ref, y_ref, *rest, split_out):
    if split_out:
        outp_ref, outs_ref, buf, sem = rest
    else:
        out_ref, buf, sem = rest
    i = pl.program_id(0)
    row = tile_cond_ref[i]
    slot = i % 2

    def copy(sl, k, r, t):
        src = y_ref.at[pl.ds(pl.multiple_of(t, SUBLANES), ROW_TILES)]
        dst = buf.at[sl, k, pl.ds(pl.multiple_of(r * ROW_TILES, SUBLANES), ROW_TILES)]
        return pltpu.make_async_copy(src, dst, sem.at[sl])

    def gather_tile(tile, sl):
        base = tile * TM

        def issue(r, c):
            copy(sl, 0, r, d_ref[base + r]).start(priority=0)
            copy(sl, 1, r, d_ref[N_TOK + base + r]).start(priority=1)
            return c

        lax.fori_loop(0, TM, issue, 0, unroll=8)

    @pl.when(i == 0)
    def _():
        gather_tile(0, 0)

    @pl.when(i + 1 < N_TILES)
    def _():
        gather_tile(i + 1, 1 - slot)

    def drain(r, c):
        copy(slot, 0, r, 0).wait()
        copy(slot, 1, r, 0).wait()
        return c

    lax.fori_loop(0, TM, drain, 0, unroll=8)
    w = w_ref[...]
    w0 = w[:, 0:1]
    w1 = w[:, 1:2]
    parts = []
    for s in range(ROW_TILES):
        y0 = buf[slot, 0, pl.ds(s, TM, stride=ROW_TILES), :]
        y1 = buf[slot, 1, pl.ds(s, TM, stride=ROW_TILES), :]
        parts.append(w0 * y0 + w1 * y1)
    gate = mod_ref[pl.ds(row, 1), 5 * D_MODEL:6 * D_MODEL]
    out = x1_ref[...] + gate * jnp.concatenate(parts, axis=1)
    if split_out:
        @pl.when(i < N_PROMPT_TILES)
        def _():
            outp_ref[...] = out

        @pl.when(i >= N_PROMPT_TILES)
        def _():
            outs_ref[...] = out
    else:
        out_ref[...] = out


def _moe_combine(dests, x1, weights, mod_l, y, split_out):
    tile_cond, _ = _tile_cond(TM)
    if split_out:
        out_shape = [jax.ShapeDtypeStruct((N_PROMPT, D_MODEL), F32), jax.ShapeDtypeStruct((N_SAMPLE, D_MODEL), F32)]
        out_specs = list(_split_specs(TM, False))
    else:
        out_shape = jax.ShapeDtypeStruct((N_TOK, D_MODEL), F32)
        out_specs = pl.BlockSpec((TM, D_MODEL), lambda i, *_: (i, 0))
    return pl.pallas_call(
        functools.partial(_combine_kernel, split_out=split_out),
        out_shape=out_shape,
        grid_spec=pltpu.PrefetchScalarGridSpec(
            num_scalar_prefetch=2,
            grid=(N_TILES,),
            in_specs=[
                pl.BlockSpec((TM, D_MODEL), lambda i, *_: (i, 0)),
                pl.BlockSpec((TM, 2), lambda i, *_: (i, 0)),
                pl.BlockSpec((COND_ROWS, 6 * D_MODEL), lambda i, *_: (0, 0)),
                pl.BlockSpec(memory_space=pl.ANY),
            ],
            out_specs=out_specs,
            scratch_shapes=[pltpu.VMEM((2, 2, TM * ROW_TILES, LANES), F32), pltpu.SemaphoreType.DMA((2,))],
        ),
        compiler_params=pltpu.CompilerParams(
            dimension_semantics=("arbitrary",), vmem_limit_bytes=VMEM_LIMIT),
        name="moe_combine",
    )(dests, tile_cond, x1, weights, mod_l, y)


def _chunk_plan(segments):
    chunks = []
    for width, normed, dst, dst_col0, f32_dst in segments:
        for k in range(width // LANES):
            chunks.append((normed, dst, dst_col0 + k * LANES, f32_dst, k * LANES))
    return tuple(chunks)


def _head_gain(parts):
    cols = []
    for width, g, mult in parts:
        if g is None:
            cols.append(jnp.ones((width,), F32))
        else:
            cols.append(jnp.tile(g.astype(F32) * mult, width // HEAD_DIM))
    return jnp.concatenate(cols).reshape(1, -1)


def kernel(x_prompt, x_sample, cache_a_k, cache_a_v, cache_b_k, cache_b_v, cache_c_k, cache_c_v, c, c_ctx, w_mod, b_mod, norm_mix, norm_ffn, w_in_ab, w_out_ab, a_q_norm, a_k_norm, b_q_norm, b_k_norm, lam_q1, lam_k1, lam_q2, lam_k2, b_subln, w_in_c, w_out_c, c_q_norm, c_k_norm, c_rpb, w_router, b_router, w_gate, w_up, w_down):
    scale = LOG2E * HEAD_DIM ** -0.5
    cond = jnp.concatenate(
        [c, c_ctx[None, :], jnp.zeros((COND_ROWS - DEC_BATCH - 1, D_MODEL), F32)], axis=0)
    mod = _modulation(cond, w_mod, b_mod)

    xa = x_prompt.reshape(N_PROMPT, D_MODEL)
    xb = x_sample.reshape(N_SAMPLE, D_MODEL)
    merged = False
    new_caches = []
    for l in range(DEPTH):
        mod_l = mod[l]
        if l % 2 == 0:
            e = l // 2
            lam_init = 0.8 - 0.6 * math.exp(-0.3 * l)
            segments = (
                (A_Q_W, True, 0, 0, None),
                (A_KV_W, True, 1, 0, 2),
                (A_KV_W, False, 1, A_KV_W, 3),
                (B_QK_W, True, 0, A_Q_W, None),
                (B_QK_W, True, 1, 2 * A_KV_W, 4),
                (B_V_W, False, 1, 2 * A_KV_W + B_QK_W, 5),
            )
            hg = _head_gain((
                (A_Q_W, a_q_norm[e], scale), (A_KV_W, a_k_norm[e], 1.0), (A_KV_W, None, 1.0),
                (B_QK_W, b_q_norm[e], scale), (B_QK_W, b_k_norm[e], 1.0), (B_V_W, None, 1.0)))
            outs = _lnproj(xa, xb, merged, mod_l, norm_mix[l], w_in_ab[e].astype(BF16), hg, _chunk_plan(segments),
                           A_Q_W + B_QK_W, 2 * A_KV_W + B_QK_W + B_V_W,
                           (A_KV_W, A_KV_W, B_QK_W, B_V_W), True)
            q, kv, ak, av, bk, bv = outs
            new_caches.append((
                ak.reshape(BATCH, SEQ, A_KV_HEADS, HEAD_DIM), av.reshape(BATCH, SEQ, A_KV_HEADS, HEAD_DIM),
                bk.reshape(BATCH, SEQ, B_HEADS, 2, HEAD_DIM), bv.reshape(BATCH, SEQ, B_HEADS, B_V_DIM)))
            lamv = jnp.stack([lam_q1[e], lam_k1[e], lam_q2[e], lam_k2[e]]).astype(F32)
            subln = b_subln[e].reshape(1, B_V_DIM)
            o_p = _attn_ab(q, kv, lamv, subln, lam_init, None)
            caches = (cache_a_k[:, e].reshape(DEC_BATCH, PAST_LEN, A_KV_W),
                      cache_a_v[:, e].reshape(DEC_BATCH, PAST_LEN, A_KV_W),
                      cache_b_k[:, e].reshape(DEC_BATCH, PAST_LEN, B_QK_W),
                      cache_b_v[:, e].reshape(DEC_BATCH, PAST_LEN, B_V_W))
            o_s = _attn_ab(q, kv, lamv, subln, lam_init, caches)
            w_out = w_out_ab[e].astype(BF16)
        else:
            oi = l // 2
            segments = (
                (C_WIDTH, True, 0, 0, None),
                (C_WIDTH, True, 1, 0, 2),
                (C_WIDTH, False, 1, C_WIDTH, 3),
            )
            hg = _head_gain(((C_WIDTH, c_q_norm[oi], scale), (C_WIDTH, c_k_norm[oi], 1.0), (C_WIDTH, None, 1.0)))
            outs = _lnproj(xa, xb, merged, mod_l, norm_mix[l], w_in_c[oi].astype(BF16), hg, _chunk_plan(segments),
                           C_WIDTH, 2 * C_WIDTH, (C_WIDTH, C_WIDTH), False)
            q, kv, ck_new, cv_new = outs
            new_caches.append((ck_new.reshape(BATCH, SEQ, C_HEADS, HEAD_DIM),
                               cv_new.reshape(BATCH, SEQ, C_HEADS, HEAD_DIM)))
            o_p = _attn_c_prompt(q, kv)
            o_s = _attn_na(q, kv,
                           cache_c_k[:, oi].reshape(DEC_BATCH, PAST_LEN, C_WIDTH),
                           cache_c_v[:, oi].reshape(DEC_BATCH, PAST_LEN, C_WIDTH),
                           _na_bias_table(c_rpb[oi]))
            w_out = w_out_c[oi].astype(BF16)

        x1, h2, gates, rank = _outproj(o_p, o_s, xa, xb, merged, w_out, mod_l, norm_ffn[l], w_router, b_router)
        dests, weights, tile_expert, n_tiles, next_expert, clear = _route_plan(gates, rank)
        xs = _moe_scatter(dests, clear, h2)
        y = _moe_mlp(l, tile_expert, n_tiles, next_expert, xs, w_gate, w_up, w_down)
        last = l == DEPTH - 1
        out = _moe_combine(dests, x1, weights, mod_l, y, last)
        if last:
            y_prompt, y_sample = out
        else:
            xa = xb = out
            merged = True

    even = [nc for i, nc in enumerate(new_caches) if i % 2 == 0]
    odd = [nc for i, nc in enumerate(new_caches) if i % 2 == 1]
    stack = lambda items, k: jnp.stack([it[k] for it in items], axis=1)
    return (y_prompt.reshape(BATCH, SEQ, D_MODEL), y_sample.reshape(DEC_BATCH, DEC_SEQ, D_MODEL),
            stack(even, 0), stack(even, 1), stack(even, 2), stack(even, 3),
            stack(odd, 0), stack(odd, 1))
```

```python
import functools
import math

import numpy as np
import jax
import jax.numpy as jnp
from jax import lax
from jax.experimental import pallas as pl
from jax.experimental.pallas import tpu as pltpu

F32 = jnp.float32
BF16 = jnp.bfloat16

D_MODEL = 1024
BATCH = 16
SEQ = 256
DEPTH = 2
DEC_BATCH = 8
DEC_SEQ = 1024
PAST_LEN = 256
GRID_W = 64
HEAD_DIM = 64
ROPE_THETA = 10000.0
RMS_EPS = 1e-6
A_Q_HEADS = 8
A_KV_HEADS = 2
A_GROUP = A_Q_HEADS // A_KV_HEADS
B_HEADS = 4
B_V_DIM = 2 * HEAD_DIM
A_Q_W = A_Q_HEADS * HEAD_DIM
A_KV_W = A_KV_HEADS * HEAD_DIM
B_QK_W = B_HEADS * 2 * HEAD_DIM
B_V_W = B_HEADS * B_V_DIM
C_HEADS = 16
C_WIDTH = C_HEADS * HEAD_DIM
NA_ROWS = 8
NA_COLS = 16
N_EXPERTS = 16
N_GROUPS = 4
EXPERTS_PER_GROUP = N_EXPERTS // N_GROUPS
D_EXPERT = 1024

LANES = 128
SUBLANES = 8
ROW_TILES = D_MODEL // LANES
N_PROMPT = BATCH * SEQ
N_SAMPLE = DEC_BATCH * DEC_SEQ
N_TOK = N_PROMPT + N_SAMPLE
TM = 256
N_TILES = N_TOK // TM
N_PROMPT_TILES = N_PROMPT // TM
TM_LN = 512
LN_PIECE = 512
TM_OUT = 512
COND_ROWS = 16
CTX_COND_ROW = DEC_BATCH
MOE_TM = 256
MOE_ROWS = 2 * N_TOK + N_EXPERTS * MOE_TM
MOE_TILES = MOE_ROWS // MOE_TM
GRID_ROWS = DEC_SEQ // GRID_W
NA_ROWS_PER_STEP = 4
NEG_BIG = -1e30
LOG2E = math.log2(math.e)
VMEM_LIMIT = 56 * 1024 * 1024

def _silu(x):
    return x * (1.0 / (1.0 + jnp.exp(-x)))


def _dot(a, b):
    return jnp.dot(a, b, preferred_element_type=F32)


def _dot_nt(a, b):
    return lax.dot_general(a, b, (((1,), (1,)), ((), ())), preferred_element_type=F32)


def _low_half(shape):
    return lax.broadcasted_iota(jnp.int32, shape, len(shape) - 1) < HEAD_DIM


def _swap_halves(x):
    return jnp.concatenate([x[:, HEAD_DIM:], x[:, :HEAD_DIM]], axis=1)


def _tile_cond(tm):
    tiles = np.arange(N_TOK // tm)
    npt = N_PROMPT // tm
    samp = np.maximum(tiles - npt, 0)
    per_seq = DEC_SEQ // tm
    cond = np.where(tiles < npt, CTX_COND_ROW, samp // per_seq)
    rope = np.where(tiles < npt, per_seq, samp % per_seq)
    return jnp.asarray(cond, jnp.int32), jnp.asarray(rope, jnp.int32)


def _split_specs(tm, merged):
    npt = N_PROMPT // tm
    base = npt if merged else 0
    a = pl.BlockSpec((tm, D_MODEL), lambda i, *_: (jnp.minimum(i, npt - 1), 0))
    b = pl.BlockSpec((tm, D_MODEL), lambda i, *_: (jnp.maximum(i - npt, 0) + base, 0))
    return a, b


def _mod_kernel(c_ref, w_ref, b_ref, o_ref):
    s = _silu(c_ref[...])
    s_hi = s.astype(BF16)
    s_lo = (s - s_hi.astype(F32)).astype(BF16)
    w = w_ref[...].astype(BF16)
    o_ref[...] = _dot(s_hi, w) + _dot(s_lo, w) + b_ref[...]


def _modulation(cond, w_mod, b_mod):
    tn = 1536
    return pl.pallas_call(
        _mod_kernel,
        out_shape=jax.ShapeDtypeStruct((DEPTH, COND_ROWS, 6 * D_MODEL), F32),
        grid=(DEPTH, 6 * D_MODEL // tn),
        in_specs=[
            pl.BlockSpec((COND_ROWS, D_MODEL), lambda l, j: (0, 0)),
            pl.BlockSpec((None, D_MODEL, tn), lambda l, j: (l, 0, j)),
            pl.BlockSpec((None, 1, tn), lambda l, j: (l, 0, j)),
        ],
        out_specs=pl.BlockSpec((None, COND_ROWS, tn), lambda l, j: (l, 0, j)),
        compiler_params=pltpu.CompilerParams(
            dimension_semantics=("arbitrary", "arbitrary"), vmem_limit_bytes=VMEM_LIMIT),
        name="modulation",
    )(cond, w_mod, b_mod.reshape(DEPTH, 1, 6 * D_MODEL))


def _rope_tables(tm):
    pos = np.arange(DEC_SEQ)
    rows = (pos // GRID_W).astype(np.float64)
    cols = (pos % GRID_W).astype(np.float64)
    nfreq = HEAD_DIM // 4
    inv = ROPE_THETA ** (-np.arange(nfreq, dtype=np.float64) / nfreq)
    d = np.arange(HEAD_DIM)
    dd = d % (HEAD_DIM // 2)
    p = np.where((d >= HEAD_DIM // 2)[None, :], cols[:, None], rows[:, None])
    ang = p * inv[dd % nfreq][None, :]
    cos, sin = np.cos(ang), np.sin(ang)
    second = (dd >= nfreq)[None, :]
    sa = np.where(second, sin, 0.0)
    sb = np.where(second, 0.0, -sin)

    def full(t, ident):
        t = np.concatenate([t, np.full((tm, HEAD_DIM), ident)], axis=0)
        return jnp.asarray(np.tile(t, (1, LANES // HEAD_DIM)), dtype=F32)

    return full(cos, 1.0), full(sa, 0.0), full(sb, 0.0)


def _lnproj_kernel(tile_cond_ref, tile_rope_ref, xa_ref, xb_ref, mod_ref, g_ref, w_ref, hg_ref, gmat_ref,
                   cos_ref, sa_ref, sb_ref, *out_refs, chunks, use_rope):
    i = pl.program_id(0)
    row = tile_cond_ref[i]
    x = jnp.where(i >= N_SAMPLE // TM_LN, xa_ref[...], xb_ref[...])
    ms = jnp.mean(x * x, axis=-1, keepdims=True)
    xn = x * lax.rsqrt(ms + RMS_EPS) * g_ref[...]
    shift = mod_ref[pl.ds(row, 1), 0:D_MODEL]
    scale = mod_ref[pl.ds(row, 1), D_MODEL:2 * D_MODEL]
    h = (xn * (1.0 + scale) + shift).astype(BF16)
    piece_chunks = LN_PIECE // LANES
    for c0 in range(0, len(chunks), piece_chunks):
        n_sub = min(piece_chunks, len(chunks) - c0)
        yp = _dot(h, w_ref[:, c0 * LANES:(c0 + n_sub) * LANES])
        yn = [None] * n_sub
        for s0 in range(0, n_sub, 2):
            if chunks[c0 + s0][0] or chunks[c0 + s0 + 1][0]:
                y2 = yp[:, s0 * LANES:(s0 + 2) * LANES]
                gs = _dot((y2 * y2).astype(BF16), gmat_ref[...])
                yn2 = y2 * lax.rsqrt(gs * (1.0 / HEAD_DIM) + RMS_EPS)
                yn[s0], yn[s0 + 1] = yn2[:, :LANES], yn2[:, LANES:]
        for sub in range(n_sub):
            c = c0 + sub
            normed, dst, dst_col, f32_dst, f32_col = chunks[c]
            if normed:
                y = yn[sub] * hg_ref[:, c * LANES:(c + 1) * LANES]
            else:
                y = yp[:, sub * LANES:(sub + 1) * LANES]
            if f32_dst is not None:
                out_refs[f32_dst][:, f32_col:f32_col + LANES] = y
            if normed and use_rope:
                y = (y * cos_ref[...] + pltpu.roll(y, HEAD_DIM // 4, 1) * sa_ref[...]
                     + pltpu.roll(y, LANES - HEAD_DIM // 4, 1) * sb_ref[...])
            out_refs[dst][:, dst_col:dst_col + LANES] = y.astype(BF16)


def _lnproj(xa, xb, merged, mod_l, gain, w_bf16, head_gain, chunks, q_w, kv_w, f32_widths, use_rope):
    dout = w_bf16.shape[1]
    tm = TM_LN
    npt = N_PROMPT // tm
    nst = N_SAMPLE // tm
    cond_tok, rope_tok = _tile_cond(tm)
    order = np.concatenate([np.arange(npt, npt + nst), np.arange(npt)])
    tile_cond, tile_rope = cond_tok[order], rope_tok[order]
    cos, sa, sb = _rope_tables(tm)
    gmat = jnp.asarray(np.kron(np.eye(2 * LANES // HEAD_DIM), np.ones((HEAD_DIM, HEAD_DIM))), dtype=BF16)
    const = lambda i, *_: (0, 0)
    tok = lambda i, *_: (jnp.where(i < nst, i + npt, i - nst), 0)
    rope_map = lambda i, tc, tr: (tr[i], 0)
    prm = lambda i, *_: (jnp.maximum(i - nst, 0), 0)
    base = npt if merged else 0
    xa_spec = pl.BlockSpec((tm, D_MODEL), prm)
    xb_spec = pl.BlockSpec((tm, D_MODEL), lambda i, *_: (jnp.minimum(i, nst - 1) + base, 0))
    out_shape = [jax.ShapeDtypeStruct((N_TOK, q_w), BF16), jax.ShapeDtypeStruct((N_TOK, kv_w), BF16)]
    out_specs = [pl.BlockSpec((tm, q_w), tok), pl.BlockSpec((tm, kv_w), tok)]
    for wd in f32_widths:
        out_shape.append(jax.ShapeDtypeStruct((N_PROMPT, wd), F32))
        out_specs.append(pl.BlockSpec((tm, wd), prm))
    return pl.pallas_call(
        functools.partial(_lnproj_kernel, chunks=chunks, use_rope=use_rope),
        out_shape=out_shape,
        grid_spec=pltpu.PrefetchScalarGridSpec(
            num_scalar_prefetch=2,
            grid=(N_TOK // tm,),
            in_specs=[
                xa_spec, xb_spec,
                pl.BlockSpec((COND_ROWS, 6 * D_MODEL), const),
                pl.BlockSpec((1, D_MODEL), const),
                pl.BlockSpec((D_MODEL, dout), const),
                pl.BlockSpec((1, dout), const),
                pl.BlockSpec((2 * LANES, 2 * LANES), const),
                pl.BlockSpec((tm, LANES), rope_map),
                pl.BlockSpec((tm, LANES), rope_map),
                pl.BlockSpec((tm, LANES), rope_map),
            ],
            out_specs=out_specs,
        ),
        compiler_params=pltpu.CompilerParams(
            dimension_semantics=("arbitrary",), vmem_limit_bytes=VMEM_LIMIT),
        name="lnproj",
    )(tile_cond, tile_rope, xa, xb, mod_l, gain.reshape(1, D_MODEL), w_bf16, head_gain, gmat, cos, sa, sb)


def _softmax_parts(q, k_new, k_ctx, want_sum=True):
    s_n = _dot_nt(q, k_new)
    m = jnp.max(s_n, axis=-1, keepdims=True)
    if k_ctx is not None:
        s_c = _dot_nt(q, k_ctx)
        m = jnp.maximum(m, jnp.max(s_c, axis=-1, keepdims=True))
    p_n = jnp.exp2(s_n - m)
    l = jnp.sum(p_n, axis=-1, keepdims=True) if want_sum else None
    p_c = None
    if k_ctx is not None:
        p_c = jnp.exp2(s_c - m)
        if want_sum:
            l = l + jnp.sum(p_c, axis=-1, keepdims=True)
    return p_n, p_c, l


def _gqa_pairs(q_ref, q_col0, kv_ref, k_col0, v_col0, ck_ref, cv_ref, n_kv, group, tq):
    low = _low_half((tq, LANES))
    mxu_sums = group > 1
    head_out = [None] * (n_kv * group)
    for pair in range(n_kv // 2):
        lanes = slice(pair * LANES, (pair + 1) * LANES)
        k_n = kv_ref[:, k_col0 + lanes.start:k_col0 + lanes.stop]
        v_pair = kv_ref[:, v_col0 + lanes.start:v_col0 + lanes.stop]
        k_c = vc_pair = None
        if ck_ref is not None:
            k_c = ck_ref[:, lanes].astype(BF16)
            vc_pair = cv_ref[:, lanes].astype(BF16)
        for half in range(2):
            kvh = 2 * pair + half
            keep = low if half == 0 else jnp.logical_not(low)
            v_n, v_c = v_pair, vc_pair
            if mxu_sums:
                own = _low_half(v_pair.shape) if half == 0 else jnp.logical_not(_low_half(v_pair.shape))
                v_n = jnp.where(own, v_pair, jnp.ones_like(v_pair))
                if vc_pair is not None:
                    own_c = _low_half(vc_pair.shape) if half == 0 else jnp.logical_not(_low_half(vc_pair.shape))
                    v_c = jnp.where(own_c, vc_pair, jnp.ones_like(vc_pair))
            qs = []
            for g in range(group):
                head = kvh * group + g
                blk = q_ref[:, q_col0 + (head // 2) * LANES:q_col0 + (head // 2 + 1) * LANES]
                if head % 2 != half:
                    blk = _swap_halves(blk)
                qs.append(jnp.where(keep, blk, jnp.zeros_like(blk)))
            q = qs[0] if group == 1 else jnp.concatenate(qs, axis=0)
            p_n, p_c, l = _softmax_parts(q, k_n, k_c, want_sum=not mxu_sums)
            o = _dot(p_n.astype(BF16), v_n)
            if p_c is not None:
                o = o + _dot(p_c.astype(BF16), v_c)
            o = o * (1.0 / (pltpu.roll(o, HEAD_DIM, 1) if mxu_sums else l))
            for g in range(group):
                head = kvh * group + g
                og = o[g * tq:(g + 1) * tq]
                if head % 2 != half:
                    og = pltpu.roll(og, HEAD_DIM, 1)
                head_out[head] = og
    return [jnp.where(low, head_out[2 * k], head_out[2 * k + 1]) for k in range(n_kv * group // 2)]


def _attn_ab_kernel(*refs, has_cache, tq, lam_init):
    if has_cache:
        q_ref, kv_ref, cak_ref, cav_ref, cbk_ref, cbv_ref, lamv_ref, subln_ref, o_ref = refs
    else:
        q_ref, kv_ref, lamv_ref, subln_ref, o_ref = refs
        cak_ref = cav_ref = cbk_ref = cbv_ref = None
    outs = _gqa_pairs(q_ref, 0, kv_ref, 0, A_KV_W, cak_ref, cav_ref, A_KV_HEADS, A_GROUP, tq)

    lv = lamv_ref[...]
    l1 = jnp.sum(lv[0:1] * lv[1:2], axis=-1, keepdims=True)
    l2 = jnp.sum(lv[2:3] * lv[3:4], axis=-1, keepdims=True)
    lam = jnp.exp(l1) - jnp.exp(l2) + lam_init
    bk0 = 2 * A_KV_W
    bv0 = bk0 + B_QK_W
    low = _low_half((tq, LANES))
    for h in range(B_HEADS):
        lanes = slice(h * LANES, (h + 1) * LANES)
        qp = q_ref[:, A_Q_W + lanes.start:A_Q_W + lanes.stop]
        zero = jnp.zeros_like(qp)
        q = jnp.concatenate([jnp.where(low, qp, zero), jnp.where(low, zero, qp)], axis=0)
        k_n = kv_ref[:, bk0 + lanes.start:bk0 + lanes.stop]
        v_n = kv_ref[:, bv0 + lanes.start:bv0 + lanes.stop]
        k_c = v_c = None
        if has_cache:
            k_c = cbk_ref[:, lanes].astype(BF16)
            v_c = cbv_ref[:, lanes].astype(BF16)
        p_n, p_c, l = _softmax_parts(q, k_n, k_c)
        r = 1.0 / l
        r1 = r[:tq]
        r2 = lam * r[tq:]
        o = _dot((p_n[:tq] * r1 - p_n[tq:] * r2).astype(BF16), v_n)
        if has_cache:
            o = o + _dot((p_c[:tq] * r1 - p_c[tq:] * r2).astype(BF16), v_c)
        ms = jnp.mean(o * o, axis=-1, keepdims=True)
        o = o * lax.rsqrt(ms + RMS_EPS) * subln_ref[...] * (1.0 - lam_init)
        outs.append(o)
    o_ref[...] = jnp.concatenate(outs, axis=1).astype(BF16)


def _attn_ab(q, kv, lamv, subln, lam_init, caches):
    kv_w = kv.shape[1]
    kern = functools.partial(_attn_ab_kernel, lam_init=lam_init)
    const = lambda b, j: (0, 0)
    cp = pltpu.CompilerParams(dimension_semantics=("arbitrary", "arbitrary"), vmem_limit_bytes=VMEM_LIMIT)
    if caches is None:
        tq = SEQ
        return pl.pallas_call(
            functools.partial(kern, has_cache=False, tq=tq),
            out_shape=jax.ShapeDtypeStruct((N_PROMPT, D_MODEL), BF16),
            grid=(BATCH, 1),
            in_specs=[
                pl.BlockSpec((tq, D_MODEL), lambda b, j: (b, 0)),
                pl.BlockSpec((SEQ, kv_w), lambda b, j: (b, 0)),
                pl.BlockSpec((4, HEAD_DIM), const),
                pl.BlockSpec((1, B_V_DIM), const),
            ],
            out_specs=pl.BlockSpec((tq, D_MODEL), lambda b, j: (b, 0)),
            compiler_params=cp,
            name="attn_ab_prompt",
        )(q, kv, lamv, subln)
    tq = 256
    nq = DEC_SEQ // tq
    q0 = N_PROMPT // tq
    kv0 = N_PROMPT // DEC_SEQ
    cak, cav, cbk, cbv = caches
    cspec = lambda w: pl.BlockSpec((None, PAST_LEN, w), lambda b, j: (b, 0, 0))
    return pl.pallas_call(
        functools.partial(kern, has_cache=True, tq=tq),
        out_shape=jax.ShapeDtypeStruct((N_SAMPLE, D_MODEL), BF16),
        grid=(DEC_BATCH, nq),
        in_specs=[
            pl.BlockSpec((tq, D_MODEL), lambda b, j: (q0 + b * nq + j, 0)),
            pl.BlockSpec((DEC_SEQ, kv_w), lambda b, j: (kv0 + b, 0)),
            cspec(A_KV_W), cspec(A_KV_W), cspec(B_QK_W), cspec(B_V_W),
            pl.BlockSpec((4, HEAD_DIM), const),
            pl.BlockSpec((1, B_V_DIM), const),
        ],
        out_specs=pl.BlockSpec((tq, D_MODEL), lambda b, j: (b * nq + j, 0)),
        compiler_params=cp,
        name="attn_ab_sample",
    )(q, kv, cak, cav, cbk, cbv, lamv, subln)


def _attn_c_prompt_kernel(q_ref, kv_ref, o_ref):
    outs = _gqa_pairs(q_ref, 0, kv_ref, 0, C_WIDTH, None, None, C_HEADS, 1, SEQ)
    o_ref[...] = jnp.concatenate(outs, axis=1).astype(BF16)


def _attn_c_prompt(q, kv):
    return pl.pallas_call(
        _attn_c_prompt_kernel,
        out_shape=jax.ShapeDtypeStruct((N_PROMPT, D_MODEL), BF16),
        grid=(BATCH,),
        in_specs=[
            pl.BlockSpec((SEQ, C_WIDTH), lambda b: (b, 0)),
            pl.BlockSpec((SEQ, 2 * C_WIDTH), lambda b: (b, 0)),
        ],
        out_specs=pl.BlockSpec((SEQ, C_WIDTH), lambda b: (b, 0)),
        compiler_params=pltpu.CompilerParams(dimension_semantics=("arbitrary",), vmem_limit_bytes=VMEM_LIMIT),
        name="attn_c_prompt",
    )(q, kv)


def _na_kernel(q_ref, kv_ref, ck_ref, cv_ref, tp_ref, o_ref, ckb, cvb):
    step = pl.program_id(1)

    @pl.when(step == 0)
    def _():
        ckb[...] = ck_ref[...].astype(BF16)
        cvb[...] = cv_ref[...].astype(BF16)

    kh = min(NA_ROWS, GRID_ROWS)
    win = kh * GRID_W
    low = _low_half((GRID_W, LANES))
    n_pair = C_HEADS // 2
    for rr in range(NA_ROWS_PER_STEP):
        r = step * NA_ROWS_PER_STEP + rr
        rows = slice(rr * GRID_W, (rr + 1) * GRID_W)
        rs = jnp.clip(r - kh // 2, 0, GRID_ROWS - kh)
        ro0 = rs - r + (NA_ROWS - 1)
        start = pl.multiple_of(rs * GRID_W, GRID_W)
        s_lat, s_ctx = [], []
        for j in range(n_pair):
            lanes = slice(j * LANES, (j + 1) * LANES)
            qp = q_ref[rows, lanes]
            zero = jnp.zeros_like(qp)
            q = jnp.concatenate([jnp.where(low, qp, zero), jnp.where(low, zero, qp)], axis=0)
            kw = kv_ref[pl.ds(start, win), lanes]
            bias = jnp.concatenate(
                [jnp.concatenate([tp_ref[2 * j + hh, ro0 + 2 * t] for t in range(kh // 2)], axis=1)
                 for hh in range(2)], axis=0)
            s_lat.append(_dot_nt(q, kw) + bias)
            s_ctx.append(_dot_nt(q, ckb[:, lanes]))
        outs = []
        for j in range(n_pair):
            lanes = slice(j * LANES, (j + 1) * LANES)
            s_l, s_c = s_lat[j], s_ctx[j]
            m = jnp.maximum(jnp.max(s_l, axis=-1, keepdims=True), jnp.max(s_c, axis=-1, keepdims=True))
            p_l = jnp.exp2(s_l - m)
            p_c = jnp.exp2(s_c - m)
            l = jnp.sum(p_l, axis=-1, keepdims=True) + jnp.sum(p_c, axis=-1, keepdims=True)
            vw = kv_ref[pl.ds(start, win), C_WIDTH + lanes.start:C_WIDTH + lanes.stop]
            o = _dot(p_c.astype(BF16), cvb[:, lanes]) + _dot(p_l.astype(BF16), vw)
            o = o * (1.0 / l)
            outs.append(jnp.where(low, o[:GRID_W], o[GRID_W:]))
        o_ref[rows, :] = jnp.concatenate(outs, axis=1).astype(BF16)


NA_BIAS_PAD = GRID_W - NA_COLS


def _na_bias_kernel(w_ref, mask_ref, tp_ref):
    low = _low_half((GRID_W, LANES))
    keep = mask_ref[...] > 0.0
    n_off = 2 * NA_ROWS - 1
    left, right = [], []
    for ro in range(n_off):
        row = jnp.broadcast_to(w_ref[ro:ro + 1, :] * LOG2E, (GRID_W, LANES))
        left.append(pltpu.roll(row, LANES - GRID_W + 1, 1, stride=1, stride_axis=0))
        right.append(pltpu.roll(row, 1, 1, stride=1, stride_axis=0))
    for t in range(n_off - 1):
        tp_ref[t] = jnp.where(keep, jnp.where(low, left[t], right[t + 1]), NEG_BIG)


def _na_bias_table(rpb):
    cols = np.arange(GRID_W)
    col_start = np.clip(cols - NA_COLS // 2, 0, GRID_W - NA_COLS)
    col_in = (cols[None, :] >= col_start[:, None]) & (cols[None, :] < col_start[:, None] + NA_COLS)
    assert np.abs((cols[None, :] - cols[:, None])[col_in]).max() <= NA_COLS - 1
    mask = jnp.asarray(np.tile(col_in, (1, LANES // GRID_W)), dtype=F32)
    n_off = 2 * NA_ROWS - 1
    n_rel = 2 * NA_COLS - 1
    w = jnp.pad(rpb.astype(F32), ((0, 0), (0, 0), (NA_BIAS_PAD, LANES - NA_BIAS_PAD - n_rel)))
    return pl.pallas_call(
        _na_bias_kernel,
        out_shape=jax.ShapeDtypeStruct((C_HEADS, n_off - 1, GRID_W, LANES), F32),
        grid=(C_HEADS,),
        in_specs=[
            pl.BlockSpec((None, n_off, LANES), lambda h: (h, 0, 0)),
            pl.BlockSpec((GRID_W, LANES), lambda h: (0, 0)),
        ],
        out_specs=pl.BlockSpec((None, n_off - 1, GRID_W, LANES), lambda h: (h, 0, 0, 0)),
        compiler_params=pltpu.CompilerParams(dimension_semantics=("arbitrary",)),
        name="na_bias",
    )(w, mask)


def _attn_na(q, kv, ck, cv, tp):
    tq = NA_ROWS_PER_STEP * GRID_W
    steps = GRID_ROWS // NA_ROWS_PER_STEP
    q0 = N_PROMPT // tq
    kv0 = N_PROMPT // DEC_SEQ
    return pl.pallas_call(
        _na_kernel,
        out_shape=jax.ShapeDtypeStruct((N_SAMPLE, D_MODEL), BF16),
        grid=(DEC_BATCH, steps),
        in_specs=[
            pl.BlockSpec((tq, C_WIDTH), lambda b, r: (q0 + b * steps + r, 0)),
            pl.BlockSpec((DEC_SEQ, 2 * C_WIDTH), lambda b, r: (kv0 + b, 0)),
            pl.BlockSpec((None, PAST_LEN, C_WIDTH), lambda b, r: (b, 0, 0)),
            pl.BlockSpec((None, PAST_LEN, C_WIDTH), lambda b, r: (b, 0, 0)),
            pl.BlockSpec(tp.shape, lambda b, r: (0, 0, 0, 0)),
        ],
        out_specs=pl.BlockSpec((tq, C_WIDTH), lambda b, r: (b * steps + r, 0)),
        scratch_shapes=[pltpu.VMEM((PAST_LEN, C_WIDTH), BF16), pltpu.VMEM((PAST_LEN, C_WIDTH), BF16)],
        compiler_params=pltpu.CompilerParams(
            dimension_semantics=("arbitrary", "arbitrary"), vmem_limit_bytes=VMEM_LIMIT),
        name="attn_na_sample",
    )(q, kv, ck, cv, tp)


def _first_wins_ranks(vals):
    ranks = []
    for i in range(len(vals)):
        r = jnp.zeros_like(vals[i])
        for j in range(len(vals)):
            if j == i:
                continue
            beats = (vals[j] >= vals[i]) if j < i else (vals[j] > vals[i])
            r = r + jnp.where(beats, 1.0, 0.0)
        ranks.append(r)
    return ranks


def _outproj_kernel(tile_cond_ref, op_ref, os_ref, xa_ref, xb_ref, w_ref, mod_ref, g2_ref, wrt_ref, br_ref,
                    tri_ref, x1_ref, h2_ref, gates_ref, rank_ref, carry_ref):
    i = pl.program_id(0)
    row = tile_cond_ref[i]
    is_prompt = i < N_PROMPT // TM_OUT

    @pl.when(i == 0)
    def _():
        carry_ref[...] = jnp.zeros_like(carry_ref)

    o = jnp.where(is_prompt, op_ref[...], os_ref[...])
    x = jnp.where(is_prompt, xa_ref[...], xb_ref[...])
    acc = _dot(o, w_ref[...])
    gate = mod_ref[pl.ds(row, 1), 2 * D_MODEL:3 * D_MODEL]
    x1 = x + gate * acc
    x1_ref[...] = x1
    ms = jnp.mean(x1 * x1, axis=-1, keepdims=True)
    xn = x1 * lax.rsqrt(ms + RMS_EPS) * g2_ref[...]
    shift = mod_ref[pl.ds(row, 1), 3 * D_MODEL:4 * D_MODEL]
    scale = mod_ref[pl.ds(row, 1), 4 * D_MODEL:5 * D_MODEL]
    h2 = xn * (1.0 + scale) + shift
    for s in range(ROW_TILES):
        h2_ref[pl.ds(s, TM_OUT, stride=ROW_TILES), :] = h2[:, s * LANES:(s + 1) * LANES]

    h_hi = h2.astype(BF16)
    h_lo = (h2 - h_hi.astype(F32)).astype(BF16)
    part = _dot_nt(wrt_ref[...], h_hi)
    logits = part[:N_EXPERTS] + part[N_EXPERTS:] + _dot_nt(wrt_ref[:N_EXPERTS, :], h_lo)
    e = jnp.exp(logits - jnp.max(logits, axis=0, keepdims=True))
    scores = e * (1.0 / jnp.sum(e, axis=0, keepdims=True))
    sel = scores + br_ref[...]
    sel_rows = [sel[k:k + 1, :] for k in range(N_EXPERTS)]
    in_top2 = []
    group_sum = []
    for g in range(N_GROUPS):
        vals = sel_rows[g * EXPERTS_PER_GROUP:(g + 1) * EXPERTS_PER_GROUP]
        ranks = _first_wins_ranks(vals)
        top = [rk < 2.0 for rk in ranks]
        in_top2.extend(top)
        s = jnp.zeros_like(vals[0])
        for v, t in zip(vals, top):
            s = s + jnp.where(t, v, 0.0)
        group_sum.append(s)
    group_rank = _first_wins_ranks(group_sum)
    mask_rows = []
    for k in range(N_EXPERTS):
        chosen = jnp.where(in_top2[k], 1.0, 0.0) * jnp.where(group_rank[k // EXPERTS_PER_GROUP] < 1.0, 1.0, 0.0)
        mask_rows.append(chosen)
    mask = jnp.concatenate(mask_rows, axis=0)
    picked = scores * mask
    gates_ref[...] = picked * (1.0 / jnp.sum(picked, axis=0, keepdims=True))
    prefix = _dot(mask.astype(BF16), tri_ref[...])
    rank_ref[...] = jnp.where(mask > 0.0, prefix + carry_ref[...], -1.0)
    carry_ref[...] = carry_ref[...] + jnp.sum(mask, axis=1, keepdims=True)


def _outproj(o_prompt, o_sample, xa, xb, merged, w_bf16, mod_l, gain2, w_router, b_router):
    const = lambda i, *_: (0, 0)
    tok = lambda i, *_: (i, 0)
    tokT = lambda i, *_: (0, i)
    tm = TM_OUT
    tile_cond, _ = _tile_cond(tm)
    op_spec, os_spec = _split_specs(tm, False)
    xa_spec, xb_spec = _split_specs(tm, merged)
    tri = jnp.asarray(np.triu(np.ones((tm, tm)), k=1), dtype=BF16)
    wrt = w_router.T.astype(F32)
    wrt_hi = wrt.astype(BF16)
    wrt_split = jnp.concatenate([wrt_hi, (wrt - wrt_hi.astype(F32)).astype(BF16)], axis=0)
    return pl.pallas_call(
        _outproj_kernel,
        out_shape=[
            jax.ShapeDtypeStruct((N_TOK, D_MODEL), F32),
            jax.ShapeDtypeStruct((N_TOK * ROW_TILES, LANES), F32),
            jax.ShapeDtypeStruct((N_EXPERTS, N_TOK), F32),
            jax.ShapeDtypeStruct((N_EXPERTS, N_TOK), F32),
        ],
        grid_spec=pltpu.PrefetchScalarGridSpec(
            num_scalar_prefetch=1,
            grid=(N_TOK // tm,),
            in_specs=[
                op_spec, os_spec, xa_spec, xb_spec,
                pl.BlockSpec((D_MODEL, D_MODEL), const),
                pl.BlockSpec((COND_ROWS, 6 * D_MODEL), const),
                pl.BlockSpec((1, D_MODEL), const),
                pl.BlockSpec((2 * N_EXPERTS, D_MODEL), const),
                pl.BlockSpec((N_EXPERTS, 1), const),
                pl.BlockSpec((tm, tm), const),
            ],
            out_specs=[
                pl.BlockSpec((tm, D_MODEL), tok),
                pl.BlockSpec((tm * ROW_TILES, LANES), tok),
                pl.BlockSpec((N_EXPERTS, tm), tokT),
                pl.BlockSpec((N_EXPERTS, tm), tokT),
            ],
            scratch_shapes=[pltpu.VMEM((N_EXPERTS, 1), F32)],
        ),
        compiler_params=pltpu.CompilerParams(
            dimension_semantics=("arbitrary",), vmem_limit_bytes=VMEM_LIMIT),
        name="outproj_router",
    )(tile_cond, o_prompt, o_sample, xa, xb, w_bf16, mod_l, gain2.reshape(1, D_MODEL), wrt_split,
      b_router.reshape(N_EXPERTS, 1), tri)


def _plan_kernel(gates_ref, rank_ref, dests_ref, wts_ref, last_ref, misc_ref):
    rank = rank_ref[...]
    gates = gates_ref[...]
    sel = rank >= 0.0
    hit = jnp.where(sel, 1.0, 0.0)
    tile = float(MOE_TM)

    def pad_up(c):
        return jnp.floor((c + (tile - 1.0)) * (1.0 / tile)) * tile

    counts_col = jnp.sum(hit, axis=1, keepdims=True)
    ones = jnp.ones((SUBLANES, hit.shape[1]), BF16)
    counts_row = _dot_nt(ones, hit.astype(BF16))[0:1]
    padded_col = pad_up(counts_col)
    padded_row = pad_up(counts_row)
    e_sub = lax.broadcasted_iota(jnp.int32, (N_EXPERTS, N_EXPERTS), 0)
    e_lane = lax.broadcasted_iota(jnp.int32, (N_EXPERTS, N_EXPERTS), 1)
    ends_col = jnp.sum(jnp.where(e_lane <= e_sub, padded_row, 0.0), axis=1, keepdims=True)
    ends_row = jnp.sum(jnp.where(e_sub <= e_lane, padded_col, 0.0), axis=0, keepdims=True)
    dest = jnp.where(sel, ends_col - padded_col + rank, -1.0)
    d1 = jnp.max(dest, axis=0, keepdims=True)
    d0 = jnp.min(jnp.where(sel, dest, float(MOE_ROWS)), axis=0, keepdims=True)
    w0 = jnp.sum(jnp.where(dest == d0, gates, 0.0), axis=0, keepdims=True)
    w1 = jnp.sum(jnp.where(dest == d1, gates, 0.0), axis=0, keepdims=True)
    dests_ref[...] = (jnp.concatenate([d0, d1], axis=0) * float(ROW_TILES)).astype(jnp.int32)
    wts_ref[...] = jnp.concatenate([w0, w1], axis=0)

    last = jnp.where(padded_col > 0.0, (ends_col - tile) * float(ROW_TILES), -1.0)
    last_ref[...] = jnp.broadcast_to(last, last_ref.shape).astype(jnp.int32)

    lane = lax.broadcasted_iota(jnp.int32, (1, LANES), 1).astype(F32)
    n_tiles = ends_row[:, N_EXPERTS - 1:N_EXPERTS] * (1.0 / tile)
    tile_expert = jnp.sum(jnp.where(lane * tile >= ends_col, 1.0, 0.0), axis=0, keepdims=True)
    tile_expert = jnp.minimum(tile_expert, float(N_EXPERTS - 1))
    e_col = lax.broadcasted_iota(jnp.int32, (N_EXPERTS, 1), 0).astype(F32)
    later = jnp.where((e_col > tile_expert) & (padded_col > 0.0), e_col, float(N_EXPERTS))
    nxt = jnp.min(later, axis=0, keepdims=True)
    nxt = jnp.where(nxt > float(N_EXPERTS - 1), tile_expert, nxt)
    spare = n_tiles + lane
    spare = jnp.where(spare < float(MOE_TILES), spare * (tile * ROW_TILES), -1.0)
    zero = jnp.zeros((1, LANES), F32)
    rows = [tile_expert, nxt, spare, n_tiles + zero] + [zero] * (SUBLANES - 4)
    misc_ref[...] = jnp.concatenate(rows, axis=0).astype(jnp.int32)


def _route_plan(gates, rank):
    assert MOE_TILES <= LANES and N_EXPERTS <= LANES
    full = lambda shape: pl.BlockSpec(shape, lambda: (0,) * len(shape))
    dests, wts, last, misc = pl.pallas_call(
        _plan_kernel,
        out_shape=[
            jax.ShapeDtypeStruct((2, N_TOK), jnp.int32),
            jax.ShapeDtypeStruct((2, N_TOK), F32),
            jax.ShapeDtypeStruct((N_EXPERTS, LANES), jnp.int32),
            jax.ShapeDtypeStruct((SUBLANES, LANES), jnp.int32),
        ],
        in_specs=[full((N_EXPERTS, N_TOK)), full((N_EXPERTS, N_TOK))],
        out_specs=[full((2, N_TOK)), full((2, N_TOK)), full((N_EXPERTS, LANES)), full((SUBLANES, LANES))],
        compiler_params=pltpu.CompilerParams(vmem_limit_bytes=VMEM_LIMIT),
        name="route_plan",
    )(gates, rank)
    tile_expert = misc[0, :MOE_TILES]
    next_expert = misc[1, :MOE_TILES]
    n_tiles = misc[3, :1]
    clear = jnp.concatenate([last[:, 0], misc[2, :N_EXPERTS]])
    return dests.reshape(2 * N_TOK), wts.T, tile_expert, n_tiles, next_expert, clear


def _scatter_kernel(d_ref, last_ref, h_ref, xs_ref, zeros, sem, zsem):
    i = pl.program_id(0)
    base = i * TM
    rows = MOE_TM * ROW_TILES

    @pl.when(i == 0)
    def _():
        zeros[...] = jnp.zeros_like(zeros)
        for e in range(2 * N_EXPERTS):
            @pl.when(last_ref[e] >= 0)
            def _():
                at = pl.multiple_of(last_ref[e], SUBLANES)
                pltpu.make_async_copy(zeros, xs_ref.at[pl.ds(at, rows)], zsem).start()
        for e in range(2 * N_EXPERTS):
            @pl.when(last_ref[e] >= 0)
            def _():
                pltpu.make_async_copy(zeros, xs_ref.at[pl.ds(0, rows)], zsem).wait()

    def copy(r, t):
        src = h_ref.at[pl.ds(pl.multiple_of(r * ROW_TILES, SUBLANES), ROW_TILES)]
        dst = xs_ref.at[pl.ds(pl.multiple_of(t, SUBLANES), ROW_TILES)]
        return pltpu.make_async_copy(src, dst, sem)

    def issue(r, c):
        copy(r, d_ref[base + r]).start(priority=0)
        copy(r, d_ref[N_TOK + base + r]).start(priority=1)
        return c

    lax.fori_loop(0, TM, issue, 0, unroll=8)

    def drain(r, c):
        copy(r, 0).wait()
        copy(r, 0).wait()
        return c

    lax.fori_loop(0, TM, drain, 0, unroll=8)


def _moe_scatter(dests, last_tile, h2):
    return pl.pallas_call(
        _scatter_kernel,
        out_shape=jax.ShapeDtypeStruct((MOE_ROWS * ROW_TILES, LANES), F32),
        grid_spec=pltpu.PrefetchScalarGridSpec(
            num_scalar_prefetch=2,
            grid=(N_TILES,),
            in_specs=[pl.BlockSpec((TM * ROW_TILES, LANES), lambda i, *_: (i, 0))],
            out_specs=pl.BlockSpec(memory_space=pl.ANY),
            scratch_shapes=[
                pltpu.VMEM((MOE_TM * ROW_TILES, LANES), F32),
                pltpu.SemaphoreType.DMA,
                pltpu.SemaphoreType.DMA,
            ],
        ),
        compiler_params=pltpu.CompilerParams(
            dimension_semantics=("arbitrary",), vmem_limit_bytes=VMEM_LIMIT, has_side_effects=True),
        name="moe_scatter",
    )(dests, last_tile, h2)


def _mlp_kernel(te_ref, nv_ref, nxt_ref, xs_ref, wg_hbm, wu_hbm, wd_hbm, y_ref, stage, wgb, wub, wdb, hb, sem,
                *, layer):
    t = pl.program_id(0)

    def fetch(e):
        return [pltpu.make_async_copy(w.at[layer, e], stage.at[k], sem.at[k])
                for k, w in enumerate((wg_hbm, wu_hbm, wd_hbm))]

    @pl.when(t < nv_ref[0])
    def _():
        e = te_ref[t]
        prev = te_ref[jnp.maximum(t - 1, 0)]

        @pl.when(t == 0)
        def _():
            for cp in fetch(e):
                cp.start()

        @pl.when((t == 0) | (e != prev))
        def _():
            for cp in fetch(e):
                cp.wait()
            wgb[...] = stage[0].astype(BF16)
            wub[...] = stage[1].astype(BF16)
            wdb[...] = stage[2].astype(BF16)

            @pl.when(nxt_ref[t] != e)
            def _():
                for cp in fetch(nxt_ref[t]):
                    cp.start()

        x = jnp.concatenate(
            [xs_ref[pl.ds(s, MOE_TM, stride=ROW_TILES), :] for s in range(ROW_TILES)], axis=1).astype(BF16)
        step = 512
        for c in range(0, D_EXPERT, step):
            g = _dot(x, wgb[:, c:c + step])
            u = _dot(x, wub[:, c:c + step])
            hb[:, c:c + step] = (_silu(g) * u).astype(BF16)
        y = _dot(hb[...], wdb[...])
        for s in range(ROW_TILES):
            y_ref[pl.ds(s, MOE_TM, stride=ROW_TILES), :] = y[:, s * LANES:(s + 1) * LANES]

    @pl.when(t >= nv_ref[0])
    def _():
        y_ref[...] = jnp.zeros_like(y_ref)


def _moe_mlp(layer, tile_expert, n_tiles, next_expert, xs, w_gate, w_up, w_down):
    assert D_MODEL == D_EXPERT

    def tile_map(t, te, nv, nx):
        return (jnp.minimum(t, nv[0] - 1), 0)

    def out_map(t, te, nv, nx):
        return (t, 0)

    hbm = pl.BlockSpec(memory_space=pl.ANY)
    return pl.pallas_call(
        functools.partial(_mlp_kernel, layer=layer),
        out_shape=jax.ShapeDtypeStruct((MOE_ROWS * ROW_TILES, LANES), F32),
        grid_spec=pltpu.PrefetchScalarGridSpec(
            num_scalar_prefetch=3,
            grid=(MOE_TILES,),
            in_specs=[pl.BlockSpec((MOE_TM * ROW_TILES, LANES), tile_map), hbm, hbm, hbm],
            out_specs=pl.BlockSpec((MOE_TM * ROW_TILES, LANES), out_map),
            scratch_shapes=[
                pltpu.VMEM((3, D_MODEL, D_EXPERT), F32),
                pltpu.VMEM((D_MODEL, D_EXPERT), BF16),
                pltpu.VMEM((D_MODEL, D_EXPERT), BF16),
                pltpu.VMEM((D_EXPERT, D_MODEL), BF16),
                pltpu.VMEM((MOE_TM, D_EXPERT), BF16),
                pltpu.SemaphoreType.DMA((3,)),
            ],
        ),
        compiler_params=pltpu.CompilerParams(
            dimension_semantics=("arbitrary",), vmem_limit_bytes=VMEM_LIMIT),
        name="moe_mlp",
    )(tile_expert, n_tiles, next_expert, xs, w_gate, w_up, w_down)


def _combine_kernel(d_ref, tile_cond_ref, x1_ref, w_ref, mod_ref, y_ref, *rest, split_out):
    if split_out:
        outp_ref, outs_ref, buf, sem = rest
    else:
        out_ref, buf, sem = rest
    i = pl.program_id(0)
    row = tile_cond_ref[i]
    slot = i % 2

    def copy(sl, k, r, t):
        src = y_ref.at[pl.ds(pl.multiple_of(t, SUBLANES), ROW_TILES)]
        dst = buf.at[sl, k, pl.ds(pl.multiple_of(r * ROW_TILES, SUBLANES), ROW_TILES)]
        return pltpu.make_async_copy(src, dst, sem.at[sl])

    def gather_tile(tile, sl):
        base = tile * TM

        def issue(r, c):
            copy(sl, 0, r, d_ref[base + r]).start(priority=0)
            copy(sl, 1, r, d_ref[N_TOK + base + r]).start(priority=1)
            return c

        lax.fori_loop(0, TM, issue, 0, unroll=8)

    @pl.when(i == 0)
    def _():
        gather_tile(0, 0)

    @pl.when(i + 1 < N_TILES)
    def _():
        gather_tile(i + 1, 1 - slot)

    def drain(r, c):
        copy(slot, 0, r, 0).wait()
        copy(slot, 1, r, 0).wait()
        return c

    lax.fori_loop(0, TM, drain, 0, unroll=8)
    w = w_ref[...]
    w0 = w[:, 0:1]
    w1 = w[:, 1:2]
    parts = []
    for s in range(ROW_TILES):
        y0 = buf[slot, 0, pl.ds(s, TM, stride=ROW_TILES), :]
        y1 = buf[slot, 1, pl.ds(s, TM, stride=ROW_TILES), :]
        parts.append(w0 * y0 + w1 * y1)
    gate = mod_ref[pl.ds(row, 1), 5 * D_MODEL:6 * D_MODEL]
    out = x1_ref[...] + gate * jnp.concatenate(parts, axis=1)
    if split_out:
        @pl.when(i < N_PROMPT_TILES)
        def _():
            outp_ref[...] = out

        @pl.when(i >= N_PROMPT_TILES)
        def _():
            outs_ref[...] = out
    else:
        out_ref[...] = out


def _moe_combine(dests, x1, weights, mod_l, y, split_out):
    tile_cond, _ = _tile_cond(TM)
    if split_out:
        out_shape = [jax.ShapeDtypeStruct((N_PROMPT, D_MODEL), F32), jax.ShapeDtypeStruct((N_SAMPLE, D_MODEL), F32)]
        out_specs = list(_split_specs(TM, False))
    else:
        out_shape = jax.ShapeDtypeStruct((N_TOK, D_MODEL), F32)
        out_specs = pl.BlockSpec((TM, D_MODEL), lambda i, *_: (i, 0))
    return pl.pallas_call(
        functools.partial(_combine_kernel, split_out=split_out),
        out_shape=out_shape,
        grid_spec=pltpu.PrefetchScalarGridSpec(
            num_scalar_prefetch=2,
            grid=(N_TILES,),
            in_specs=[
                pl.BlockSpec((TM, D_MODEL), lambda i, *_: (i, 0)),
                pl.BlockSpec((TM, 2), lambda i, *_: (i, 0)),
                pl.BlockSpec((COND_ROWS, 6 * D_MODEL), lambda i, *_: (0, 0)),
                pl.BlockSpec(memory_space=pl.ANY),
            ],
            out_specs=out_specs,
            scratch_shapes=[pltpu.VMEM((2, 2, TM * ROW_TILES, LANES), F32), pltpu.SemaphoreType.DMA((2,))],
        ),
        compiler_params=pltpu.CompilerParams(
            dimension_semantics=("arbitrary",), vmem_limit_bytes=VMEM_LIMIT),
        name="moe_combine",
    )(dests, tile_cond, x1, weights, mod_l, y)


def _chunk_plan(segments):
    chunks = []
    for width, normed, dst, dst_col0, f32_dst in segments:
        for k in range(width // LANES):
            chunks.append((normed, dst, dst_col0 + k * LANES, f32_dst, k * LANES))
    return tuple(chunks)


def _head_gain(parts):
    cols = []
    for width, g, mult in parts:
        if g is None:
            cols.append(jnp.ones((width,), F32))
        else:
            cols.append(jnp.tile(g.astype(F32) * mult, width // HEAD_DIM))
    return jnp.concatenate(cols).reshape(1, -1)


def kernel(x_prompt, x_sample, cache_a_k, cache_a_v, cache_b_k, cache_b_v, cache_c_k, cache_c_v, c, c_ctx, w_mod, b_mod, norm_mix, norm_ffn, w_in_ab, w_out_ab, a_q_norm, a_k_norm, b_q_norm, b_k_norm, lam_q1, lam_k1, lam_q2, lam_k2, b_subln, w_in_c, w_out_c, c_q_norm, c_k_norm, c_rpb, w_router, b_router, w_gate, w_up, w_down):
    scale = LOG2E * HEAD_DIM ** -0.5
    cond = jnp.concatenate(
        [c, c_ctx[None, :], jnp.zeros((COND_ROWS - DEC_BATCH - 1, D_MODEL), F32)], axis=0)
    mod = _modulation(cond, w_mod, b_mod)

    xa = x_prompt.reshape(N_PROMPT, D_MODEL)
    xb = x_sample.reshape(N_SAMPLE, D_MODEL)
    merged = False
    new_caches = []
    for l in range(DEPTH):
        mod_l = mod[l]
        if l % 2 == 0:
            e = l // 2
            lam_init = 0.8 - 0.6 * math.exp(-0.3 * l)
            segments = (
                (A_Q_W, True, 0, 0, None),
                (A_KV_W, True, 1, 0, 2),
                (A_KV_W, False, 1, A_KV_W, 3),
                (B_QK_W, True, 0, A_Q_W, None),
                (B_QK_W, True, 1, 2 * A_KV_W, 4),
                (B_V_W, False, 1, 2 * A_KV_W + B_QK_W, 5),
            )
            hg = _head_gain((
                (A_Q_W, a_q_norm[e], scale), (A_KV_W, a_k_norm[e], 1.0), (A_KV_W, None, 1.0),
                (B_QK_W, b_q_norm[e], scale), (B_QK_W, b_k_norm[e], 1.0), (B_V_W, None, 1.0)))
            outs = _lnproj(xa, xb, merged, mod_l, norm_mix[l], w_in_ab[e].astype(BF16), hg, _chunk_plan(segments),
                           A_Q_W + B_QK_W, 2 * A_KV_W + B_QK_W + B_V_W,
                           (A_KV_W, A_KV_W, B_QK_W, B_V_W), True)
            q, kv, ak, av, bk, bv = outs
            new_caches.append((
                ak.reshape(BATCH, SEQ, A_KV_HEADS, HEAD_DIM), av.reshape(BATCH, SEQ, A_KV_HEADS, HEAD_DIM),
                bk.reshape(BATCH, SEQ, B_HEADS, 2, HEAD_DIM), bv.reshape(BATCH, SEQ, B_HEADS, B_V_DIM)))
            lamv = jnp.stack([lam_q1[e], lam_k1[e], lam_q2[e], lam_k2[e]]).astype(F32)
            subln = b_subln[e].reshape(1, B_V_DIM)
            o_p = _attn_ab(q, kv, lamv, subln, lam_init, None)
            caches = (cache_a_k[:, e].reshape(DEC_BATCH, PAST_LEN, A_KV_W),
                      cache_a_v[:, e].reshape(DEC_BATCH, PAST_LEN, A_KV_W),
                      cache_b_k[:, e].reshape(DEC_BATCH, PAST_LEN, B_QK_W),
                      cache_b_v[:, e].reshape(DEC_BATCH, PAST_LEN, B_V_W))
            o_s = _attn_ab(q, kv, lamv, subln, lam_init, caches)
            w_out = w_out_ab[e].astype(BF16)
        else:
            oi = l // 2
            segments = (
                (C_WIDTH, True, 0, 0, None),
                (C_WIDTH, True, 1, 0, 2),
                (C_WIDTH, False, 1, C_WIDTH, 3),
            )
            hg = _head_gain(((C_WIDTH, c_q_norm[oi], scale), (C_WIDTH, c_k_norm[oi], 1.0), (C_WIDTH, None, 1.0)))
            outs = _lnproj(xa, xb, merged, mod_l, norm_mix[l], w_in_c[oi].astype(BF16), hg, _chunk_plan(segments),
                           C_WIDTH, 2 * C_WIDTH, (C_WIDTH, C_WIDTH), False)
            q, kv, ck_new, cv_new = outs
            new_caches.append((ck_new.reshape(BATCH, SEQ, C_HEADS, HEAD_DIM),
                               cv_new.reshape(BATCH, SEQ, C_HEADS, HEAD_DIM)))
            o_p = _attn_c_prompt(q, kv)
            o_s = _attn_na(q, kv,
                           cache_c_k[:, oi].reshape(DEC_BATCH, PAST_LEN, C_WIDTH),
                           cache_c_v[:, oi].reshape(DEC_BATCH, PAST_LEN, C_WIDTH),
                           _na_bias_table(c_rpb[oi]))
            w_out = w_out_c[oi].astype(BF16)

        x1, h2, gates, rank = _outproj(o_p, o_s, xa, xb, merged, w_out, mod_l, norm_ffn[l], w_router, b_router)
        dests, weights, tile_expert, n_tiles, next_expert, clear = _route_plan(gates, rank)
        xs = _moe_scatter(dests, clear, h2)
        y = _moe_mlp(l, tile_expert, n_tiles, next_expert, xs, w_gate, w_up, w_down)
        last = l == DEPTH - 1
        out = _moe_combine(dests, x1, weights, mod_l, y, last)
        if last:
            y_prompt, y_sample = out
        else:
            xa = xb = out
            merged = True

    even = [nc for i, nc in enumerate(new_caches) if i % 2 == 0]
    odd = [nc for i, nc in enumerate(new_caches) if i % 2 == 1]
    stack = lambda items, k: jnp.stack([it[k] for it in items], axis=1)
    return (y_prompt.reshape(BATCH, SEQ, D_MODEL), y_sample.reshape(DEC_BATCH, DEC_SEQ, D_MODEL),
            stack(even, 0), stack(even, 1), stack(even, 2), stack(even, 3),
            stack(odd, 0), stack(odd, 1))
```

```python
import functools
import math

import numpy as np
import jax
import jax.numpy as jnp
from jax import lax
from jax.experimental import pallas as pl
from jax.experimental.pallas import tpu as pltpu

F32 = jnp.float32
BF16 = jnp.bfloat16

D_MODEL = 1024
BATCH = 16
SEQ = 256
DEPTH = 2
DEC_BATCH = 8
DEC_SEQ = 1024
PAST_LEN = 256
GRID_W = 64
HEAD_DIM = 64
ROPE_THETA = 10000.0
RMS_EPS = 1e-6
A_Q_HEADS = 8
A_KV_HEADS = 2
A_GROUP = A_Q_HEADS // A_KV_HEADS
B_HEADS = 4
B_V_DIM = 2 * HEAD_DIM
A_Q_W = A_Q_HEADS * HEAD_DIM
A_KV_W = A_KV_HEADS * HEAD_DIM
B_QK_W = B_HEADS * 2 * HEAD_DIM
B_V_W = B_HEADS * B_V_DIM
C_HEADS = 16
C_WIDTH = C_HEADS * HEAD_DIM
NA_ROWS = 8
NA_COLS = 16
N_EXPERTS = 16
N_GROUPS = 4
EXPERTS_PER_GROUP = N_EXPERTS // N_GROUPS
D_EXPERT = 1024

LANES = 128
SUBLANES = 8
ROW_TILES = D_MODEL // LANES
N_PROMPT = BATCH * SEQ
N_SAMPLE = DEC_BATCH * DEC_SEQ
N_TOK = N_PROMPT + N_SAMPLE
TM = 256
N_TILES = N_TOK // TM
N_PROMPT_TILES = N_PROMPT // TM
TM_LN = 512
LN_PIECE = 512
TM_OUT = 512
COND_ROWS = 16
CTX_COND_ROW = DEC_BATCH
MOE_TM = 256
MOE_ROWS = 2 * N_TOK + N_EXPERTS * MOE_TM
MOE_TILES = MOE_ROWS // MOE_TM
GRID_ROWS = DEC_SEQ // GRID_W
PROMPTS_PER_STEP = 2
NA_ROWS_PER_STEP = 4
NEG_BIG = -1e30
LOG2E = math.log2(math.e)
VMEM_LIMIT = 56 * 1024 * 1024

def _silu(x):
    return x * (1.0 / (1.0 + jnp.exp(-x)))


def _dot(a, b):
    return jnp.dot(a, b, preferred_element_type=F32)


def _dot_nt(a, b):
    return lax.dot_general(a, b, (((1,), (1,)), ((), ())), preferred_element_type=F32)


def _low_half(shape):
    return lax.broadcasted_iota(jnp.int32, shape, len(shape) - 1) < HEAD_DIM


def _swap_halves(x):
    return jnp.concatenate([x[:, HEAD_DIM:], x[:, :HEAD_DIM]], axis=1)


def _tile_cond(tm):
    tiles = np.arange(N_TOK // tm)
    npt = N_PROMPT // tm
    samp = np.maximum(tiles - npt, 0)
    per_seq = DEC_SEQ // tm
    cond = np.where(tiles < npt, CTX_COND_ROW, samp // per_seq)
    rope = np.where(tiles < npt, per_seq, samp % per_seq)
    return jnp.asarray(cond, jnp.int32), jnp.asarray(rope, jnp.int32)


def _split_specs(tm, merged):
    npt = N_PROMPT // tm
    base = npt if merged else 0
    a = pl.BlockSpec((tm, D_MODEL), lambda i, *_: (jnp.minimum(i, npt - 1), 0))
    b = pl.BlockSpec((tm, D_MODEL), lambda i, *_: (jnp.maximum(i - npt, 0) + base, 0))
    return a, b


def _mod_kernel(c_ref, w_ref, b_ref, o_ref):
    s = _silu(c_ref[...])
    s_hi = s.astype(BF16)
    s_lo = (s - s_hi.astype(F32)).astype(BF16)
    w = w_ref[...].astype(BF16)
    o_ref[...] = _dot(s_hi, w) + _dot(s_lo, w) + b_ref[...]


def _modulation(cond, w_mod, b_mod):
    tn = 1536
    return pl.pallas_call(
        _mod_kernel,
        out_shape=jax.ShapeDtypeStruct((DEPTH, COND_ROWS, 6 * D_MODEL), F32),
        grid=(DEPTH, 6 * D_MODEL // tn),
        in_specs=[
            pl.BlockSpec((COND_ROWS, D_MODEL), lambda l, j: (0, 0)),
            pl.BlockSpec((None, D_MODEL, tn), lambda l, j: (l, 0, j)),
            pl.BlockSpec((None, 1, tn), lambda l, j: (l, 0, j)),
        ],
        out_specs=pl.BlockSpec((None, COND_ROWS, tn), lambda l, j: (l, 0, j)),
        compiler_params=pltpu.CompilerParams(
            dimension_semantics=("arbitrary", "arbitrary"), vmem_limit_bytes=VMEM_LIMIT),
        name="modulation",
    )(cond, w_mod, b_mod.reshape(DEPTH, 1, 6 * D_MODEL))


def _rope_tables(tm):
    pos = np.arange(DEC_SEQ)
    rows = (pos // GRID_W).astype(np.float64)
    cols = (pos % GRID_W).astype(np.float64)
    nfreq = HEAD_DIM // 4
    inv = ROPE_THETA ** (-np.arange(nfreq, dtype=np.float64) / nfreq)
    d = np.arange(HEAD_DIM)
    dd = d % (HEAD_DIM // 2)
    p = np.where((d >= HEAD_DIM // 2)[None, :], cols[:, None], rows[:, None])
    ang = p * inv[dd % nfreq][None, :]
    cos, sin = np.cos(ang), np.sin(ang)
    second = (dd >= nfreq)[None, :]
    sa = np.where(second, sin, 0.0)
    sb = np.where(second, 0.0, -sin)

    def full(t, ident):
        t = np.concatenate([t, np.full((tm, HEAD_DIM), ident)], axis=0)
        return jnp.asarray(np.tile(t, (1, LANES // HEAD_DIM)), dtype=F32)

    return full(cos, 1.0), full(sa, 0.0), full(sb, 0.0)


def _lnproj_kernel(tile_cond_ref, tile_rope_ref, xa_ref, xb_ref, mod_ref, g_ref, w_ref, hg_ref, gmat_ref,
                   cos_ref, sa_ref, sb_ref, *out_refs, chunks, use_rope):
    i = pl.program_id(0)
    row = tile_cond_ref[i]
    x = jnp.where(i >= N_SAMPLE // TM_LN, xa_ref[...], xb_ref[...])
    ms = jnp.mean(x * x, axis=-1, keepdims=True)
    xn = x * lax.rsqrt(ms + RMS_EPS) * g_ref[...]
    shift = mod_ref[pl.ds(row, 1), 0:D_MODEL]
    scale = mod_ref[pl.ds(row, 1), D_MODEL:2 * D_MODEL]
    h = (xn * (1.0 + scale) + shift).astype(BF16)
    piece_chunks = LN_PIECE // LANES
    for c0 in range(0, len(chunks), piece_chunks):
        n_sub = min(piece_chunks, len(chunks) - c0)
        yp = _dot(h, w_ref[:, c0 * LANES:(c0 + n_sub) * LANES])
        yn = [None] * n_sub
        for s0 in range(0, n_sub, 2):
            if chunks[c0 + s0][0] or chunks[c0 + s0 + 1][0]:
                y2 = yp[:, s0 * LANES:(s0 + 2) * LANES]
                gs = _dot((y2 * y2).astype(BF16), gmat_ref[...])
                yn2 = y2 * lax.rsqrt(gs * (1.0 / HEAD_DIM) + RMS_EPS)
                yn[s0], yn[s0 + 1] = yn2[:, :LANES], yn2[:, LANES:]
        for sub in range(n_sub):
            c = c0 + sub
            normed, dst, dst_col, f32_dst, f32_col = chunks[c]
            if normed:
                y = yn[sub] * hg_ref[:, c * LANES:(c + 1) * LANES]
            else:
                y = yp[:, sub * LANES:(sub + 1) * LANES]
            if f32_dst is not None:
                out_refs[f32_dst][:, f32_col:f32_col + LANES] = y
            if normed and use_rope:
                y = (y * cos_ref[...] + pltpu.roll(y, HEAD_DIM // 4, 1) * sa_ref[...]
                     + pltpu.roll(y, LANES - HEAD_DIM // 4, 1) * sb_ref[...])
            out_refs[dst][:, dst_col:dst_col + LANES] = y.astype(BF16)


def _lnproj(xa, xb, merged, mod_l, gain, w_bf16, head_gain, chunks, q_w, kv_w, f32_widths, use_rope):
    dout = w_bf16.shape[1]
    tm = TM_LN
    npt = N_PROMPT // tm
    nst = N_SAMPLE // tm
    cond_tok, rope_tok = _tile_cond(tm)
    order = np.concatenate([np.arange(npt, npt + nst), np.arange(npt)])
    tile_cond, tile_rope = cond_tok[order], rope_tok[order]
    cos, sa, sb = _rope_tables(tm)
    gmat = jnp.asarray(np.kron(np.eye(2 * LANES // HEAD_DIM), np.ones((HEAD_DIM, HEAD_DIM))), dtype=BF16)
    const = lambda i, *_: (0, 0)
    tok = lambda i, *_: (jnp.where(i < nst, i + npt, i - nst), 0)
    rope_map = lambda i, tc, tr: (tr[i], 0)
    prm = lambda i, *_: (jnp.maximum(i - nst, 0), 0)
    base = npt if merged else 0
    xa_spec = pl.BlockSpec((tm, D_MODEL), prm)
    xb_spec = pl.BlockSpec((tm, D_MODEL), lambda i, *_: (jnp.minimum(i, nst - 1) + base, 0))
    out_shape = [jax.ShapeDtypeStruct((N_TOK, q_w), BF16), jax.ShapeDtypeStruct((N_TOK, kv_w), BF16)]
    out_specs = [pl.BlockSpec((tm, q_w), tok), pl.BlockSpec((tm, kv_w), tok)]
    for wd in f32_widths:
        out_shape.append(jax.ShapeDtypeStruct((N_PROMPT, wd), F32))
        out_specs.append(pl.BlockSpec((tm, wd), prm))
    return pl.pallas_call(
        functools.partial(_lnproj_kernel, chunks=chunks, use_rope=use_rope),
        out_shape=out_shape,
        grid_spec=pltpu.PrefetchScalarGridSpec(
            num_scalar_prefetch=2,
            grid=(N_TOK // tm,),
            in_specs=[
                xa_spec, xb_spec,
                pl.BlockSpec((COND_ROWS, 6 * D_MODEL), const),
                pl.BlockSpec((1, D_MODEL), const),
                pl.BlockSpec((D_MODEL, dout), const),
                pl.BlockSpec((1, dout), const),
                pl.BlockSpec((2 * LANES, 2 * LANES), const),
                pl.BlockSpec((tm, LANES), rope_map),
                pl.BlockSpec((tm, LANES), rope_map),
                pl.BlockSpec((tm, LANES), rope_map),
            ],
            out_specs=out_specs,
        ),
        compiler_params=pltpu.CompilerParams(
            dimension_semantics=("arbitrary",), vmem_limit_bytes=VMEM_LIMIT),
        name="lnproj",
    )(tile_cond, tile_rope, xa, xb, mod_l, gain.reshape(1, D_MODEL), w_bf16, head_gain, gmat, cos, sa, sb)


def _softmax_parts(s, want_sum=True):
    p = jnp.exp2(s - jnp.max(s, axis=-1, keepdims=True))
    return p, (jnp.sum(p, axis=-1, keepdims=True) if want_sum else None)


def _gqa_pairs(q_ref, q_col0, kv_ref, k_col0, v_col0, n_kv, group, tq):
    low = _low_half((tq, LANES))
    mxu_sums = group > 1
    head_out = [None] * (n_kv * group)
    jobs = [(pair, half) for pair in range(n_kv // 2) for half in range(2)]
    scores = []
    for pair, half in jobs:
        k_pair = kv_ref[:, k_col0 + pair * LANES:k_col0 + (pair + 1) * LANES]
        kvh = 2 * pair + half
        keep = low if half == 0 else jnp.logical_not(low)
        qs = []
        for g in range(group):
            head = kvh * group + g
            blk = q_ref[:, q_col0 + (head // 2) * LANES:q_col0 + (head // 2 + 1) * LANES]
            if head % 2 != half:
                blk = _swap_halves(blk)
            qs.append(jnp.where(keep, blk, jnp.zeros_like(blk)))
        q = qs[0] if group == 1 else jnp.concatenate(qs, axis=0)
        scores.append(_dot_nt(q, k_pair))

    def finish():
        for (pair, half), s in zip(jobs, scores):
            v = kv_ref[:, v_col0 + pair * LANES:v_col0 + (pair + 1) * LANES]
            kvh = 2 * pair + half
            if mxu_sums:
                own = _low_half(v.shape) if half == 0 else jnp.logical_not(_low_half(v.shape))
                v = jnp.where(own, v, jnp.ones_like(v))
            p, l = _softmax_parts(s, want_sum=not mxu_sums)
            o = _dot(p.astype(BF16), v)
            o = o * (1.0 / (pltpu.roll(o, HEAD_DIM, 1) if mxu_sums else l))
            for g in range(group):
                head = kvh * group + g
                og = o[g * tq:(g + 1) * tq]
                if head % 2 != half:
                    og = pltpu.roll(og, HEAD_DIM, 1)
                head_out[head] = og
        return [jnp.where(low, head_out[2 * k], head_out[2 * k + 1]) for k in range(n_kv * group // 2)]

    return finish


def _attn_ab_kernel(*refs, has_cache, tq, n_seq, lam_init):
    if has_cache:
        q_ref, kvn_ref, cak_ref, cav_ref, cbk_ref, cbv_ref, lamv_ref, subln_ref, o_ref, kv_ref = refs

        @pl.when(pl.program_id(1) == 0)
        def _():
            col = 0
            for c_ref in (cak_ref, cav_ref, cbk_ref, cbv_ref):
                w = c_ref.shape[1]
                kv_ref[0:PAST_LEN, col:col + w] = c_ref[...].astype(BF16)
                col += w
            kv_ref[PAST_LEN:, :] = kvn_ref[...]
    else:
        q_ref, kv_ref, lamv_ref, subln_ref, o_ref = refs

    lv = lamv_ref[...]
    l1 = jnp.sum(lv[0:1] * lv[1:2], axis=-1, keepdims=True)
    l2 = jnp.sum(lv[2:3] * lv[3:4], axis=-1, keepdims=True)
    lam = jnp.exp(l1) - jnp.exp(l2) + lam_init
    bk0 = 2 * A_KV_W
    bv0 = bk0 + B_QK_W
    low = _low_half((tq, LANES))
    t_kv = kv_ref.shape[0] // n_seq
    for b in range(n_seq):
        q_b = q_ref.at[b * tq:(b + 1) * tq]
        kv_b = kv_ref.at[b * t_kv:(b + 1) * t_kv]
        finish_a = _gqa_pairs(q_b, 0, kv_b, 0, A_KV_W, A_KV_HEADS, A_GROUP, tq)
        b_scores = []
        for h in range(B_HEADS):
            lanes = slice(h * LANES, (h + 1) * LANES)
            qp = q_b[:, A_Q_W + lanes.start:A_Q_W + lanes.stop]
            zero = jnp.zeros_like(qp)
            q = jnp.concatenate([jnp.where(low, qp, zero), jnp.where(low, zero, qp)], axis=0)
            b_scores.append(_dot_nt(q, kv_b[:, bk0 + lanes.start:bk0 + lanes.stop]))
        outs = finish_a()
        for h in range(B_HEADS):
            lanes = slice(h * LANES, (h + 1) * LANES)
            p, l = _softmax_parts(b_scores[h])
            r = 1.0 / l
            a = p[:tq] * r[:tq] - p[tq:] * (lam * r[tq:])
            o = _dot(a.astype(BF16), kv_b[:, bv0 + lanes.start:bv0 + lanes.stop])
            ms = jnp.mean(o * o, axis=-1, keepdims=True)
            o = o * lax.rsqrt(ms + RMS_EPS) * subln_ref[...] * (1.0 - lam_init)
            outs.append(o)
        o_ref[b * tq:(b + 1) * tq, :] = jnp.concatenate(outs, axis=1).astype(BF16)


def _attn_ab(q, kv, lamv, subln, lam_init, caches):
    kv_w = kv.shape[1]
    kern = functools.partial(_attn_ab_kernel, lam_init=lam_init)
    const = lambda b, j: (0, 0)
    cp = pltpu.CompilerParams(dimension_semantics=("arbitrary", "arbitrary"), vmem_limit_bytes=VMEM_LIMIT)
    if caches is None:
        n_seq = 1
        rows = n_seq * SEQ
        return pl.pallas_call(
            functools.partial(kern, has_cache=False, tq=SEQ, n_seq=n_seq),
            out_shape=jax.ShapeDtypeStruct((N_PROMPT, D_MODEL), BF16),
            grid=(BATCH // n_seq, 1),
            in_specs=[
                pl.BlockSpec((rows, D_MODEL), lambda b, j: (b, 0)),
                pl.BlockSpec((rows, kv_w), lambda b, j: (b, 0)),
                pl.BlockSpec((4, HEAD_DIM), const),
                pl.BlockSpec((1, B_V_DIM), const),
            ],
            out_specs=pl.BlockSpec((rows, D_MODEL), lambda b, j: (b, 0)),
            compiler_params=cp,
            name="attn_ab_prompt",
        )(q, kv, lamv, subln)
    tq = 256
    nq = DEC_SEQ // tq
    q0 = N_PROMPT // tq
    kv0 = N_PROMPT // DEC_SEQ
    cak, cav, cbk, cbv = caches
    cspec = lambda w: pl.BlockSpec((None, PAST_LEN, w), lambda b, j: (b, 0, 0))
    return pl.pallas_call(
        functools.partial(kern, has_cache=True, tq=tq, n_seq=1),
        out_shape=jax.ShapeDtypeStruct((N_SAMPLE, D_MODEL), BF16),
        grid=(DEC_BATCH, nq),
        in_specs=[
            pl.BlockSpec((tq, D_MODEL), lambda b, j: (q0 + b * nq + j, 0)),
            pl.BlockSpec((DEC_SEQ, kv_w), lambda b, j: (kv0 + b, 0)),
            cspec(A_KV_W), cspec(A_KV_W), cspec(B_QK_W), cspec(B_V_W),
            pl.BlockSpec((4, HEAD_DIM), const),
            pl.BlockSpec((1, B_V_DIM), const),
        ],
        out_specs=pl.BlockSpec((tq, D_MODEL), lambda b, j: (b * nq + j, 0)),
        scratch_shapes=[pltpu.VMEM((PAST_LEN + DEC_SEQ, kv_w), BF16)],
        compiler_params=cp,
        name="attn_ab_sample",
    )(q, kv, cak, cav, cbk, cbv, lamv, subln)


def _attn_c_prompt_kernel(q_ref, kv_ref, o_ref):
    for b in range(PROMPTS_PER_STEP):
        rows = slice(b * SEQ, (b + 1) * SEQ)
        outs = _gqa_pairs(q_ref.at[rows], 0, kv_ref.at[rows], 0, C_WIDTH, C_HEADS, 1, SEQ)()
        o_ref[rows, :] = jnp.concatenate(outs, axis=1).astype(BF16)


def _attn_c_prompt(q, kv):
    rows = PROMPTS_PER_STEP * SEQ
    return pl.pallas_call(
        _attn_c_prompt_kernel,
        out_shape=jax.ShapeDtypeStruct((N_PROMPT, D_MODEL), BF16),
        grid=(BATCH // PROMPTS_PER_STEP,),
        in_specs=[
            pl.BlockSpec((rows, C_WIDTH), lambda b: (b, 0)),
            pl.BlockSpec((rows, 2 * C_WIDTH), lambda b: (b, 0)),
        ],
        out_specs=pl.BlockSpec((rows, C_WIDTH), lambda b: (b, 0)),
        compiler_params=pltpu.CompilerParams(dimension_semantics=("arbitrary",), vmem_limit_bytes=VMEM_LIMIT),
        name="attn_c_prompt",
    )(q, kv)


def _na_kernel(q_ref, kv_ref, ck_ref, cv_ref, tp_ref, o_ref, ckb, cvb):
    step = pl.program_id(1)

    @pl.when(step == 0)
    def _():
        ckb[...] = ck_ref[...].astype(BF16)
        cvb[...] = cv_ref[...].astype(BF16)

    kh = min(NA_ROWS, GRID_ROWS)
    win = kh * GRID_W
    low = _low_half((GRID_W, LANES))
    n_pair = C_HEADS // 2
    for rr in range(NA_ROWS_PER_STEP):
        r = step * NA_ROWS_PER_STEP + rr
        rows = slice(rr * GRID_W, (rr + 1) * GRID_W)
        rs = jnp.clip(r - kh // 2, 0, GRID_ROWS - kh)
        ro0 = rs - r + (NA_ROWS - 1)
        start = pl.multiple_of(rs * GRID_W, GRID_W)
        s_lat, s_ctx = [], []
        for j in range(n_pair):
            lanes = slice(j * LANES, (j + 1) * LANES)
            qp = q_ref[rows, lanes]
            zero = jnp.zeros_like(qp)
            q = jnp.concatenate([jnp.where(low, qp, zero), jnp.where(low, zero, qp)], axis=0)
            kw = kv_ref[pl.ds(start, win), lanes]
            bias = jnp.concatenate(
                [jnp.concatenate([tp_ref[2 * j + hh, ro0 + 2 * t] for t in range(kh // 2)], axis=1)
                 for hh in range(2)], axis=0)
            s_lat.append(_dot_nt(q, kw) + bias)
            s_ctx.append(_dot_nt(q, ckb[:, lanes]))
        outs = []
        for j in range(n_pair):
            lanes = slice(j * LANES, (j + 1) * LANES)
            s_l, s_c = s_lat[j], s_ctx[j]
            m = jnp.maximum(jnp.max(s_l, axis=-1, keepdims=True), jnp.max(s_c, axis=-1, keepdims=True))
            p_l = jnp.exp2(s_l - m)
            p_c = jnp.exp2(s_c - m)
            l = jnp.sum(p_l, axis=-1, keepdims=True) + jnp.sum(p_c, axis=-1, keepdims=True)
            vw = kv_ref[pl.ds(start, win), C_WIDTH + lanes.start:C_WIDTH + lanes.stop]
            o = _dot(p_c.astype(BF16), cvb[:, lanes]) + _dot(p_l.astype(BF16), vw)
            o = o * (1.0 / l)
            outs.append(jnp.where(low, o[:GRID_W], o[GRID_W:]))
        o_ref[rows, :] = jnp.concatenate(outs, axis=1).astype(BF16)


NA_BIAS_PAD = GRID_W - NA_COLS


def _na_bias_kernel(w_ref, mask_ref, tp_ref):
    low = _low_half((GRID_W, LANES))
    keep = mask_ref[...] > 0.0
    n_off = 2 * NA_ROWS - 1
    left, right = [], []
    for ro in range(n_off):
        row = jnp.broadcast_to(w_ref[ro:ro + 1, :] * LOG2E, (GRID_W, LANES))
        left.append(pltpu.roll(row, LANES - GRID_W + 1, 1, stride=1, stride_axis=0))
        right.append(pltpu.roll(row, 1, 1, stride=1, stride_axis=0))
    for t in range(n_off - 1):
        tp_ref[t] = jnp.where(keep, jnp.where(low, left[t], right[t + 1]), NEG_BIG)


def _na_bias_table(rpb):
    cols = np.arange(GRID_W)
    col_start = np.clip(cols - NA_COLS // 2, 0, GRID_W - NA_COLS)
    col_in = (cols[None, :] >= col_start[:, None]) & (cols[None, :] < col_start[:, None] + NA_COLS)
    assert np.abs((cols[None, :] - cols[:, None])[col_in]).max() <= NA_COLS - 1
    mask = jnp.asarray(np.tile(col_in, (1, LANES // GRID_W)), dtype=F32)
    n_off = 2 * NA_ROWS - 1
    n_rel = 2 * NA_COLS - 1
    w = jnp.pad(rpb.astype(F32), ((0, 0), (0, 0), (NA_BIAS_PAD, LANES - NA_BIAS_PAD - n_rel)))
    return pl.pallas_call(
        _na_bias_kernel,
        out_shape=jax.ShapeDtypeStruct((C_HEADS, n_off - 1, GRID_W, LANES), F32),
        grid=(C_HEADS,),
        in_specs=[
            pl.BlockSpec((None, n_off, LANES), lambda h: (h, 0, 0)),
            pl.BlockSpec((GRID_W, LANES), lambda h: (0, 0)),
        ],
        out_specs=pl.BlockSpec((None, n_off - 1, GRID_W, LANES), lambda h: (h, 0, 0, 0)),
        compiler_params=pltpu.CompilerParams(dimension_semantics=("arbitrary",)),
        name="na_bias",
    )(w, mask)


def _attn_na(q, kv, ck, cv, tp):
    tq = NA_ROWS_PER_STEP * GRID_W
    steps = GRID_ROWS // NA_ROWS_PER_STEP
    q0 = N_PROMPT // tq
    kv0 = N_PROMPT // DEC_SEQ
    return pl.pallas_call(
        _na_kernel,
        out_shape=jax.ShapeDtypeStruct((N_SAMPLE, D_MODEL), BF16),
        grid=(DEC_BATCH, steps),
        in_specs=[
            pl.BlockSpec((tq, C_WIDTH), lambda b, r: (q0 + b * steps + r, 0)),
            pl.BlockSpec((DEC_SEQ, 2 * C_WIDTH), lambda b, r: (kv0 + b, 0)),
            pl.BlockSpec((None, PAST_LEN, C_WIDTH), lambda b, r: (b, 0, 0)),
            pl.BlockSpec((None, PAST_LEN, C_WIDTH), lambda b, r: (b, 0, 0)),
            pl.BlockSpec(tp.shape, lambda b, r: (0, 0, 0, 0)),
        ],
        out_specs=pl.BlockSpec((tq, C_WIDTH), lambda b, r: (b * steps + r, 0)),
        scratch_shapes=[pltpu.VMEM((PAST_LEN, C_WIDTH), BF16), pltpu.VMEM((PAST_LEN, C_WIDTH), BF16)],
        compiler_params=pltpu.CompilerParams(
            dimension_semantics=("arbitrary", "arbitrary"), vmem_limit_bytes=VMEM_LIMIT),
        name="attn_na_sample",
    )(q, kv, ck, cv, tp)


def _first_wins_ranks(vals):
    ranks = []
    for i in range(len(vals)):
        r = jnp.zeros_like(vals[i])
        for j in range(len(vals)):
            if j == i:
                continue
            beats = (vals[j] >= vals[i]) if j < i else (vals[j] > vals[i])
            r = r + jnp.where(beats, 1.0, 0.0)
        ranks.append(r)
    return ranks


def _outproj_kernel(tile_cond_ref, op_ref, os_ref, xa_ref, xb_ref, w_ref, mod_ref, g2_ref, wrt_ref, br_ref,
                    tri_ref, x1_ref, h2_ref, gates_ref, rank_ref, carry_ref):
    i = pl.program_id(0)
    row = tile_cond_ref[i]
    is_prompt = i < N_PROMPT // TM_OUT

    @pl.when(i == 0)
    def _():
        carry_ref[...] = jnp.zeros_like(carry_ref)

    o = jnp.where(is_prompt, op_ref[...], os_ref[...])
    x = jnp.where(is_prompt, xa_ref[...], xb_ref[...])
    acc = _dot(o, w_ref[...])
    gate = mod_ref[pl.ds(row, 1), 2 * D_MODEL:3 * D_MODEL]
    x1 = x + gate * acc
    x1_ref[...] = x1
    ms = jnp.mean(x1 * x1, axis=-1, keepdims=True)
    xn = x1 * lax.rsqrt(ms + RMS_EPS) * g2_ref[...]
    shift = mod_ref[pl.ds(row, 1), 3 * D_MODEL:4 * D_MODEL]
    scale = mod_ref[pl.ds(row, 1), 4 * D_MODEL:5 * D_MODEL]
    h2 = xn * (1.0 + scale) + shift
    for s in range(ROW_TILES):
        h2_ref[pl.ds(s, TM_OUT, stride=ROW_TILES), :] = h2[:, s * LANES:(s + 1) * LANES]

    h_hi = h2.astype(BF16)
    h_lo = (h2 - h_hi.astype(F32)).astype(BF16)
    part = _dot_nt(wrt_ref[...], h_hi)
    logits = part[:N_EXPERTS] + part[N_EXPERTS:] + _dot_nt(wrt_ref[:N_EXPERTS, :], h_lo)
    e = jnp.exp(logits - jnp.max(logits, axis=0, keepdims=True))
    scores = e * (1.0 / jnp.sum(e, axis=0, keepdims=True))
    sel = scores + br_ref[...]
    sel_rows = [sel[k:k + 1, :] for k in range(N_EXPERTS)]
    in_top2 = []
    group_sum = []
    for g in range(N_GROUPS):
        vals = sel_rows[g * EXPERTS_PER_GROUP:(g + 1) * EXPERTS_PER_GROUP]
        ranks = _first_wins_ranks(vals)
        top = [rk < 2.0 for rk in ranks]
        in_top2.extend(top)
        s = jnp.zeros_like(vals[0])
        for v, t in zip(vals, top):
            s = s + jnp.where(t, v, 0.0)
        group_sum.append(s)
    group_rank = _first_wins_ranks(group_sum)
    mask_rows = []
    for k in range(N_EXPERTS):
        chosen = jnp.where(in_top2[k], 1.0, 0.0) * jnp.where(group_rank[k // EXPERTS_PER_GROUP] < 1.0, 1.0, 0.0)
        mask_rows.append(chosen)
    mask = jnp.concatenate(mask_rows, axis=0)
    picked = scores * mask
    gates_ref[...] = picked * (1.0 / jnp.sum(picked, axis=0, keepdims=True))
    prefix = _dot(mask.astype(BF16), tri_ref[...])
    rank_ref[...] = jnp.where(mask > 0.0, prefix + carry_ref[...], -1.0)
    carry_ref[...] = carry_ref[...] + jnp.sum(mask, axis=1, keepdims=True)


def _outproj(o_prompt, o_sample, xa, xb, merged, w_bf16, mod_l, gain2, w_router, b_router):
    const = lambda i, *_: (0, 0)
    tok = lambda i, *_: (i, 0)
    tokT = lambda i, *_: (0, i)
    tm = TM_OUT
    tile_cond, _ = _tile_cond(tm)
    op_spec, os_spec = _split_specs(tm, False)
    xa_spec, xb_spec = _split_specs(tm, merged)
    tri = jnp.asarray(np.triu(np.ones((tm, tm)), k=1), dtype=BF16)
    wrt = w_router.T.astype(F32)
    wrt_hi = wrt.astype(BF16)
    wrt_split = jnp.concatenate([wrt_hi, (wrt - wrt_hi.astype(F32)).astype(BF16)], axis=0)
    return pl.pallas_call(
        _outproj_kernel,
        out_shape=[
            jax.ShapeDtypeStruct((N_TOK, D_MODEL), F32),
            jax.ShapeDtypeStruct((N_TOK * ROW_TILES, LANES), F32),
            jax.ShapeDtypeStruct((N_EXPERTS, N_TOK), F32),
            jax.ShapeDtypeStruct((N_EXPERTS, N_TOK), F32),
        ],
        grid_spec=pltpu.PrefetchScalarGridSpec(
            num_scalar_prefetch=1,
            grid=(N_TOK // tm,),
            in_specs=[
                op_spec, os_spec, xa_spec, xb_spec,
                pl.BlockSpec((D_MODEL, D_MODEL), const),
                pl.BlockSpec((COND_ROWS, 6 * D_MODEL), const),
                pl.BlockSpec((1, D_MODEL), const),
                pl.BlockSpec((2 * N_EXPERTS, D_MODEL), const),
                pl.BlockSpec((N_EXPERTS, 1), const),
                pl.BlockSpec((tm, tm), const),
            ],
            out_specs=[
                pl.BlockSpec((tm, D_MODEL), tok),
                pl.BlockSpec((tm * ROW_TILES, LANES), tok),
                pl.BlockSpec((N_EXPERTS, tm), tokT),
                pl.BlockSpec((N_EXPERTS, tm), tokT),
            ],
            scratch_shapes=[pltpu.VMEM((N_EXPERTS, 1), F32)],
        ),
        compiler_params=pltpu.CompilerParams(
            dimension_semantics=("arbitrary",), vmem_limit_bytes=VMEM_LIMIT),
        name="outproj_router",
    )(tile_cond, o_prompt, o_sample, xa, xb, w_bf16, mod_l, gain2.reshape(1, D_MODEL), wrt_split,
      b_router.reshape(N_EXPERTS, 1), tri)


def _plan_kernel(gates_ref, rank_ref, dests_ref, wts_ref, last_ref, misc_ref):
    rank = rank_ref[...]
    gates = gates_ref[...]
    sel = rank >= 0.0
    hit = jnp.where(sel, 1.0, 0.0)
    tile = float(MOE_TM)

    def pad_up(c):
        return jnp.floor((c + (tile - 1.0)) * (1.0 / tile)) * tile

    counts_col = jnp.sum(hit, axis=1, keepdims=True)
    ones = jnp.ones((SUBLANES, hit.shape[1]), BF16)
    counts_row = _dot_nt(ones, hit.astype(BF16))[0:1]
    padded_col = pad_up(counts_col)
    padded_row = pad_up(counts_row)
    e_sub = lax.broadcasted_iota(jnp.int32, (N_EXPERTS, N_EXPERTS), 0)
    e_lane = lax.broadcasted_iota(jnp.int32, (N_EXPERTS, N_EXPERTS), 1)
    ends_col = jnp.sum(jnp.where(e_lane <= e_sub, padded_row, 0.0), axis=1, keepdims=True)
    ends_row = jnp.sum(jnp.where(e_sub <= e_lane, padded_col, 0.0), axis=0, keepdims=True)
    dest = jnp.where(sel, ends_col - padded_col + rank, -1.0)
    d1 = jnp.max(dest, axis=0, keepdims=True)
    d0 = jnp.min(jnp.where(sel, dest, float(MOE_ROWS)), axis=0, keepdims=True)
    w0 = jnp.sum(jnp.where(dest == d0, gates, 0.0), axis=0, keepdims=True)
    w1 = jnp.sum(jnp.where(dest == d1, gates, 0.0), axis=0, keepdims=True)
    dests_ref[...] = (jnp.concatenate([d0, d1], axis=0) * float(ROW_TILES)).astype(jnp.int32)
    wts_ref[...] = jnp.concatenate([w0, w1], axis=0)

    last = jnp.where(padded_col > 0.0, (ends_col - tile) * float(ROW_TILES), -1.0)
    last_ref[...] = jnp.broadcast_to(last, last_ref.shape).astype(jnp.int32)

    lane = lax.broadcasted_iota(jnp.int32, (1, LANES), 1).astype(F32)
    n_tiles = ends_row[:, N_EXPERTS - 1:N_EXPERTS] * (1.0 / tile)
    tile_expert = jnp.sum(jnp.where(lane * tile >= ends_col, 1.0, 0.0), axis=0, keepdims=True)
    tile_expert = jnp.minimum(tile_expert, float(N_EXPERTS - 1))
    e_col = lax.broadcasted_iota(jnp.int32, (N_EXPERTS, 1), 0).astype(F32)
    later = jnp.where((e_col > tile_expert) & (padded_col > 0.0), e_col, float(N_EXPERTS))
    nxt = jnp.min(later, axis=0, keepdims=True)
    nxt = jnp.where(nxt > float(N_EXPERTS - 1), tile_expert, nxt)
    spare = n_tiles + lane
    spare = jnp.where(spare < float(MOE_TILES), spare * (tile * ROW_TILES), -1.0)
    zero = jnp.zeros((1, LANES), F32)
    rows = [tile_expert, nxt, spare, n_tiles + zero] + [zero] * (SUBLANES - 4)
    misc_ref[...] = jnp.concatenate(rows, axis=0).astype(jnp.int32)


def _route_plan(gates, rank):
    assert MOE_TILES <= LANES and N_EXPERTS <= LANES
    full = lambda shape: pl.BlockSpec(shape, lambda: (0,) * len(shape))
    dests, wts, last, misc = pl.pallas_call(
        _plan_kernel,
        out_shape=[
            jax.ShapeDtypeStruct((2, N_TOK), jnp.int32),
            jax.ShapeDtypeStruct((2, N_TOK), F32),
            jax.ShapeDtypeStruct((N_EXPERTS, LANES), jnp.int32),
            jax.ShapeDtypeStruct((SUBLANES, LANES), jnp.int32),
        ],
        in_specs=[full((N_EXPERTS, N_TOK)), full((N_EXPERTS, N_TOK))],
        out_specs=[full((2, N_TOK)), full((2, N_TOK)), full((N_EXPERTS, LANES)), full((SUBLANES, LANES))],
        compiler_params=pltpu.CompilerParams(vmem_limit_bytes=VMEM_LIMIT),
        name="route_plan",
    )(gates, rank)
    tile_expert = misc[0, :MOE_TILES]
    next_expert = misc[1, :MOE_TILES]
    n_tiles = misc[3, :1]
    clear = jnp.concatenate([last[:, 0], misc[2, :N_EXPERTS]])
    return dests.reshape(2 * N_TOK), wts.T, tile_expert, n_tiles, next_expert, clear


def _scatter_kernel(d_ref, last_ref, h_ref, xs_ref, zeros, sem, zsem):
    i = pl.program_id(0)
    base = i * TM
    rows = MOE_TM * ROW_TILES

    @pl.when(i == 0)
    def _():
        zeros[...] = jnp.zeros_like(zeros)
        for e in range(2 * N_EXPERTS):
            @pl.when(last_ref[e] >= 0)
            def _():
                at = pl.multiple_of(last_ref[e], SUBLANES)
                pltpu.make_async_copy(zeros, xs_ref.at[pl.ds(at, rows)], zsem).start()
        for e in range(2 * N_EXPERTS):
            @pl.when(last_ref[e] >= 0)
            def _():
                pltpu.make_async_copy(zeros, xs_ref.at[pl.ds(0, rows)], zsem).wait()

    def copy(r, t):
        src = h_ref.at[pl.ds(pl.multiple_of(r * ROW_TILES, SUBLANES), ROW_TILES)]
        dst = xs_ref.at[pl.ds(pl.multiple_of(t, SUBLANES), ROW_TILES)]
        return pltpu.make_async_copy(src, dst, sem)

    def issue(r, c):
        copy(r, d_ref[base + r]).start(priority=0)
        copy(r, d_ref[N_TOK + base + r]).start(priority=1)
        return c

    lax.fori_loop(0, TM, issue, 0, unroll=8)

    def drain(r, c):
        copy(r, 0).wait()
        copy(r, 0).wait()
        return c

    lax.fori_loop(0, TM, drain, 0, unroll=8)


def _moe_scatter(dests, last_tile, h2):
    return pl.pallas_call(
        _scatter_kernel,
        out_shape=jax.ShapeDtypeStruct((MOE_ROWS * ROW_TILES, LANES), F32),
        grid_spec=pltpu.PrefetchScalarGridSpec(
            num_scalar_prefetch=2,
            grid=(N_TILES,),
            in_specs=[pl.BlockSpec((TM * ROW_TILES, LANES), lambda i, *_: (i, 0))],
            out_specs=pl.BlockSpec(memory_space=pl.ANY),
            scratch_shapes=[
                pltpu.VMEM((MOE_TM * ROW_TILES, LANES), F32),
                pltpu.SemaphoreType.DMA,
                pltpu.SemaphoreType.DMA,
            ],
        ),
        compiler_params=pltpu.CompilerParams(
            dimension_semantics=("arbitrary",), vmem_limit_bytes=VMEM_LIMIT, has_side_effects=True),
        name="moe_scatter",
    )(dests, last_tile, h2)


def _mlp_kernel(te_ref, nv_ref, nxt_ref, xs_ref, wg_hbm, wu_hbm, wd_hbm, y_ref, stage, wgb, wub, wdb, hb, sem,
                *, layer):
    t = pl.program_id(0)

    def fetch(e):
        return [pltpu.make_async_copy(w.at[layer, e], stage.at[k], sem.at[k])
                for k, w in enumerate((wg_hbm, wu_hbm, wd_hbm))]

    @pl.when(t < nv_ref[0])
    def _():
        e = te_ref[t]
        prev = te_ref[jnp.maximum(t - 1, 0)]

        @pl.when(t == 0)
        def _():
            for cp in fetch(e):
                cp.start()

        @pl.when((t == 0) | (e != prev))
        def _():
            for cp in fetch(e):
                cp.wait()
            wgb[...] = stage[0].astype(BF16)
            wub[...] = stage[1].astype(BF16)
            wdb[...] = stage[2].astype(BF16)

            @pl.when(nxt_ref[t] != e)
            def _():
                for cp in fetch(nxt_ref[t]):
                    cp.start()

        x = jnp.concatenate(
            [xs_ref[pl.ds(s, MOE_TM, stride=ROW_TILES), :] for s in range(ROW_TILES)], axis=1).astype(BF16)
        step = 512
        for c in range(0, D_EXPERT, step):
            g = _dot(x, wgb[:, c:c + step])
            u = _dot(x, wub[:, c:c + step])
            hb[:, c:c + step] = (_silu(g) * u).astype(BF16)
        y = _dot(hb[...], wdb[...])
        for s in range(ROW_TILES):
            y_ref[pl.ds(s, MOE_TM, stride=ROW_TILES), :] = y[:, s * LANES:(s + 1) * LANES]

    @pl.when(t >= nv_ref[0])
    def _():
        y_ref[...] = jnp.zeros_like(y_ref)


def _moe_mlp(layer, tile_expert, n_tiles, next_expert, xs, w_gate, w_up, w_down):
    assert D_MODEL == D_EXPERT

    def tile_map(t, te, nv, nx):
        return (jnp.minimum(t, nv[0] - 1), 0)

    def out_map(t, te, nv, nx):
        return (t, 0)

    hbm = pl.BlockSpec(memory_space=pl.ANY)
    return pl.pallas_call(
        functools.partial(_mlp_kernel, layer=layer),
        out_shape=jax.ShapeDtypeStruct((MOE_ROWS * ROW_TILES, LANES), F32),
        grid_spec=pltpu.PrefetchScalarGridSpec(
            num_scalar_prefetch=3,
            grid=(MOE_TILES,),
            in_specs=[pl.BlockSpec((MOE_TM * ROW_TILES, LANES), tile_map), hbm, hbm, hbm],
            out_specs=pl.BlockSpec((MOE_TM * ROW_TILES, LANES), out_map),
            scratch_shapes=[
                pltpu.VMEM((3, D_MODEL, D_EXPERT), F32),
                pltpu.VMEM((D_MODEL, D_EXPERT), BF16),
                pltpu.VMEM((D_MODEL, D_EXPERT), BF16),
                pltpu.VMEM((D_EXPERT, D_MODEL), BF16),
                pltpu.VMEM((MOE_TM, D_EXPERT), BF16),
                pltpu.SemaphoreType.DMA((3,)),
            ],
        ),
        compiler_params=pltpu.CompilerParams(
            dimension_semantics=("arbitrary",), vmem_limit_bytes=VMEM_LIMIT),
        name="moe_mlp",
    )(tile_expert, n_tiles, next_expert, xs, w_gate, w_up, w_down)


def _combine_kernel(d_ref, tile_cond_ref, x1_ref, w_ref, mod_ref, y_ref, *rest, split_out):
    if split_out:
        outp_ref, outs_ref, buf, sem = rest
    else:
        out_ref, buf, sem = rest
    i = pl.program_id(0)
    row = tile_cond_ref[i]
    slot = i % 2

    def copy(sl, k, r, t):
        src = y_ref.at[pl.ds(pl.multiple_of(t, SUBLANES), ROW_TILES)]
        dst = buf.at[sl, k, pl.ds(pl.multiple_of(r * ROW_TILES, SUBLANES), ROW_TILES)]
        return pltpu.make_async_copy(src, dst, sem.at[sl])

    def gather_tile(tile, sl):
        base = tile * TM

        def issue(r, c):
            copy(sl, 0, r, d_ref[base + r]).start(priority=0)
            copy(sl, 1, r, d_ref[N_TOK + base + r]).start(priority=1)
            return c

        lax.fori_loop(0, TM, issue, 0, unroll=8)

    @pl.when(i == 0)
    def _():
        gather_tile(0, 0)

    @pl.when(i + 1 < N_TILES)
    def _():
        gather_tile(i + 1, 1 - slot)

    def drain(r, c):
        copy(slot, 0, r, 0).wait()
        copy(slot, 1, r, 0).wait()
        return c

    lax.fori_loop(0, TM, drain, 0, unroll=8)
    w = w_ref[...]
    w0 = w[:, 0:1]
    w1 = w[:, 1:2]
    parts = []
    for s in range(ROW_TILES):
        y0 = buf[slot, 0, pl.ds(s, TM, stride=ROW_TILES), :]
        y1 = buf[slot, 1, pl.ds(s, TM, stride=ROW_TILES), :]
        parts.append(w0 * y0 + w1 * y1)
    gate = mod_ref[pl.ds(row, 1), 5 * D_MODEL:6 * D_MODEL]
    out = x1_ref[...] + gate * jnp.concatenate(parts, axis=1)
    if split_out:
        @pl.when(i < N_PROMPT_TILES)
        def _():
            outp_ref[...] = out

        @pl.when(i >= N_PROMPT_TILES)
        def _():
            outs_ref[...] = out
    else:
        out_ref[...] = out


def _moe_combine(dests, x1, weights, mod_l, y, split_out):
    tile_cond, _ = _tile_cond(TM)
    if split_out:
        out_shape = [jax.ShapeDtypeStruct((N_PROMPT, D_MODEL), F32), jax.ShapeDtypeStruct((N_SAMPLE, D_MODEL), F32)]
        out_specs = list(_split_specs(TM, False))
    else:
        out_shape = jax.ShapeDtypeStruct((N_TOK, D_MODEL), F32)
        out_specs = pl.BlockSpec((TM, D_MODEL), lambda i, *_: (i, 0))
    return pl.pallas_call(
        functools.partial(_combine_kernel, split_out=split_out),
        out_shape=out_shape,
        grid_spec=pltpu.PrefetchScalarGridSpec(
            num_scalar_prefetch=2,
            grid=(N_TILES,),
            in_specs=[
                pl.BlockSpec((TM, D_MODEL), lambda i, *_: (i, 0)),
                pl.BlockSpec((TM, 2), lambda i, *_: (i, 0)),
                pl.BlockSpec((COND_ROWS, 6 * D_MODEL), lambda i, *_: (0, 0)),
                pl.BlockSpec(memory_space=pl.ANY),
            ],
            out_specs=out_specs,
            scratch_shapes=[pltpu.VMEM((2, 2, TM * ROW_TILES, LANES), F32), pltpu.SemaphoreType.DMA((2,))],
        ),
        compiler_params=pltpu.CompilerParams(
            dimension_semantics=("arbitrary",), vmem_limit_bytes=VMEM_LIMIT),
        name="moe_combine",
    )(dests, tile_cond, x1, weights, mod_l, y)


def _chunk_plan(segments):
    chunks = []
    for width, normed, dst, dst_col0, f32_dst in segments:
        for k in range(width // LANES):
            chunks.append((normed, dst, dst_col0 + k * LANES, f32_dst, k * LANES))
    return tuple(chunks)


def _head_gain(parts):
    cols = []
    for width, g, mult in parts:
        if g is None:
            cols.append(jnp.ones((width,), F32))
        else:
            cols.append(jnp.tile(g.astype(F32) * mult, width // HEAD_DIM))
    return jnp.concatenate(cols).reshape(1, -1)


def kernel(x_prompt, x_sample, cache_a_k, cache_a_v, cache_b_k, cache_b_v, cache_c_k, cache_c_v, c, c_ctx, w_mod, b_mod, norm_mix, norm_ffn, w_in_ab, w_out_ab, a_q_norm, a_k_norm, b_q_norm, b_k_norm, lam_q1, lam_k1, lam_q2, lam_k2, b_subln, w_in_c, w_out_c, c_q_norm, c_k_norm, c_rpb, w_router, b_router, w_gate, w_up, w_down):
    scale = LOG2E * HEAD_DIM ** -0.5
    cond = jnp.concatenate(
        [c, c_ctx[None, :], jnp.zeros((COND_ROWS - DEC_BATCH - 1, D_MODEL), F32)], axis=0)
    mod = _modulation(cond, w_mod, b_mod)

    xa = x_prompt.reshape(N_PROMPT, D_MODEL)
    xb = x_sample.reshape(N_SAMPLE, D_MODEL)
    merged = False
    new_caches = []
    for l in range(DEPTH):
        mod_l = mod[l]
        if l % 2 == 0:
            e = l // 2
            lam_init = 0.8 - 0.6 * math.exp(-0.3 * l)
            segments = (
                (A_Q_W, True, 0, 0, None),
                (A_KV_W, True, 1, 0, 2),
                (A_KV_W, False, 1, A_KV_W, 3),
                (B_QK_W, True, 0, A_Q_W, None),
                (B_QK_W, True, 1, 2 * A_KV_W, 4),
                (B_V_W, False, 1, 2 * A_KV_W + B_QK_W, 5),
            )
            hg = _head_gain((
                (A_Q_W, a_q_norm[e], scale), (A_KV_W, a_k_norm[e], 1.0), (A_KV_W, None, 1.0),
                (B_QK_W, b_q_norm[e], scale), (B_QK_W, b_k_norm[e], 1.0), (B_V_W, None, 1.0)))
            outs = _lnproj(xa, xb, merged, mod_l, norm_mix[l], w_in_ab[e].astype(BF16), hg, _chunk_plan(segments),
                           A_Q_W + B_QK_W, 2 * A_KV_W + B_QK_W + B_V_W,
                           (A_KV_W, A_KV_W, B_QK_W, B_V_W), True)
            q, kv, ak, av, bk, bv = outs
            new_caches.append((
                ak.reshape(BATCH, SEQ, A_KV_HEADS, HEAD_DIM), av.reshape(BATCH, SEQ, A_KV_HEADS, HEAD_DIM),
                bk.reshape(BATCH, SEQ, B_HEADS, 2, HEAD_DIM), bv.reshape(BATCH, SEQ, B_HEADS, B_V_DIM)))
            lamv = jnp.stack([lam_q1[e], lam_k1[e], lam_q2[e], lam_k2[e]]).astype(F32)
            subln = b_subln[e].reshape(1, B_V_DIM)
            o_p = _attn_ab(q, kv, lamv, subln, lam_init, None)
            caches = (cache_a_k[:, e].reshape(DEC_BATCH, PAST_LEN, A_KV_W),
                      cache_a_v[:, e].reshape(DEC_BATCH, PAST_LEN, A_KV_W),
                      cache_b_k[:, e].reshape(DEC_BATCH, PAST_LEN, B_QK_W),
                      cache_b_v[:, e].reshape(DEC_BATCH, PAST_LEN, B_V_W))
            o_s = _attn_ab(q, kv, lamv, subln, lam_init, caches)
            w_out = w_out_ab[e].astype(BF16)
        else:
            oi = l // 2
            segments = (
                (C_WIDTH, True, 0, 0, None),
                (C_WIDTH, True, 1, 0, 2),
                (C_WIDTH, False, 1, C_WIDTH, 3),
            )
            hg = _head_gain(((C_WIDTH, c_q_norm[oi], scale), (C_WIDTH, c_k_norm[oi], 1.0), (C_WIDTH, None, 1.0)))
            outs = _lnproj(xa, xb, merged, mod_l, norm_mix[l], w_in_c[oi].astype(BF16), hg, _chunk_plan(segments),
                           C_WIDTH, 2 * C_WIDTH, (C_WIDTH, C_WIDTH), False)
            q, kv, ck_new, cv_new = outs
            new_caches.append((ck_new.reshape(BATCH, SEQ, C_HEADS, HEAD_DIM),
                               cv_new.reshape(BATCH, SEQ, C_HEADS, HEAD_DIM)))
            o_p = _attn_c_prompt(q, kv)
            o_s = _attn_na(q, kv,
                           cache_c_k[:, oi].reshape(DEC_BATCH, PAST_LEN, C_WIDTH),
                           cache_c_v[:, oi].reshape(DEC_BATCH, PAST_LEN, C_WIDTH),
                           _na_bias_table(c_rpb[oi]))
            w_out = w_out_c[oi].astype(BF16)

        x1, h2, gates, rank = _outproj(o_p, o_s, xa, xb, merged, w_out, mod_l, norm_ffn[l], w_router, b_router)
        dests, weights, tile_expert, n_tiles, next_expert, clear = _route_plan(gates, rank)
        xs = _moe_scatter(dests, clear, h2)
        y = _moe_mlp(l, tile_expert, n_tiles, next_expert, xs, w_gate, w_up, w_down)
        last = l == DEPTH - 1
        out = _moe_combine(dests, x1, weights, mod_l, y, last)
        if last:
            y_prompt, y_sample = out
        else:
            xa = xb = out
            merged = True

    even = [nc for i, nc in enumerate(new_caches) if i % 2 == 0]
    odd = [nc for i, nc in enumerate(new_caches) if i % 2 == 1]
    stack = lambda items, k: jnp.stack([it[k] for it in items], axis=1)
    return (y_prompt.reshape(BATCH, SEQ, D_MODEL), y_sample.reshape(DEC_BATCH, DEC_SEQ, D_MODEL),
            stack(even, 0), stack(even, 1), stack(even, 2), stack(even, 3),
            stack(odd, 0), stack(odd, 1))
```

```python
import functools
import math

import numpy as np
import jax
import jax.numpy as jnp
from jax import lax
from jax.experimental import pallas as pl
from jax.experimental.pallas import tpu as pltpu

F32 = jnp.float32
BF16 = jnp.bfloat16

D_MODEL = 1024
BATCH = 16
SEQ = 256
DEPTH = 2
DEC_BATCH = 8
DEC_SEQ = 1024
PAST_LEN = 256
GRID_W = 64
HEAD_DIM = 64
ROPE_THETA = 10000.0
RMS_EPS = 1e-6
A_Q_HEADS = 8
A_KV_HEADS = 2
A_GROUP = A_Q_HEADS // A_KV_HEADS
B_HEADS = 4
B_V_DIM = 2 * HEAD_DIM
A_Q_W = A_Q_HEADS * HEAD_DIM
A_KV_W = A_KV_HEADS * HEAD_DIM
B_QK_W = B_HEADS * 2 * HEAD_DIM
B_V_W = B_HEADS * B_V_DIM
C_HEADS = 16
C_WIDTH = C_HEADS * HEAD_DIM
NA_ROWS = 8
NA_COLS = 16
N_EXPERTS = 16
N_GROUPS = 4
EXPERTS_PER_GROUP = N_EXPERTS // N_GROUPS
D_EXPERT = 1024

LANES = 128
SUBLANES = 8
ROW_TILES = D_MODEL // LANES
N_PROMPT = BATCH * SEQ
N_SAMPLE = DEC_BATCH * DEC_SEQ
N_TOK = N_PROMPT + N_SAMPLE
TM = 512
N_TILES = N_TOK // TM
N_PROMPT_TILES = N_PROMPT // TM
TM_LN = 512
LN_PIECE = 512
TM_OUT = 512
COND_ROWS = 16
CTX_COND_ROW = DEC_BATCH
MOE_TM = 256
MOE_ROWS = 2 * N_TOK + N_EXPERTS * MOE_TM
MOE_TILES = MOE_ROWS // MOE_TM
GRID_ROWS = DEC_SEQ // GRID_W
PROMPTS_PER_STEP = 2
NA_ROWS_PER_STEP = 4
NEG_BIG = -1e30
LOG2E = math.log2(math.e)
VMEM_LIMIT = 56 * 1024 * 1024

def _silu(x):
    return x * (1.0 / (1.0 + jnp.exp(-x)))


def _dot(a, b):
    return jnp.dot(a, b, preferred_element_type=F32)


def _dot_nt(a, b):
    return lax.dot_general(a, b, (((1,), (1,)), ((), ())), preferred_element_type=F32)


def _low_half(shape):
    return lax.broadcasted_iota(jnp.int32, shape, len(shape) - 1) < HEAD_DIM


def _swap_halves(x):
    return jnp.concatenate([x[:, HEAD_DIM:], x[:, :HEAD_DIM]], axis=1)


def _tile_cond(tm):
    tiles = np.arange(N_TOK // tm)
    npt = N_PROMPT // tm
    samp = np.maximum(tiles - npt, 0)
    per_seq = DEC_SEQ // tm
    cond = np.where(tiles < npt, CTX_COND_ROW, samp // per_seq)
    rope = np.where(tiles < npt, per_seq, samp % per_seq)
    return jnp.asarray(cond, jnp.int32), jnp.asarray(rope, jnp.int32)


def _split_specs(tm, merged):
    npt = N_PROMPT // tm
    base = npt if merged else 0
    a = pl.BlockSpec((tm, D_MODEL), lambda i, *_: (jnp.minimum(i, npt - 1), 0))
    b = pl.BlockSpec((tm, D_MODEL), lambda i, *_: (jnp.maximum(i - npt, 0) + base, 0))
    return a, b


def _mod_kernel(c_ref, w_ref, b_ref, o_ref):
    s = _silu(c_ref[...])
    s_hi = s.astype(BF16)
    s_lo = (s - s_hi.astype(F32)).astype(BF16)
    w = w_ref[...].astype(BF16)
    o_ref[...] = _dot(s_hi, w) + _dot(s_lo, w) + b_ref[...]


def _modulation(cond, w_mod, b_mod):
    tn = 1536
    return pl.pallas_call(
        _mod_kernel,
        out_shape=jax.ShapeDtypeStruct((DEPTH, COND_ROWS, 6 * D_MODEL), F32),
        grid=(DEPTH, 6 * D_MODEL // tn),
        in_specs=[
            pl.BlockSpec((COND_ROWS, D_MODEL), lambda l, j: (0, 0)),
            pl.BlockSpec((None, D_MODEL, tn), lambda l, j: (l, 0, j)),
            pl.BlockSpec((None, 1, tn), lambda l, j: (l, 0, j)),
        ],
        out_specs=pl.BlockSpec((None, COND_ROWS, tn), lambda l, j: (l, 0, j)),
        compiler_params=pltpu.CompilerParams(
            dimension_semantics=("arbitrary", "arbitrary"), vmem_limit_bytes=VMEM_LIMIT),
        name="modulation",
    )(cond, w_mod, b_mod.reshape(DEPTH, 1, 6 * D_MODEL))


def _rope_tables(tm):
    pos = np.arange(DEC_SEQ)
    rows = (pos // GRID_W).astype(np.float64)
    cols = (pos % GRID_W).astype(np.float64)
    nfreq = HEAD_DIM // 4
    inv = ROPE_THETA ** (-np.arange(nfreq, dtype=np.float64) / nfreq)
    d = np.arange(HEAD_DIM)
    dd = d % (HEAD_DIM // 2)
    p = np.where((d >= HEAD_DIM // 2)[None, :], cols[:, None], rows[:, None])
    ang = p * inv[dd % nfreq][None, :]
    cos, sin = np.cos(ang), np.sin(ang)
    second = (dd >= nfreq)[None, :]
    sa = np.where(second, sin, 0.0)
    sb = np.where(second, 0.0, -sin)

    def full(t, ident):
        t = np.concatenate([t, np.full((tm, HEAD_DIM), ident)], axis=0)
        return jnp.asarray(np.tile(t, (1, LANES // HEAD_DIM)), dtype=F32)

    return full(cos, 1.0), full(sa, 0.0), full(sb, 0.0)


def _lnproj_kernel(tile_cond_ref, tile_rope_ref, xa_ref, xb_ref, mod_ref, g_ref, w_ref, hg_ref, gmat_ref,
                   cos_ref, sa_ref, sb_ref, *out_refs, chunks, use_rope):
    i = pl.program_id(0)
    row = tile_cond_ref[i]
    x = jnp.where(i >= N_SAMPLE // TM_LN, xa_ref[...], xb_ref[...])
    ms = jnp.mean(x * x, axis=-1, keepdims=True)
    xn = x * lax.rsqrt(ms + RMS_EPS) * g_ref[...]
    shift = mod_ref[pl.ds(row, 1), 0:D_MODEL]
    scale = mod_ref[pl.ds(row, 1), D_MODEL:2 * D_MODEL]
    h = (xn * (1.0 + scale) + shift).astype(BF16)
    piece_chunks = LN_PIECE // LANES
    for c0 in range(0, len(chunks), piece_chunks):
        n_sub = min(piece_chunks, len(chunks) - c0)
        yp = _dot(h, w_ref[:, c0 * LANES:(c0 + n_sub) * LANES])
        yn = [None] * n_sub
        for s0 in range(0, n_sub, 2):
            if chunks[c0 + s0][0] or chunks[c0 + s0 + 1][0]:
                y2 = yp[:, s0 * LANES:(s0 + 2) * LANES]
                gs = _dot((y2 * y2).astype(BF16), gmat_ref[...])
                yn2 = y2 * lax.rsqrt(gs * (1.0 / HEAD_DIM) + RMS_EPS)
                yn[s0], yn[s0 + 1] = yn2[:, :LANES], yn2[:, LANES:]
        for sub in range(n_sub):
            c = c0 + sub
            normed, dst, dst_col, f32_dst, f32_col = chunks[c]
            if normed:
                y = yn[sub] * hg_ref[:, c * LANES:(c + 1) * LANES]
            else:
                y = yp[:, sub * LANES:(sub + 1) * LANES]
            if f32_dst is not None:
                out_refs[f32_dst][:, f32_col:f32_col + LANES] = y
            if normed and use_rope:
                y = (y * cos_ref[...] + pltpu.roll(y, HEAD_DIM // 4, 1) * sa_ref[...]
                     + pltpu.roll(y, LANES - HEAD_DIM // 4, 1) * sb_ref[...])
            out_refs[dst][:, dst_col:dst_col + LANES] = y.astype(BF16)


def _lnproj(xa, xb, merged, mod_l, gain, w_bf16, head_gain, chunks, q_w, kv_w, f32_widths, use_rope):
    dout = w_bf16.shape[1]
    tm = TM_LN
    npt = N_PROMPT // tm
    nst = N_SAMPLE // tm
    cond_tok, rope_tok = _tile_cond(tm)
    order = np.concatenate([np.arange(npt, npt + nst), np.arange(npt)])
    tile_cond, tile_rope = cond_tok[order], rope_tok[order]
    cos, sa, sb = _rope_tables(tm)
    gmat = jnp.asarray(np.kron(np.eye(2 * LANES // HEAD_DIM), np.ones((HEAD_DIM, HEAD_DIM))), dtype=BF16)
    const = lambda i, *_: (0, 0)
    tok = lambda i, *_: (jnp.where(i < nst, i + npt, i - nst), 0)
    rope_map = lambda i, tc, tr: (tr[i], 0)
    prm = lambda i, *_: (jnp.maximum(i - nst, 0), 0)
    base = npt if merged else 0
    xa_spec = pl.BlockSpec((tm, D_MODEL), prm)
    xb_spec = pl.BlockSpec((tm, D_MODEL), lambda i, *_: (jnp.minimum(i, nst - 1) + base, 0))
    out_shape = [jax.ShapeDtypeStruct((N_TOK, q_w), BF16), jax.ShapeDtypeStruct((N_TOK, kv_w), BF16)]
    out_specs = [pl.BlockSpec((tm, q_w), tok), pl.BlockSpec((tm, kv_w), tok)]
    for wd in f32_widths:
        out_shape.append(jax.ShapeDtypeStruct((N_PROMPT, wd), F32))
        out_specs.append(pl.BlockSpec((tm, wd), prm))
    return pl.pallas_call(
        functools.partial(_lnproj_kernel, chunks=chunks, use_rope=use_rope),
        out_shape=out_shape,
        grid_spec=pltpu.PrefetchScalarGridSpec(
            num_scalar_prefetch=2,
            grid=(N_TOK // tm,),
            in_specs=[
                xa_spec, xb_spec,
                pl.BlockSpec((COND_ROWS, 6 * D_MODEL), const),
                pl.BlockSpec((1, D_MODEL), const),
                pl.BlockSpec((D_MODEL, dout), const),
                pl.BlockSpec((1, dout), const),
                pl.BlockSpec((2 * LANES, 2 * LANES), const),
                pl.BlockSpec((tm, LANES), rope_map),
                pl.BlockSpec((tm, LANES), rope_map),
                pl.BlockSpec((tm, LANES), rope_map),
            ],
            out_specs=out_specs,
        ),
        compiler_params=pltpu.CompilerParams(
            dimension_semantics=("arbitrary",), vmem_limit_bytes=VMEM_LIMIT),
        name="lnproj",
    )(tile_cond, tile_rope, xa, xb, mod_l, gain.reshape(1, D_MODEL), w_bf16, head_gain, gmat, cos, sa, sb)


def _softmax_parts(s, want_sum=True):
    p = jnp.exp2(s - jnp.max(s, axis=-1, keepdims=True))
    return p, (jnp.sum(p, axis=-1, keepdims=True) if want_sum else None)


def _gqa_pairs(q_ref, q_col0, kv_ref, k_col0, v_col0, n_kv, group, tq):
    low = _low_half((tq, LANES))
    mxu_sums = group > 1
    head_out = [None] * (n_kv * group)
    jobs = [(pair, half) for pair in range(n_kv // 2) for half in range(2)]
    scores = []
    for pair, half in jobs:
        k_pair = kv_ref[:, k_col0 + pair * LANES:k_col0 + (pair + 1) * LANES]
        kvh = 2 * pair + half
        keep = low if half == 0 else jnp.logical_not(low)
        qs = []
        for g in range(group):
            head = kvh * group + g
            blk = q_ref[:, q_col0 + (head // 2) * LANES:q_col0 + (head // 2 + 1) * LANES]
            if head % 2 != half:
                blk = _swap_halves(blk)
            qs.append(jnp.where(keep, blk, jnp.zeros_like(blk)))
        q = qs[0] if group == 1 else jnp.concatenate(qs, axis=0)
        scores.append(_dot_nt(q, k_pair))

    def finish():
        for (pair, half), s in zip(jobs, scores):
            v = kv_ref[:, v_col0 + pair * LANES:v_col0 + (pair + 1) * LANES]
            kvh = 2 * pair + half
            if mxu_sums:
                own = _low_half(v.shape) if half == 0 else jnp.logical_not(_low_half(v.shape))
                v = jnp.where(own, v, jnp.ones_like(v))
            p, l = _softmax_parts(s, want_sum=not mxu_sums)
            o = _dot(p.astype(BF16), v)
            o = o * (1.0 / (pltpu.roll(o, HEAD_DIM, 1) if mxu_sums else l))
            for g in range(group):
                head = kvh * group + g
                og = o[g * tq:(g + 1) * tq]
                if head % 2 != half:
                    og = pltpu.roll(og, HEAD_DIM, 1)
                head_out[head] = og
        return [jnp.where(low, head_out[2 * k], head_out[2 * k + 1]) for k in range(n_kv * group // 2)]

    return finish


def _attn_ab_kernel(*refs, has_cache, tq, n_seq, lam_init):
    if has_cache:
        q_ref, kvn_ref, cak_ref, cav_ref, cbk_ref, cbv_ref, lamv_ref, subln_ref, o_ref, kv_ref = refs

        @pl.when(pl.program_id(1) == 0)
        def _():
            col = 0
            for c_ref in (cak_ref, cav_ref, cbk_ref, cbv_ref):
                w = c_ref.shape[1]
                kv_ref[0:PAST_LEN, col:col + w] = c_ref[...].astype(BF16)
                col += w
            kv_ref[PAST_LEN:, :] = kvn_ref[...]
    else:
        q_ref, kv_ref, lamv_ref, subln_ref, o_ref = refs

    lv = lamv_ref[...]
    l1 = jnp.sum(lv[0:1] * lv[1:2], axis=-1, keepdims=True)
    l2 = jnp.sum(lv[2:3] * lv[3:4], axis=-1, keepdims=True)
    lam = jnp.exp(l1) - jnp.exp(l2) + lam_init
    bk0 = 2 * A_KV_W
    bv0 = bk0 + B_QK_W
    low = _low_half((tq, LANES))
    t_kv = kv_ref.shape[0] // n_seq
    for b in range(n_seq):
        q_b = q_ref.at[b * tq:(b + 1) * tq]
        kv_b = kv_ref.at[b * t_kv:(b + 1) * t_kv]
        finish_a = _gqa_pairs(q_b, 0, kv_b, 0, A_KV_W, A_KV_HEADS, A_GROUP, tq)
        b_scores = []
        for h in range(B_HEADS):
            lanes = slice(h * LANES, (h + 1) * LANES)
            qp = q_b[:, A_Q_W + lanes.start:A_Q_W + lanes.stop]
            zero = jnp.zeros_like(qp)
            q = jnp.concatenate([jnp.where(low, qp, zero), jnp.where(low, zero, qp)], axis=0)
            b_scores.append(_dot_nt(q, kv_b[:, bk0 + lanes.start:bk0 + lanes.stop]))
        outs = finish_a()
        for h in range(B_HEADS):
            lanes = slice(h * LANES, (h + 1) * LANES)
            p, l = _softmax_parts(b_scores[h])
            r = 1.0 / l
            a = p[:tq] * r[:tq] - p[tq:] * (lam * r[tq:])
            o = _dot(a.astype(BF16), kv_b[:, bv0 + lanes.start:bv0 + lanes.stop])
            ms = jnp.mean(o * o, axis=-1, keepdims=True)
            o = o * lax.rsqrt(ms + RMS_EPS) * subln_ref[...] * (1.0 - lam_init)
            outs.append(o)
        o_ref[b * tq:(b + 1) * tq, :] = jnp.concatenate(outs, axis=1).astype(BF16)


def _attn_ab(q, kv, lamv, subln, lam_init, caches):
    kv_w = kv.shape[1]
    kern = functools.partial(_attn_ab_kernel, lam_init=lam_init)
    const = lambda b, j: (0, 0)
    cp = pltpu.CompilerParams(dimension_semantics=("arbitrary", "arbitrary"), vmem_limit_bytes=VMEM_LIMIT)
    if caches is None:
        n_seq = 1
        rows = n_seq * SEQ
        return pl.pallas_call(
            functools.partial(kern, has_cache=False, tq=SEQ, n_seq=n_seq),
            out_shape=jax.ShapeDtypeStruct((N_PROMPT, D_MODEL), BF16),
            grid=(BATCH // n_seq, 1),
            in_specs=[
                pl.BlockSpec((rows, D_MODEL), lambda b, j: (b, 0)),
                pl.BlockSpec((rows, kv_w), lambda b, j: (b, 0)),
                pl.BlockSpec((4, HEAD_DIM), const),
                pl.BlockSpec((1, B_V_DIM), const),
            ],
            out_specs=pl.BlockSpec((rows, D_MODEL), lambda b, j: (b, 0)),
            compiler_params=cp,
            name="attn_ab_prompt",
        )(q, kv, lamv, subln)
    tq = 256
    nq = DEC_SEQ // tq
    q0 = N_PROMPT // tq
    kv0 = N_PROMPT // DEC_SEQ
    cak, cav, cbk, cbv = caches
    cspec = lambda w: pl.BlockSpec((None, PAST_LEN, w), lambda b, j: (b, 0, 0))
    return pl.pallas_call(
        functools.partial(kern, has_cache=True, tq=tq, n_seq=1),
        out_shape=jax.ShapeDtypeStruct((N_SAMPLE, D_MODEL), BF16),
        grid=(DEC_BATCH, nq),
        in_specs=[
            pl.BlockSpec((tq, D_MODEL), lambda b, j: (q0 + b * nq + j, 0)),
            pl.BlockSpec((DEC_SEQ, kv_w), lambda b, j: (kv0 + b, 0)),
            cspec(A_KV_W), cspec(A_KV_W), cspec(B_QK_W), cspec(B_V_W),
            pl.BlockSpec((4, HEAD_DIM), const),
            pl.BlockSpec((1, B_V_DIM), const),
        ],
        out_specs=pl.BlockSpec((tq, D_MODEL), lambda b, j: (b * nq + j, 0)),
        scratch_shapes=[pltpu.VMEM((PAST_LEN + DEC_SEQ, kv_w), BF16)],
        compiler_params=cp,
        name="attn_ab_sample",
    )(q, kv, cak, cav, cbk, cbv, lamv, subln)


def _attn_c_prompt_kernel(q_ref, kv_ref, o_ref):
    for b in range(PROMPTS_PER_STEP):
        rows = slice(b * SEQ, (b + 1) * SEQ)
        outs = _gqa_pairs(q_ref.at[rows], 0, kv_ref.at[rows], 0, C_WIDTH, C_HEADS, 1, SEQ)()
        o_ref[rows, :] = jnp.concatenate(outs, axis=1).astype(BF16)


def _attn_c_prompt(q, kv):
    rows = PROMPTS_PER_STEP * SEQ
    return pl.pallas_call(
        _attn_c_prompt_kernel,
        out_shape=jax.ShapeDtypeStruct((N_PROMPT, D_MODEL), BF16),
        grid=(BATCH // PROMPTS_PER_STEP,),
        in_specs=[
            pl.BlockSpec((rows, C_WIDTH), lambda b: (b, 0)),
            pl.BlockSpec((rows, 2 * C_WIDTH), lambda b: (b, 0)),
        ],
        out_specs=pl.BlockSpec((rows, C_WIDTH), lambda b: (b, 0)),
        compiler_params=pltpu.CompilerParams(dimension_semantics=("arbitrary",), vmem_limit_bytes=VMEM_LIMIT),
        name="attn_c_prompt",
    )(q, kv)


def _na_kernel(q_ref, kv_ref, ck_ref, cv_ref, tp_ref, o_ref, ckb, cvb):
    step = pl.program_id(1)

    @pl.when(step == 0)
    def _():
        ckb[...] = ck_ref[...].astype(BF16)
        cvb[...] = cv_ref[...].astype(BF16)

    kh = min(NA_ROWS, GRID_ROWS)
    win = kh * GRID_W
    low = _low_half((GRID_W, LANES))
    n_pair = C_HEADS // 2
    for rr in range(NA_ROWS_PER_STEP):
        r = step * NA_ROWS_PER_STEP + rr
        rows = slice(rr * GRID_W, (rr + 1) * GRID_W)
        rs = jnp.clip(r - kh // 2, 0, GRID_ROWS - kh)
        ro0 = rs - r + (NA_ROWS - 1)
        start = pl.multiple_of(rs * GRID_W, GRID_W)
        s_lat, s_ctx = [], []
        for j in range(n_pair):
            lanes = slice(j * LANES, (j + 1) * LANES)
            qp = q_ref[rows, lanes]
            zero = jnp.zeros_like(qp)
            q = jnp.concatenate([jnp.where(low, qp, zero), jnp.where(low, zero, qp)], axis=0)
            kw = kv_ref[pl.ds(start, win), lanes]
            bias = jnp.concatenate(
                [jnp.concatenate([tp_ref[2 * j + hh, ro0 + 2 * t] for t in range(kh // 2)], axis=1)
                 for hh in range(2)], axis=0)
            s_lat.append(_dot_nt(q, kw) + bias)
            s_ctx.append(_dot_nt(q, ckb[:, lanes]))
        outs = []
        for j in range(n_pair):
            lanes = slice(j * LANES, (j + 1) * LANES)
            s_l, s_c = s_lat[j], s_ctx[j]
            m = jnp.maximum(jnp.max(s_l, axis=-1, keepdims=True), jnp.max(s_c, axis=-1, keepdims=True))
            p_l = jnp.exp2(s_l - m)
            p_c = jnp.exp2(s_c - m)
            l = jnp.sum(p_l, axis=-1, keepdims=True) + jnp.sum(p_c, axis=-1, keepdims=True)
            vw = kv_ref[pl.ds(start, win), C_WIDTH + lanes.start:C_WIDTH + lanes.stop]
            o = _dot(p_c.astype(BF16), cvb[:, lanes]) + _dot(p_l.astype(BF16), vw)
            o = o * (1.0 / l)
            outs.append(jnp.where(low, o[:GRID_W], o[GRID_W:]))
        o_ref[rows, :] = jnp.concatenate(outs, axis=1).astype(BF16)


NA_BIAS_PAD = GRID_W - NA_COLS


def _na_bias_kernel(w_ref, mask_ref, tp_ref):
    low = _low_half((GRID_W, LANES))
    keep = mask_ref[...] > 0.0
    n_off = 2 * NA_ROWS - 1
    left, right = [], []
    for ro in range(n_off):
        row = jnp.broadcast_to(w_ref[ro:ro + 1, :] * LOG2E, (GRID_W, LANES))
        left.append(pltpu.roll(row, LANES - GRID_W + 1, 1, stride=1, stride_axis=0))
        right.append(pltpu.roll(row, 1, 1, stride=1, stride_axis=0))
    for t in range(n_off - 1):
        tp_ref[t] = jnp.where(keep, jnp.where(low, left[t], right[t + 1]), NEG_BIG)


def _na_bias_table(rpb):
    cols = np.arange(GRID_W)
    col_start = np.clip(cols - NA_COLS // 2, 0, GRID_W - NA_COLS)
    col_in = (cols[None, :] >= col_start[:, None]) & (cols[None, :] < col_start[:, None] + NA_COLS)
    assert np.abs((cols[None, :] - cols[:, None])[col_in]).max() <= NA_COLS - 1
    mask = jnp.asarray(np.tile(col_in, (1, LANES // GRID_W)), dtype=F32)
    n_off = 2 * NA_ROWS - 1
    n_rel = 2 * NA_COLS - 1
    w = jnp.pad(rpb.astype(F32), ((0, 0), (0, 0), (NA_BIAS_PAD, LANES - NA_BIAS_PAD - n_rel)))
    return pl.pallas_call(
        _na_bias_kernel,
        out_shape=jax.ShapeDtypeStruct((C_HEADS, n_off - 1, GRID_W, LANES), F32),
        grid=(C_HEADS,),
        in_specs=[
            pl.BlockSpec((None, n_off, LANES), lambda h: (h, 0, 0)),
            pl.BlockSpec((GRID_W, LANES), lambda h: (0, 0)),
        ],
        out_specs=pl.BlockSpec((None, n_off - 1, GRID_W, LANES), lambda h: (h, 0, 0, 0)),
        compiler_params=pltpu.CompilerParams(dimension_semantics=("arbitrary",)),
        name="na_bias",
    )(w, mask)


def _attn_na(q, kv, ck, cv, tp):
    tq = NA_ROWS_PER_STEP * GRID_W
    steps = GRID_ROWS // NA_ROWS_PER_STEP
    q0 = N_PROMPT // tq
    kv0 = N_PROMPT // DEC_SEQ
    return pl.pallas_call(
        _na_kernel,
        out_shape=jax.ShapeDtypeStruct((N_SAMPLE, D_MODEL), BF16),
        grid=(DEC_BATCH, steps),
        in_specs=[
            pl.BlockSpec((tq, C_WIDTH), lambda b, r: (q0 + b * steps + r, 0)),
            pl.BlockSpec((DEC_SEQ, 2 * C_WIDTH), lambda b, r: (kv0 + b, 0)),
            pl.BlockSpec((None, PAST_LEN, C_WIDTH), lambda b, r: (b, 0, 0)),
            pl.BlockSpec((None, PAST_LEN, C_WIDTH), lambda b, r: (b, 0, 0)),
            pl.BlockSpec(tp.shape, lambda b, r: (0, 0, 0, 0)),
        ],
        out_specs=pl.BlockSpec((tq, C_WIDTH), lambda b, r: (b * steps + r, 0)),
        scratch_shapes=[pltpu.VMEM((PAST_LEN, C_WIDTH), BF16), pltpu.VMEM((PAST_LEN, C_WIDTH), BF16)],
        compiler_params=pltpu.CompilerParams(
            dimension_semantics=("arbitrary", "arbitrary"), vmem_limit_bytes=VMEM_LIMIT),
        name="attn_na_sample",
    )(q, kv, ck, cv, tp)


def _first_wins_ranks(vals):
    ranks = []
    for i in range(len(vals)):
        r = jnp.zeros_like(vals[i])
        for j in range(len(vals)):
            if j == i:
                continue
            beats = (vals[j] >= vals[i]) if j < i else (vals[j] > vals[i])
            r = r + jnp.where(beats, 1.0, 0.0)
        ranks.append(r)
    return ranks


def _outproj_kernel(tile_cond_ref, op_ref, os_ref, xa_ref, xb_ref, w_ref, mod_ref, g2_ref, wrt_ref, br_ref,
                    tri_ref, x1_ref, h2_ref, gates_ref, rank_ref, carry_ref):
    i = pl.program_id(0)
    row = tile_cond_ref[i]
    is_prompt = i < N_PROMPT // TM_OUT

    @pl.when(i == 0)
    def _():
        carry_ref[...] = jnp.zeros_like(carry_ref)

    o = jnp.where(is_prompt, op_ref[...], os_ref[...])
    x = jnp.where(is_prompt, xa_ref[...], xb_ref[...])
    acc = _dot(o, w_ref[...])
    gate = mod_ref[pl.ds(row, 1), 2 * D_MODEL:3 * D_MODEL]
    x1 = x + gate * acc
    x1_ref[...] = x1
    ms = jnp.mean(x1 * x1, axis=-1, keepdims=True)
    xn = x1 * lax.rsqrt(ms + RMS_EPS) * g2_ref[...]
    shift = mod_ref[pl.ds(row, 1), 3 * D_MODEL:4 * D_MODEL]
    scale = mod_ref[pl.ds(row, 1), 4 * D_MODEL:5 * D_MODEL]
    h2 = xn * (1.0 + scale) + shift
    for s in range(ROW_TILES):
        h2_ref[pl.ds(s, TM_OUT, stride=ROW_TILES), :] = h2[:, s * LANES:(s + 1) * LANES]

    h_hi = h2.astype(BF16)
    h_lo = (h2 - h_hi.astype(F32)).astype(BF16)
    part = _dot_nt(wrt_ref[...], h_hi)
    logits = part[:N_EXPERTS] + part[N_EXPERTS:] + _dot_nt(wrt_ref[:N_EXPERTS, :], h_lo)
    e = jnp.exp(logits - jnp.max(logits, axis=0, keepdims=True))
    scores = e * (1.0 / jnp.sum(e, axis=0, keepdims=True))
    sel = scores + br_ref[...]
    sel_rows = [sel[k:k + 1, :] for k in range(N_EXPERTS)]
    in_top2 = []
    group_sum = []
    for g in range(N_GROUPS):
        vals = sel_rows[g * EXPERTS_PER_GROUP:(g + 1) * EXPERTS_PER_GROUP]
        ranks = _first_wins_ranks(vals)
        top = [rk < 2.0 for rk in ranks]
        in_top2.extend(top)
        s = jnp.zeros_like(vals[0])
        for v, t in zip(vals, top):
            s = s + jnp.where(t, v, 0.0)
        group_sum.append(s)
    group_rank = _first_wins_ranks(group_sum)
    mask_rows = []
    for k in range(N_EXPERTS):
        chosen = jnp.where(in_top2[k], 1.0, 0.0) * jnp.where(group_rank[k // EXPERTS_PER_GROUP] < 1.0, 1.0, 0.0)
        mask_rows.append(chosen)
    mask = jnp.concatenate(mask_rows, axis=0)
    picked = scores * mask
    gates_ref[...] = picked * (1.0 / jnp.sum(picked, axis=0, keepdims=True))
    prefix = _dot(mask.astype(BF16), tri_ref[...])
    rank_ref[...] = jnp.where(mask > 0.0, prefix + carry_ref[...], -1.0)
    carry_ref[...] = carry_ref[...] + jnp.sum(mask, axis=1, keepdims=True)


def _outproj(o_prompt, o_sample, xa, xb, merged, w_bf16, mod_l, gain2, w_router, b_router):
    const = lambda i, *_: (0, 0)
    tok = lambda i, *_: (i, 0)
    tokT = lambda i, *_: (0, i)
    tm = TM_OUT
    tile_cond, _ = _tile_cond(tm)
    op_spec, os_spec = _split_specs(tm, False)
    xa_spec, xb_spec = _split_specs(tm, merged)
    tri = jnp.asarray(np.triu(np.ones((tm, tm)), k=1), dtype=BF16)
    wrt = w_router.T.astype(F32)
    wrt_hi = wrt.astype(BF16)
    wrt_split = jnp.concatenate([wrt_hi, (wrt - wrt_hi.astype(F32)).astype(BF16)], axis=0)
    return pl.pallas_call(
        _outproj_kernel,
        out_shape=[
            jax.ShapeDtypeStruct((N_TOK, D_MODEL), F32),
            jax.ShapeDtypeStruct((N_TOK * ROW_TILES, LANES), F32),
            jax.ShapeDtypeStruct((N_EXPERTS, N_TOK), F32),
            jax.ShapeDtypeStruct((N_EXPERTS, N_TOK), F32),
        ],
        grid_spec=pltpu.PrefetchScalarGridSpec(
            num_scalar_prefetch=1,
            grid=(N_TOK // tm,),
            in_specs=[
                op_spec, os_spec, xa_spec, xb_spec,
                pl.BlockSpec((D_MODEL, D_MODEL), const),
                pl.BlockSpec((COND_ROWS, 6 * D_MODEL), const),
                pl.BlockSpec((1, D_MODEL), const),
                pl.BlockSpec((2 * N_EXPERTS, D_MODEL), const),
                pl.BlockSpec((N_EXPERTS, 1), const),
                pl.BlockSpec((tm, tm), const),
            ],
            out_specs=[
                pl.BlockSpec((tm, D_MODEL), tok),
                pl.BlockSpec((tm * ROW_TILES, LANES), tok),
                pl.BlockSpec((N_EXPERTS, tm), tokT),
                pl.BlockSpec((N_EXPERTS, tm), tokT),
            ],
            scratch_shapes=[pltpu.VMEM((N_EXPERTS, 1), F32)],
        ),
        compiler_params=pltpu.CompilerParams(
            dimension_semantics=("arbitrary",), vmem_limit_bytes=VMEM_LIMIT),
        name="outproj_router",
    )(tile_cond, o_prompt, o_sample, xa, xb, w_bf16, mod_l, gain2.reshape(1, D_MODEL), wrt_split,
      b_router.reshape(N_EXPERTS, 1), tri)


def _plan_kernel(gates_ref, rank_ref, dests_ref, wts_ref, misc_ref):
    rank = rank_ref[...]
    gates = gates_ref[...]
    sel = rank >= 0.0
    hit = jnp.where(sel, 1.0, 0.0)
    tile = float(MOE_TM)

    def pad_up(c):
        return jnp.floor((c + (tile - 1.0)) * (1.0 / tile)) * tile

    counts_col = jnp.sum(hit, axis=1, keepdims=True)
    ones = jnp.ones((SUBLANES, hit.shape[1]), BF16)
    counts_row = _dot_nt(ones, hit.astype(BF16))[0:1]
    padded_col = pad_up(counts_col)
    padded_row = pad_up(counts_row)
    e_sub = lax.broadcasted_iota(jnp.int32, (N_EXPERTS, N_EXPERTS), 0)
    e_lane = lax.broadcasted_iota(jnp.int32, (N_EXPERTS, N_EXPERTS), 1)
    ends_col = jnp.sum(jnp.where(e_lane <= e_sub, padded_row, 0.0), axis=1, keepdims=True)
    ends_row = jnp.sum(jnp.where(e_sub <= e_lane, padded_col, 0.0), axis=0, keepdims=True)
    dest = jnp.where(sel, ends_col - padded_col + rank, -1.0)
    d1 = jnp.max(dest, axis=0, keepdims=True)
    d0 = jnp.min(jnp.where(sel, dest, float(MOE_ROWS)), axis=0, keepdims=True)
    w0 = jnp.sum(jnp.where(dest == d0, gates, 0.0), axis=0, keepdims=True)
    w1 = jnp.sum(jnp.where(dest == d1, gates, 0.0), axis=0, keepdims=True)
    dests_ref[...] = jnp.concatenate([d0, d1], axis=0).astype(jnp.int32)
    wts_ref[...] = jnp.concatenate([w0, w1], axis=0)

    lane = lax.broadcasted_iota(jnp.int32, (1, LANES), 1).astype(F32)
    n_tiles = ends_row[:, N_EXPERTS - 1:N_EXPERTS] * (1.0 / tile)
    tile_expert = jnp.sum(jnp.where(lane * tile >= ends_col, 1.0, 0.0), axis=0, keepdims=True)
    tile_expert = jnp.minimum(tile_expert, float(N_EXPERTS - 1))
    e_col = lax.broadcasted_iota(jnp.int32, (N_EXPERTS, 1), 0).astype(F32)
    later = jnp.where((e_col > tile_expert) & (padded_col > 0.0), e_col, float(N_EXPERTS))
    nxt = jnp.min(later, axis=0, keepdims=True)
    nxt = jnp.where(nxt > float(N_EXPERTS - 1), tile_expert, nxt)
    used_end = ends_col - padded_col + counts_col
    tile_end = jnp.sum(jnp.where(e_col == tile_expert, used_end, 0.0), axis=0, keepdims=True)
    valid = jnp.clip(tile_end - lane * tile, 0.0, tile)
    ends_lane = jnp.sum(jnp.where(e_col <= lane, padded_col, 0.0), axis=0, keepdims=True)
    own = e_col == lane
    used_lane = ends_lane + jnp.sum(jnp.where(own, counts_col - padded_col, 0.0), axis=0, keepdims=True)
    is_expert = lane < float(N_EXPERTS)
    pad_lo = jnp.where(is_expert, used_lane, ends_lane)
    pad_hi = jnp.where(is_expert, ends_lane, float(MOE_ROWS))
    zero = jnp.zeros((1, LANES), F32)
    rows = [tile_expert, nxt, valid, n_tiles + zero, pad_lo, pad_hi] + [zero] * (SUBLANES - 6)
    misc_ref[...] = jnp.concatenate(rows, axis=0).astype(jnp.int32)


def _inverse_kernel(p0_ref, p1_ref, pad_lo_ref, pad_hi_ref, code_ref):
    def clear(p, c):
        code_ref[p] = 0
        return c

    for e in range(N_EXPERTS + 1):
        lax.fori_loop(pad_lo_ref[e], pad_hi_ref[e], clear, 0)

    def fill(n, c):
        code_ref[p0_ref[n]] = 2 * n
        code_ref[p1_ref[n]] = 2 * n + 1
        return c

    lax.fori_loop(0, N_TOK, fill, 0, unroll=16)


def _route_plan(gates, rank):
    assert MOE_TILES <= LANES and N_EXPERTS <= LANES
    full = lambda shape: pl.BlockSpec(shape, lambda: (0,) * len(shape))
    pos, wts, misc = pl.pallas_call(
        _plan_kernel,
        out_shape=[
            jax.ShapeDtypeStruct((2, N_TOK), jnp.int32),
            jax.ShapeDtypeStruct((2, N_TOK), F32),
            jax.ShapeDtypeStruct((SUBLANES, LANES), jnp.int32),
        ],
        in_specs=[full((N_EXPERTS, N_TOK)), full((N_EXPERTS, N_TOK))],
        out_specs=[full((2, N_TOK)), full((2, N_TOK)), full((SUBLANES, LANES))],
        compiler_params=pltpu.CompilerParams(vmem_limit_bytes=VMEM_LIMIT),
        name="route_plan",
    )(gates, rank)
    code = pl.pallas_call(
        _inverse_kernel,
        out_shape=jax.ShapeDtypeStruct((MOE_ROWS,), jnp.int32),
        grid_spec=pltpu.PrefetchScalarGridSpec(
            num_scalar_prefetch=4,
            grid=(1,),
            in_specs=[],
            out_specs=pl.BlockSpec(memory_space=pltpu.SMEM),
        ),
        name="route_inverse",
    )(pos[0], pos[1], misc[4, :N_EXPERTS + 1], misc[5, :N_EXPERTS + 1])
    tile_expert = misc[0, :MOE_TILES]
    next_expert = misc[1, :MOE_TILES]
    tile_valid = misc[2, :MOE_TILES]
    n_tiles = misc[3, :1]
    return code, wts.T, tile_expert, n_tiles, next_expert, tile_valid


def _mlp_kernel(te_ref, nv_ref, nxt_ref, valid_ref, code_ref, h_hbm, wg_hbm, wu_hbm, wd_hbm, slots_hbm,
                xbuf, ybuf, sg, su, sd, wgb, wub, wdb, hb, sem, gsem, ssem, *, layer):
    t = pl.program_id(0)
    nv = nv_ref[0]
    dummy0 = 2 * N_TOK
    spare0 = dummy0 + MOE_ROWS

    def fetch(e):
        return [pltpu.make_async_copy(w.at[layer, e], st, sem.at[k])
                for k, (w, st) in enumerate(((wg_hbm, sg), (wu_hbm, su), (wd_hbm, sd)))]

    def row(ref, tile_row):
        return ref.at[pl.ds(pl.multiple_of(tile_row * ROW_TILES, SUBLANES), ROW_TILES)]

    def gather(tile, r, sl):
        tok = lax.shift_right_logical(code_ref[tile * MOE_TM + r], 1)
        return pltpu.make_async_copy(row(h_hbm, tok), row(xbuf.at[sl], r), gsem.at[sl])

    def scatter(tile, r, sl, live):
        p = tile * MOE_TM + r
        dst = jnp.where(r < valid_ref[tile], code_ref[p], dummy0 + p)
        dst = jnp.where(live, dst, spare0 + r)
        return pltpu.make_async_copy(row(ybuf.at[sl], r), row(slots_hbm, dst), ssem.at[sl])

    def wait_rows(copy_fn):
        for _ in range(MOE_TM):
            copy_fn().wait()

    @pl.when(t < nv)
    def _():
        e = te_ref[t]
        prev = te_ref[jnp.maximum(t - 1, 0)]
        cur = t % 2
        tile_next = jnp.minimum(t + 1, nv - 1)
        tile_prev = jnp.maximum(t - 1, 0)

        @pl.when(t == 0)
        def _():
            for cp in fetch(e):
                cp.start()
            ybuf[...] = jnp.zeros_like(ybuf)

            def first(r, c):
                gather(0, r, 0).start(priority=0)
                pltpu.make_async_copy(row(ybuf.at[0], r), row(slots_hbm, spare0 + MOE_TM + r), ssem.at[0]).start(
                    priority=1)
                return c

            lax.fori_loop(0, MOE_TM, first, 0, unroll=8)

        @pl.when((t == 0) | (e != prev))
        def _():
            for cp in fetch(e):
                cp.wait()
            for src, dst in ((sg, wgb), (su, wub), (sd, wdb)):
                for r0 in range(0, src.shape[0], 128):
                    dst[r0:r0 + 128, :] = src[r0:r0 + 128, :].astype(BF16)

            @pl.when(nxt_ref[t] != e)
            def _():
                for cp in fetch(nxt_ref[t]):
                    cp.start()

        wait_rows(lambda: pltpu.make_async_copy(row(h_hbm, 0), row(xbuf.at[cur], 0), gsem.at[cur]))
        x = jnp.concatenate(
            [xbuf[cur, pl.ds(s, MOE_TM, stride=ROW_TILES), :] for s in range(ROW_TILES)], axis=1).astype(BF16)
        step = 512
        n_chunks = D_EXPERT // step
        per_chunk = MOE_TM // n_chunks
        for k in range(n_chunks):
            c = k * step
            g = _dot(x, wgb[:, c:c + step])
            u = _dot(x, wub[:, c:c + step])
            hb[:, c:c + step] = (_silu(g) * u).astype(BF16)
            for r in range(k * per_chunk, (k + 1) * per_chunk):
                gather(tile_next, r, 1 - cur).start(priority=0)
                scatter(tile_prev, r, 1 - cur, t > 0).start(priority=1)
        y = _dot(hb[...], wdb[...])
        wait_rows(lambda: pltpu.make_async_copy(row(ybuf.at[cur], 0), row(slots_hbm, 0), ssem.at[cur]))
        for s in range(ROW_TILES):
            ybuf[cur, pl.ds(s, MOE_TM, stride=ROW_TILES), :] = y[:, s * LANES:(s + 1) * LANES]

        @pl.when(t == nv - 1)
        def _():
            def last(r, c):
                scatter(t, r, cur, True).start()
                return c

            lax.fori_loop(0, MOE_TM, last, 0, unroll=8)

            def drain(r, c):
                pltpu.make_async_copy(row(ybuf.at[cur], 0), row(slots_hbm, 0), ssem.at[cur]).wait()
                pltpu.make_async_copy(row(ybuf.at[1 - cur], 0), row(slots_hbm, 0), ssem.at[1 - cur]).wait()
                pltpu.make_async_copy(row(h_hbm, 0), row(xbuf.at[1 - cur], 0), gsem.at[1 - cur]).wait()
                return c

            lax.fori_loop(0, MOE_TM, drain, 0, unroll=8)


SLOT_ROWS = 2 * N_TOK + MOE_ROWS + 2 * MOE_TM


def _moe_mlp(layer, tile_expert, n_tiles, next_expert, tile_valid, code, h2, w_gate, w_up, w_down):
    assert D_MODEL == D_EXPERT
    hbm = pl.BlockSpec(memory_space=pl.ANY)
    tile_buf = pltpu.VMEM((2, MOE_TM * ROW_TILES, LANES), F32)
    return pl.pallas_call(
        functools.partial(_mlp_kernel, layer=layer),
        out_shape=jax.ShapeDtypeStruct((SLOT_ROWS * ROW_TILES, LANES), F32),
        grid_spec=pltpu.PrefetchScalarGridSpec(
            num_scalar_prefetch=5,
            grid=(MOE_TILES,),
            in_specs=[hbm, hbm, hbm, hbm],
            out_specs=hbm,
            scratch_shapes=[
                tile_buf, tile_buf,
                pltpu.VMEM((D_MODEL, D_EXPERT), F32),
                pltpu.VMEM((D_MODEL, D_EXPERT), F32),
                pltpu.VMEM((D_EXPERT, D_MODEL), F32),
                pltpu.VMEM((D_MODEL, D_EXPERT), BF16),
                pltpu.VMEM((D_MODEL, D_EXPERT), BF16),
                pltpu.VMEM((D_EXPERT, D_MODEL), BF16),
                pltpu.VMEM((MOE_TM, D_EXPERT), BF16),
                pltpu.SemaphoreType.DMA((3,)),
                pltpu.SemaphoreType.DMA((2,)),
                pltpu.SemaphoreType.DMA((2,)),
            ],
        ),
        compiler_params=pltpu.CompilerParams(
            dimension_semantics=("arbitrary",), vmem_limit_bytes=VMEM_LIMIT, has_side_effects=True),
        name="moe_mlp",
    )(tile_expert, n_tiles, next_expert, tile_valid, code, h2, w_gate, w_up, w_down)


def _combine_kernel(tile_cond_ref, x1_ref, w_ref, mod_ref, y_ref, *rest, split_out):
    if split_out:
        outp_ref, outs_ref = rest
    else:
        out_ref, = rest
    i = pl.program_id(0)
    row = tile_cond_ref[i]
    w = w_ref[...]
    w0 = w[:, 0:1]
    w1 = w[:, 1:2]
    parts = []
    for s in range(ROW_TILES):
        y0 = y_ref[pl.ds(s, TM, stride=2 * ROW_TILES), :]
        y1 = y_ref[pl.ds(ROW_TILES + s, TM, stride=2 * ROW_TILES), :]
        parts.append(w0 * y0 + w1 * y1)
    gate = mod_ref[pl.ds(row, 1), 5 * D_MODEL:6 * D_MODEL]
    out = x1_ref[...] + gate * jnp.concatenate(parts, axis=1)
    if split_out:
        @pl.when(i < N_PROMPT_TILES)
        def _():
            outp_ref[...] = out

        @pl.when(i >= N_PROMPT_TILES)
        def _():
            outs_ref[...] = out
    else:
        out_ref[...] = out


def _moe_combine(x1, weights, mod_l, y, split_out):
    tile_cond, _ = _tile_cond(TM)
    if split_out:
        out_shape = [jax.ShapeDtypeStruct((N_PROMPT, D_MODEL), F32), jax.ShapeDtypeStruct((N_SAMPLE, D_MODEL), F32)]
        out_specs = list(_split_specs(TM, False))
    else:
        out_shape = jax.ShapeDtypeStruct((N_TOK, D_MODEL), F32)
        out_specs = pl.BlockSpec((TM, D_MODEL), lambda i, *_: (i, 0))
    return pl.pallas_call(
        functools.partial(_combine_kernel, split_out=split_out),
        out_shape=out_shape,
        grid_spec=pltpu.PrefetchScalarGridSpec(
            num_scalar_prefetch=1,
            grid=(N_TILES,),
            in_specs=[
                pl.BlockSpec((TM, D_MODEL), lambda i, *_: (i, 0)),
                pl.BlockSpec((TM, 2), lambda i, *_: (i, 0)),
                pl.BlockSpec((COND_ROWS, 6 * D_MODEL), lambda i, *_: (0, 0)),
                pl.BlockSpec((2 * TM * ROW_TILES, LANES), lambda i, *_: (i, 0)),
            ],
            out_specs=out_specs,
        ),
        compiler_params=pltpu.CompilerParams(
            dimension_semantics=("arbitrary",), vmem_limit_bytes=VMEM_LIMIT),
        name="moe_combine",
    )(tile_cond, x1, weights, mod_l, y)


def _chunk_plan(segments):
    chunks = []
    for width, normed, dst, dst_col0, f32_dst in segments:
        for k in range(width // LANES):
            chunks.append((normed, dst, dst_col0 + k * LANES, f32_dst, k * LANES))
    return tuple(chunks)


def _head_gain(parts):
    cols = []
    for width, g, mult in parts:
        if g is None:
            cols.append(jnp.ones((width,), F32))
        else:
            cols.append(jnp.tile(g.astype(F32) * mult, width // HEAD_DIM))
    return jnp.concatenate(cols).reshape(1, -1)


def kernel(x_prompt, x_sample, cache_a_k, cache_a_v, cache_b_k, cache_b_v, cache_c_k, cache_c_v, c, c_ctx, w_mod, b_mod, norm_mix, norm_ffn, w_in_ab, w_out_ab, a_q_norm, a_k_norm, b_q_norm, b_k_norm, lam_q1, lam_k1, lam_q2, lam_k2, b_subln, w_in_c, w_out_c, c_q_norm, c_k_norm, c_rpb, w_router, b_router, w_gate, w_up, w_down):
    scale = LOG2E * HEAD_DIM ** -0.5
    cond = jnp.concatenate(
        [c, c_ctx[None, :], jnp.zeros((COND_ROWS - DEC_BATCH - 1, D_MODEL), F32)], axis=0)
    mod = _modulation(cond, w_mod, b_mod)

    xa = x_prompt.reshape(N_PROMPT, D_MODEL)
    xb = x_sample.reshape(N_SAMPLE, D_MODEL)
    merged = False
    new_caches = []
    for l in range(DEPTH):
        mod_l = mod[l]
        if l % 2 == 0:
            e = l // 2
            lam_init = 0.8 - 0.6 * math.exp(-0.3 * l)
            segments = (
                (A_Q_W, True, 0, 0, None),
                (A_KV_W, True, 1, 0, 2),
                (A_KV_W, False, 1, A_KV_W, 3),
                (B_QK_W, True, 0, A_Q_W, None),
                (B_QK_W, True, 1, 2 * A_KV_W, 4),
                (B_V_W, False, 1, 2 * A_KV_W + B_QK_W, 5),
            )
            hg = _head_gain((
                (A_Q_W, a_q_norm[e], scale), (A_KV_W, a_k_norm[e], 1.0), (A_KV_W, None, 1.0),
                (B_QK_W, b_q_norm[e], scale), (B_QK_W, b_k_norm[e], 1.0), (B_V_W, None, 1.0)))
            outs = _lnproj(xa, xb, merged, mod_l, norm_mix[l], w_in_ab[e].astype(BF16), hg, _chunk_plan(segments),
                           A_Q_W + B_QK_W, 2 * A_KV_W + B_QK_W + B_V_W,
                           (A_KV_W, A_KV_W, B_QK_W, B_V_W), True)
            q, kv, ak, av, bk, bv = outs
            new_caches.append((
                ak.reshape(BATCH, SEQ, A_KV_HEADS, HEAD_DIM), av.reshape(BATCH, SEQ, A_KV_HEADS, HEAD_DIM),
                bk.reshape(BATCH, SEQ, B_HEADS, 2, HEAD_DIM), bv.reshape(BATCH, SEQ, B_HEADS, B_V_DIM)))
            lamv = jnp.stack([lam_q1[e], lam_k1[e], lam_q2[e], lam_k2[e]]).astype(F32)
            subln = b_subln[e].reshape(1, B_V_DIM)
            o_p = _attn_ab(q, kv, lamv, subln, lam_init, None)
            caches = (cache_a_k[:, e].reshape(DEC_BATCH, PAST_LEN, A_KV_W),
                      cache_a_v[:, e].reshape(DEC_BATCH, PAST_LEN, A_KV_W),
                      cache_b_k[:, e].reshape(DEC_BATCH, PAST_LEN, B_QK_W),
                      cache_b_v[:, e].reshape(DEC_BATCH, PAST_LEN, B_V_W))
            o_s = _attn_ab(q, kv, lamv, subln, lam_init, caches)
            w_out = w_out_ab[e].astype(BF16)
        else:
            oi = l // 2
            segments = (
                (C_WIDTH, True, 0, 0, None),
                (C_WIDTH, True, 1, 0, 2),
                (C_WIDTH, False, 1, C_WIDTH, 3),
            )
            hg = _head_gain(((C_WIDTH, c_q_norm[oi], scale), (C_WIDTH, c_k_norm[oi], 1.0), (C_WIDTH, None, 1.0)))
            outs = _lnproj(xa, xb, merged, mod_l, norm_mix[l], w_in_c[oi].astype(BF16), hg, _chunk_plan(segments),
                           C_WIDTH, 2 * C_WIDTH, (C_WIDTH, C_WIDTH), False)
            q, kv, ck_new, cv_new = outs
            new_caches.append((ck_new.reshape(BATCH, SEQ, C_HEADS, HEAD_DIM),
                               cv_new.reshape(BATCH, SEQ, C_HEADS, HEAD_DIM)))
            o_p = _attn_c_prompt(q, kv)
            o_s = _attn_na(q, kv,
                           cache_c_k[:, oi].reshape(DEC_BATCH, PAST_LEN, C_WIDTH),
                           cache_c_v[:, oi].reshape(DEC_BATCH, PAST_LEN, C_WIDTH),
                           _na_bias_table(c_rpb[oi]))
            w_out = w_out_c[oi].astype(BF16)

        x1, h2, gates, rank = _outproj(o_p, o_s, xa, xb, merged, w_out, mod_l, norm_ffn[l], w_router, b_router)
        code, weights, tile_expert, n_tiles, next_expert, tile_valid = _route_plan(gates, rank)
        y = _moe_mlp(l, tile_expert, n_tiles, next_expert, tile_valid, code, h2, w_gate, w_up, w_down)
        last = l == DEPTH - 1
        out = _moe_combine(x1, weights, mod_l, y, last)
        if last:
            y_prompt, y_sample = out
        else:
            xa = xb = out
            merged = True

    even = [nc for i, nc in enumerate(new_caches) if i % 2 == 0]
    odd = [nc for i, nc in enumerate(new_caches) if i % 2 == 1]
    stack = lambda items, k: jnp.stack([it[k] for it in items], axis=1)
    return (y_prompt.reshape(BATCH, SEQ, D_MODEL), y_sample.reshape(DEC_BATCH, DEC_SEQ, D_MODEL),
            stack(even, 0), stack(even, 1), stack(even, 2), stack(even, 3),
            stack(odd, 0), stack(odd, 1))
```

```python
import functools
import math

import numpy as np
import jax
import jax.numpy as jnp
from jax import lax
from jax.experimental import pallas as pl
from jax.experimental.pallas import tpu as pltpu

F32 = jnp.float32
BF16 = jnp.bfloat16

D_MODEL = 1024
BATCH = 16
SEQ = 256
DEPTH = 2
DEC_BATCH = 8
DEC_SEQ = 1024
PAST_LEN = 256
GRID_W = 64
HEAD_DIM = 64
ROPE_THETA = 10000.0
RMS_EPS = 1e-6
A_Q_HEADS = 8
A_KV_HEADS = 2
A_GROUP = A_Q_HEADS // A_KV_HEADS
B_HEADS = 4
B_V_DIM = 2 * HEAD_DIM
A_Q_W = A_Q_HEADS * HEAD_DIM
A_KV_W = A_KV_HEADS * HEAD_DIM
B_QK_W = B_HEADS * 2 * HEAD_DIM
B_V_W = B_HEADS * B_V_DIM
C_HEADS = 16
C_WIDTH = C_HEADS * HEAD_DIM
NA_ROWS = 8
NA_COLS = 16
N_EXPERTS = 16
N_GROUPS = 4
EXPERTS_PER_GROUP = N_EXPERTS // N_GROUPS
D_EXPERT = 1024

LANES = 128
SUBLANES = 8
ROW_TILES = D_MODEL // LANES
N_PROMPT = BATCH * SEQ
N_SAMPLE = DEC_BATCH * DEC_SEQ
N_TOK = N_PROMPT + N_SAMPLE
TM = 512
N_TILES = N_TOK // TM
N_PROMPT_TILES = N_PROMPT // TM
TM_LN = 512
LN_PIECE = 512
TM_OUT = 512
COND_ROWS = 16
CTX_COND_ROW = DEC_BATCH
MOE_TM = 256
MOE_ROWS = 2 * N_TOK + N_EXPERTS * MOE_TM
MOE_TILES = MOE_ROWS // MOE_TM
GRID_ROWS = DEC_SEQ // GRID_W
PROMPTS_PER_STEP = 2
NA_ROWS_PER_STEP = 4
NEG_BIG = -1e30
LOG2E = math.log2(math.e)
VMEM_LIMIT = 56 * 1024 * 1024

def _silu(x):
    return x * (1.0 / (1.0 + jnp.exp(-x)))


def _dot(a, b):
    return jnp.dot(a, b, preferred_element_type=F32)


def _dot_nt(a, b):
    return lax.dot_general(a, b, (((1,), (1,)), ((), ())), preferred_element_type=F32)


def _low_half(shape):
    return lax.broadcasted_iota(jnp.int32, shape, len(shape) - 1) < HEAD_DIM


def _swap_halves(x):
    return jnp.concatenate([x[:, HEAD_DIM:], x[:, :HEAD_DIM]], axis=1)


def _tile_cond(tm):
    tiles = np.arange(N_TOK // tm)
    npt = N_PROMPT // tm
    samp = np.maximum(tiles - npt, 0)
    per_seq = DEC_SEQ // tm
    cond = np.where(tiles < npt, CTX_COND_ROW, samp // per_seq)
    rope = np.where(tiles < npt, per_seq, samp % per_seq)
    return jnp.asarray(cond, jnp.int32), jnp.asarray(rope, jnp.int32)


def _split_specs(tm, merged):
    npt = N_PROMPT // tm
    base = npt if merged else 0
    a = pl.BlockSpec((tm, D_MODEL), lambda i, *_: (jnp.minimum(i, npt - 1), 0))
    b = pl.BlockSpec((tm, D_MODEL), lambda i, *_: (jnp.maximum(i - npt, 0) + base, 0))
    return a, b


def _mod_kernel(c_ref, w_ref, b_ref, o_ref):
    s = _silu(c_ref[...])
    s_hi = s.astype(BF16)
    s_lo = (s - s_hi.astype(F32)).astype(BF16)
    w = w_ref[...].astype(BF16)
    o_ref[...] = _dot(s_hi, w) + _dot(s_lo, w) + b_ref[...]


def _modulation(cond, w_mod, b_mod):
    tn = 1536
    return pl.pallas_call(
        _mod_kernel,
        out_shape=jax.ShapeDtypeStruct((DEPTH, COND_ROWS, 6 * D_MODEL), F32),
        grid=(DEPTH, 6 * D_MODEL // tn),
        in_specs=[
            pl.BlockSpec((COND_ROWS, D_MODEL), lambda l, j: (0, 0)),
            pl.BlockSpec((None, D_MODEL, tn), lambda l, j: (l, 0, j)),
            pl.BlockSpec((None, 1, tn), lambda l, j: (l, 0, j)),
        ],
        out_specs=pl.BlockSpec((None, COND_ROWS, tn), lambda l, j: (l, 0, j)),
        compiler_params=pltpu.CompilerParams(
            dimension_semantics=("arbitrary", "arbitrary"), vmem_limit_bytes=VMEM_LIMIT),
        name="modulation",
    )(cond, w_mod, b_mod.reshape(DEPTH, 1, 6 * D_MODEL))


def _rope_tables(tm):
    pos = np.arange(DEC_SEQ)
    rows = (pos // GRID_W).astype(np.float64)
    cols = (pos % GRID_W).astype(np.float64)
    nfreq = HEAD_DIM // 4
    inv = ROPE_THETA ** (-np.arange(nfreq, dtype=np.float64) / nfreq)
    d = np.arange(HEAD_DIM)
    dd = d % (HEAD_DIM // 2)
    p = np.where((d >= HEAD_DIM // 2)[None, :], cols[:, None], rows[:, None])
    ang = p * inv[dd % nfreq][None, :]
    cos, sin = np.cos(ang), np.sin(ang)
    second = (dd >= nfreq)[None, :]
    sa = np.where(second, sin, 0.0)
    sb = np.where(second, 0.0, -sin)

    def full(t, ident):
        t = np.concatenate([t, np.full((tm, HEAD_DIM), ident)], axis=0)
        return jnp.asarray(np.tile(t, (1, LANES // HEAD_DIM)), dtype=F32)

    return full(cos, 1.0), full(sa, 0.0), full(sb, 0.0)


def _lnproj_kernel(tile_cond_ref, tile_rope_ref, xa_ref, xb_ref, mod_ref, g_ref, w_ref, hg_ref, gmat_ref,
                   cos_ref, sa_ref, sb_ref, *out_refs, chunks, use_rope):
    i = pl.program_id(0)
    row = tile_cond_ref[i]
    x = jnp.where(i >= N_SAMPLE // TM_LN, xa_ref[...], xb_ref[...])
    ms = jnp.mean(x * x, axis=-1, keepdims=True)
    xn = x * lax.rsqrt(ms + RMS_EPS) * g_ref[...]
    shift = mod_ref[pl.ds(row, 1), 0:D_MODEL]
    scale = mod_ref[pl.ds(row, 1), D_MODEL:2 * D_MODEL]
    h = (xn * (1.0 + scale) + shift).astype(BF16)
    piece_chunks = LN_PIECE // LANES
    for c0 in range(0, len(chunks), piece_chunks):
        n_sub = min(piece_chunks, len(chunks) - c0)
        yp = _dot(h, w_ref[:, c0 * LANES:(c0 + n_sub) * LANES])
        yn = [None] * n_sub
        for s0 in range(0, n_sub, 2):
            if chunks[c0 + s0][0] or chunks[c0 + s0 + 1][0]:
                y2 = yp[:, s0 * LANES:(s0 + 2) * LANES]
                gs = _dot((y2 * y2).astype(BF16), gmat_ref[...])
                yn2 = y2 * lax.rsqrt(gs * (1.0 / HEAD_DIM) + RMS_EPS)
                yn[s0], yn[s0 + 1] = yn2[:, :LANES], yn2[:, LANES:]
        for sub in range(n_sub):
            c = c0 + sub
            normed, dst, dst_col, f32_dst, f32_col = chunks[c]
            if normed:
                y = yn[sub] * hg_ref[:, c * LANES:(c + 1) * LANES]
            else:
                y = yp[:, sub * LANES:(sub + 1) * LANES]
            if f32_dst is not None:
                out_refs[f32_dst][:, f32_col:f32_col + LANES] = y
            if normed and use_rope:
                y = (y * cos_ref[...] + pltpu.roll(y, HEAD_DIM // 4, 1) * sa_ref[...]
                     + pltpu.roll(y, LANES - HEAD_DIM // 4, 1) * sb_ref[...])
            out_refs[dst][:, dst_col:dst_col + LANES] = y.astype(BF16)


def _lnproj(xa, xb, merged, mod_l, gain, w_bf16, head_gain, chunks, q_w, kv_w, f32_widths, use_rope):
    dout = w_bf16.shape[1]
    tm = TM_LN
    npt = N_PROMPT // tm
    nst = N_SAMPLE // tm
    cond_tok, rope_tok = _tile_cond(tm)
    order = np.concatenate([np.arange(npt, npt + nst), np.arange(npt)])
    tile_cond, tile_rope = cond_tok[order], rope_tok[order]
    cos, sa, sb = _rope_tables(tm)
    gmat = jnp.asarray(np.kron(np.eye(2 * LANES // HEAD_DIM), np.ones((HEAD_DIM, HEAD_DIM))), dtype=BF16)
    const = lambda i, *_: (0, 0)
    tok = lambda i, *_: (jnp.where(i < nst, i + npt, i - nst), 0)
    rope_map = lambda i, tc, tr: (tr[i], 0)
    prm = lambda i, *_: (jnp.maximum(i - nst, 0), 0)
    base = npt if merged else 0
    xa_spec = pl.BlockSpec((tm, D_MODEL), prm)
    xb_spec = pl.BlockSpec((tm, D_MODEL), lambda i, *_: (jnp.minimum(i, nst - 1) + base, 0))
    out_shape = [jax.ShapeDtypeStruct((N_TOK, q_w), BF16), jax.ShapeDtypeStruct((N_TOK, kv_w), BF16)]
    out_specs = [pl.BlockSpec((tm, q_w), tok), pl.BlockSpec((tm, kv_w), tok)]
    for wd in f32_widths:
        out_shape.append(jax.ShapeDtypeStruct((N_PROMPT, wd), F32))
        out_specs.append(pl.BlockSpec((tm, wd), prm))
    return pl.pallas_call(
        functools.partial(_lnproj_kernel, chunks=chunks, use_rope=use_rope),
        out_shape=out_shape,
        grid_spec=pltpu.PrefetchScalarGridSpec(
            num_scalar_prefetch=2,
            grid=(N_TOK // tm,),
            in_specs=[
                xa_spec, xb_spec,
                pl.BlockSpec((COND_ROWS, 6 * D_MODEL), const),
                pl.BlockSpec((1, D_MODEL), const),
                pl.BlockSpec((D_MODEL, dout), const),
                pl.BlockSpec((1, dout), const),
                pl.BlockSpec((2 * LANES, 2 * LANES), const),
                pl.BlockSpec((tm, LANES), rope_map),
                pl.BlockSpec((tm, LANES), rope_map),
                pl.BlockSpec((tm, LANES), rope_map),
            ],
            out_specs=out_specs,
        ),
        compiler_params=pltpu.CompilerParams(
            dimension_semantics=("arbitrary",), vmem_limit_bytes=VMEM_LIMIT),
        name="lnproj",
    )(tile_cond, tile_rope, xa, xb, mod_l, gain.reshape(1, D_MODEL), w_bf16, head_gain, gmat, cos, sa, sb)


def _softmax_parts(s, want_sum=True):
    p = jnp.exp2(s - jnp.max(s, axis=-1, keepdims=True))
    return p, (jnp.sum(p, axis=-1, keepdims=True) if want_sum else None)


def _gqa_pairs(q_ref, q_col0, kv_ref, k_col0, v_col0, n_kv, group, tq):
    low = _low_half((tq, LANES))
    mxu_sums = group > 1
    head_out = [None] * (n_kv * group)
    jobs = [(pair, half) for pair in range(n_kv // 2) for half in range(2)]
    scores = []
    for pair, half in jobs:
        k_pair = kv_ref[:, k_col0 + pair * LANES:k_col0 + (pair + 1) * LANES]
        kvh = 2 * pair + half
        keep = low if half == 0 else jnp.logical_not(low)
        qs = []
        for g in range(group):
            head = kvh * group + g
            blk = q_ref[:, q_col0 + (head // 2) * LANES:q_col0 + (head // 2 + 1) * LANES]
            if head % 2 != half:
                blk = _swap_halves(blk)
            qs.append(jnp.where(keep, blk, jnp.zeros_like(blk)))
        q = qs[0] if group == 1 else jnp.concatenate(qs, axis=0)
        scores.append(_dot_nt(q, k_pair))

    def finish():
        for (pair, half), s in zip(jobs, scores):
            v = kv_ref[:, v_col0 + pair * LANES:v_col0 + (pair + 1) * LANES]
            kvh = 2 * pair + half
            if mxu_sums:
                own = _low_half(v.shape) if half == 0 else jnp.logical_not(_low_half(v.shape))
                v = jnp.where(own, v, jnp.ones_like(v))
            p, l = _softmax_parts(s, want_sum=not mxu_sums)
            o = _dot(p.astype(BF16), v)
            o = o * (1.0 / (pltpu.roll(o, HEAD_DIM, 1) if mxu_sums else l))
            for g in range(group):
                head = kvh * group + g
                og = o[g * tq:(g + 1) * tq]
                if head % 2 != half:
                    og = pltpu.roll(og, HEAD_DIM, 1)
                head_out[head] = og
        return [jnp.where(low, head_out[2 * k], head_out[2 * k + 1]) for k in range(n_kv * group // 2)]

    return finish


def _attn_ab_kernel(*refs, has_cache, tq, n_seq, lam_init):
    if has_cache:
        q_ref, kvn_ref, cak_ref, cav_ref, cbk_ref, cbv_ref, lamv_ref, subln_ref, o_ref, kv_ref = refs

        @pl.when(pl.program_id(1) == 0)
        def _():
            col = 0
            for c_ref in (cak_ref, cav_ref, cbk_ref, cbv_ref):
                w = c_ref.shape[1]
                kv_ref[0:PAST_LEN, col:col + w] = c_ref[...].astype(BF16)
                col += w
            kv_ref[PAST_LEN:, :] = kvn_ref[...]
    else:
        q_ref, kv_ref, lamv_ref, subln_ref, o_ref = refs

    lv = lamv_ref[...]
    l1 = jnp.sum(lv[0:1] * lv[1:2], axis=-1, keepdims=True)
    l2 = jnp.sum(lv[2:3] * lv[3:4], axis=-1, keepdims=True)
    lam = jnp.exp(l1) - jnp.exp(l2) + lam_init
    bk0 = 2 * A_KV_W
    bv0 = bk0 + B_QK_W
    low = _low_half((tq, LANES))
    t_kv = kv_ref.shape[0] // n_seq
    for b in range(n_seq):
        q_b = q_ref.at[b * tq:(b + 1) * tq]
        kv_b = kv_ref.at[b * t_kv:(b + 1) * t_kv]
        finish_a = _gqa_pairs(q_b, 0, kv_b, 0, A_KV_W, A_KV_HEADS, A_GROUP, tq)
        b_scores = []
        for h in range(B_HEADS):
            lanes = slice(h * LANES, (h + 1) * LANES)
            qp = q_b[:, A_Q_W + lanes.start:A_Q_W + lanes.stop]
            zero = jnp.zeros_like(qp)
            q = jnp.concatenate([jnp.where(low, qp, zero), jnp.where(low, zero, qp)], axis=0)
            b_scores.append(_dot_nt(q, kv_b[:, bk0 + lanes.start:bk0 + lanes.stop]))
        outs = finish_a()
        for h in range(B_HEADS):
            lanes = slice(h * LANES, (h + 1) * LANES)
            p, l = _softmax_parts(b_scores[h])
            r = 1.0 / l
            a = p[:tq] * r[:tq] - p[tq:] * (lam * r[tq:])
            o = _dot(a.astype(BF16), kv_b[:, bv0 + lanes.start:bv0 + lanes.stop])
            ms = jnp.mean(o * o, axis=-1, keepdims=True)
            o = o * lax.rsqrt(ms + RMS_EPS) * subln_ref[...] * (1.0 - lam_init)
            outs.append(o)
        o_ref[b * tq:(b + 1) * tq, :] = jnp.concatenate(outs, axis=1).astype(BF16)


def _attn_ab(q, kv, lamv, subln, lam_init, caches):
    kv_w = kv.shape[1]
    kern = functools.partial(_attn_ab_kernel, lam_init=lam_init)
    const = lambda b, j: (0, 0)
    cp = pltpu.CompilerParams(dimension_semantics=("arbitrary", "arbitrary"), vmem_limit_bytes=VMEM_LIMIT)
    if caches is None:
        n_seq = 1
        rows = n_seq * SEQ
        return pl.pallas_call(
            functools.partial(kern, has_cache=False, tq=SEQ, n_seq=n_seq),
            out_shape=jax.ShapeDtypeStruct((N_PROMPT, D_MODEL), BF16),
            grid=(BATCH // n_seq, 1),
            in_specs=[
                pl.BlockSpec((rows, D_MODEL), lambda b, j: (b, 0)),
                pl.BlockSpec((rows, kv_w), lambda b, j: (b, 0)),
                pl.BlockSpec((4, HEAD_DIM), const),
                pl.BlockSpec((1, B_V_DIM), const),
            ],
            out_specs=pl.BlockSpec((rows, D_MODEL), lambda b, j: (b, 0)),
            compiler_params=cp,
            name="attn_ab_prompt",
        )(q, kv, lamv, subln)
    tq = 256
    nq = DEC_SEQ // tq
    q0 = N_PROMPT // tq
    kv0 = N_PROMPT // DEC_SEQ
    cak, cav, cbk, cbv = caches
    cspec = lambda w: pl.BlockSpec((None, PAST_LEN, w), lambda b, j: (b, 0, 0))
    return pl.pallas_call(
        functools.partial(kern, has_cache=True, tq=tq, n_seq=1),
        out_shape=jax.ShapeDtypeStruct((N_SAMPLE, D_MODEL), BF16),
        grid=(DEC_BATCH, nq),
        in_specs=[
            pl.BlockSpec((tq, D_MODEL), lambda b, j: (q0 + b * nq + j, 0)),
            pl.BlockSpec((DEC_SEQ, kv_w), lambda b, j: (kv0 + b, 0)),
            cspec(A_KV_W), cspec(A_KV_W), cspec(B_QK_W), cspec(B_V_W),
            pl.BlockSpec((4, HEAD_DIM), const),
            pl.BlockSpec((1, B_V_DIM), const),
        ],
        out_specs=pl.BlockSpec((tq, D_MODEL), lambda b, j: (b * nq + j, 0)),
        scratch_shapes=[pltpu.VMEM((PAST_LEN + DEC_SEQ, kv_w), BF16)],
        compiler_params=cp,
        name="attn_ab_sample",
    )(q, kv, cak, cav, cbk, cbv, lamv, subln)


def _attn_c_prompt_kernel(q_ref, kv_ref, o_ref):
    for b in range(PROMPTS_PER_STEP):
        rows = slice(b * SEQ, (b + 1) * SEQ)
        outs = _gqa_pairs(q_ref.at[rows], 0, kv_ref.at[rows], 0, C_WIDTH, C_HEADS, 1, SEQ)()
        o_ref[rows, :] = jnp.concatenate(outs, axis=1).astype(BF16)


def _attn_c_prompt(q, kv):
    rows = PROMPTS_PER_STEP * SEQ
    return pl.pallas_call(
        _attn_c_prompt_kernel,
        out_shape=jax.ShapeDtypeStruct((N_PROMPT, D_MODEL), BF16),
        grid=(BATCH // PROMPTS_PER_STEP,),
        in_specs=[
            pl.BlockSpec((rows, C_WIDTH), lambda b: (b, 0)),
            pl.BlockSpec((rows, 2 * C_WIDTH), lambda b: (b, 0)),
        ],
        out_specs=pl.BlockSpec((rows, C_WIDTH), lambda b: (b, 0)),
        compiler_params=pltpu.CompilerParams(dimension_semantics=("arbitrary",), vmem_limit_bytes=VMEM_LIMIT),
        name="attn_c_prompt",
    )(q, kv)


def _na_kernel(q_ref, kv_ref, ck_ref, cv_ref, tp_ref, o_ref, ckb, cvb):
    step = pl.program_id(1)

    @pl.when(step == 0)
    def _():
        ckb[...] = ck_ref[...].astype(BF16)
        cvb[...] = cv_ref[...].astype(BF16)

    kh = min(NA_ROWS, GRID_ROWS)
    win = kh * GRID_W
    low = _low_half((GRID_W, LANES))
    n_pair = C_HEADS // 2
    for rr in range(NA_ROWS_PER_STEP):
        r = step * NA_ROWS_PER_STEP + rr
        rows = slice(rr * GRID_W, (rr + 1) * GRID_W)
        rs = jnp.clip(r - kh // 2, 0, GRID_ROWS - kh)
        ro0 = rs - r + (NA_ROWS - 1)
        start = pl.multiple_of(rs * GRID_W, GRID_W)
        s_lat, s_ctx = [], []
        for j in range(n_pair):
            lanes = slice(j * LANES, (j + 1) * LANES)
            qp = q_ref[rows, lanes]
            zero = jnp.zeros_like(qp)
            q = jnp.concatenate([jnp.where(low, qp, zero), jnp.where(low, zero, qp)], axis=0)
            kw = kv_ref[pl.ds(start, win), lanes]
            bias = jnp.concatenate(
                [jnp.concatenate([tp_ref[2 * j + hh, ro0 + 2 * t] for t in range(kh // 2)], axis=1)
                 for hh in range(2)], axis=0)
            s_lat.append(_dot_nt(q, kw) + bias)
            s_ctx.append(_dot_nt(q, ckb[:, lanes]))
        outs = []
        for j in range(n_pair):
            lanes = slice(j * LANES, (j + 1) * LANES)
            s_l, s_c = s_lat[j], s_ctx[j]
            m = jnp.maximum(jnp.max(s_l, axis=-1, keepdims=True), jnp.max(s_c, axis=-1, keepdims=True))
            p_l = jnp.exp2(s_l - m)
            p_c = jnp.exp2(s_c - m)
            l = jnp.sum(p_l, axis=-1, keepdims=True) + jnp.sum(p_c, axis=-1, keepdims=True)
            vw = kv_ref[pl.ds(start, win), C_WIDTH + lanes.start:C_WIDTH + lanes.stop]
            o = _dot(p_c.astype(BF16), cvb[:, lanes]) + _dot(p_l.astype(BF16), vw)
            o = o * (1.0 / l)
            outs.append(jnp.where(low, o[:GRID_W], o[GRID_W:]))
        o_ref[rows, :] = jnp.concatenate(outs, axis=1).astype(BF16)


NA_BIAS_PAD = GRID_W - NA_COLS


def _na_bias_kernel(w_ref, mask_ref, tp_ref):
    low = _low_half((GRID_W, LANES))
    keep = mask_ref[...] > 0.0
    n_off = 2 * NA_ROWS - 1
    left, right = [], []
    for ro in range(n_off):
        row = jnp.broadcast_to(w_ref[ro:ro + 1, :] * LOG2E, (GRID_W, LANES))
        left.append(pltpu.roll(row, LANES - GRID_W + 1, 1, stride=1, stride_axis=0))
        right.append(pltpu.roll(row, 1, 1, stride=1, stride_axis=0))
    for t in range(n_off - 1):
        tp_ref[t] = jnp.where(keep, jnp.where(low, left[t], right[t + 1]), NEG_BIG)


def _na_bias_table(rpb):
    cols = np.arange(GRID_W)
    col_start = np.clip(cols - NA_COLS // 2, 0, GRID_W - NA_COLS)
    col_in = (cols[None, :] >= col_start[:, None]) & (cols[None, :] < col_start[:, None] + NA_COLS)
    assert np.abs((cols[None, :] - cols[:, None])[col_in]).max() <= NA_COLS - 1
    mask = jnp.asarray(np.tile(col_in, (1, LANES // GRID_W)), dtype=F32)
    n_off = 2 * NA_ROWS - 1
    n_rel = 2 * NA_COLS - 1
    w = jnp.pad(rpb.astype(F32), ((0, 0), (0, 0), (NA_BIAS_PAD, LANES - NA_BIAS_PAD - n_rel)))
    return pl.pallas_call(
        _na_bias_kernel,
        out_shape=jax.ShapeDtypeStruct((C_HEADS, n_off - 1, GRID_W, LANES), F32),
        grid=(C_HEADS,),
        in_specs=[
            pl.BlockSpec((None, n_off, LANES), lambda h: (h, 0, 0)),
            pl.BlockSpec((GRID_W, LANES), lambda h: (0, 0)),
        ],
        out_specs=pl.BlockSpec((None, n_off - 1, GRID_W, LANES), lambda h: (h, 0, 0, 0)),
        compiler_params=pltpu.CompilerParams(dimension_semantics=("arbitrary",)),
        name="na_bias",
    )(w, mask)


def _attn_na(q, kv, ck, cv, tp):
    tq = NA_ROWS_PER_STEP * GRID_W
    steps = GRID_ROWS // NA_ROWS_PER_STEP
    q0 = N_PROMPT // tq
    kv0 = N_PROMPT // DEC_SEQ
    return pl.pallas_call(
        _na_kernel,
        out_shape=jax.ShapeDtypeStruct((N_SAMPLE, D_MODEL), BF16),
        grid=(DEC_BATCH, steps),
        in_specs=[
            pl.BlockSpec((tq, C_WIDTH), lambda b, r: (q0 + b * steps + r, 0)),
            pl.BlockSpec((DEC_SEQ, 2 * C_WIDTH), lambda b, r: (kv0 + b, 0)),
            pl.BlockSpec((None, PAST_LEN, C_WIDTH), lambda b, r: (b, 0, 0)),
            pl.BlockSpec((None, PAST_LEN, C_WIDTH), lambda b, r: (b, 0, 0)),
            pl.BlockSpec(tp.shape, lambda b, r: (0, 0, 0, 0)),
        ],
        out_specs=pl.BlockSpec((tq, C_WIDTH), lambda b, r: (b * steps + r, 0)),
        scratch_shapes=[pltpu.VMEM((PAST_LEN, C_WIDTH), BF16), pltpu.VMEM((PAST_LEN, C_WIDTH), BF16)],
        compiler_params=pltpu.CompilerParams(
            dimension_semantics=("arbitrary", "arbitrary"), vmem_limit_bytes=VMEM_LIMIT),
        name="attn_na_sample",
    )(q, kv, ck, cv, tp)


def _first_wins_ranks(vals):
    ranks = []
    for i in range(len(vals)):
        r = jnp.zeros_like(vals[i])
        for j in range(len(vals)):
            if j == i:
                continue
            beats = (vals[j] >= vals[i]) if j < i else (vals[j] > vals[i])
            r = r + jnp.where(beats, 1.0, 0.0)
        ranks.append(r)
    return ranks


def _outproj_kernel(tile_cond_ref, op_ref, os_ref, xa_ref, xb_ref, w_ref, mod_ref, g2_ref, wrt_ref, br_ref,
                    tri_ref, x1_ref, h2_ref, gates_ref, rank_ref, carry_ref):
    i = pl.program_id(0)
    row = tile_cond_ref[i]
    is_prompt = i < N_PROMPT // TM_OUT

    @pl.when(i == 0)
    def _():
        carry_ref[...] = jnp.zeros_like(carry_ref)

    o = jnp.where(is_prompt, op_ref[...], os_ref[...])
    x = jnp.where(is_prompt, xa_ref[...], xb_ref[...])
    acc = _dot(o, w_ref[...])
    gate = mod_ref[pl.ds(row, 1), 2 * D_MODEL:3 * D_MODEL]
    x1 = x + gate * acc
    x1_ref[...] = x1
    ms = jnp.mean(x1 * x1, axis=-1, keepdims=True)
    xn = x1 * lax.rsqrt(ms + RMS_EPS) * g2_ref[...]
    shift = mod_ref[pl.ds(row, 1), 3 * D_MODEL:4 * D_MODEL]
    scale = mod_ref[pl.ds(row, 1), 4 * D_MODEL:5 * D_MODEL]
    h2 = xn * (1.0 + scale) + shift
    for s in range(ROW_TILES):
        h2_ref[pl.ds(s, TM_OUT, stride=ROW_TILES), :] = h2[:, s * LANES:(s + 1) * LANES]

    h_hi = h2.astype(BF16)
    h_lo = (h2 - h_hi.astype(F32)).astype(BF16)
    part = _dot_nt(wrt_ref[...], h_hi)
    logits = part[:N_EXPERTS] + part[N_EXPERTS:] + _dot_nt(wrt_ref[:N_EXPERTS, :], h_lo)
    e = jnp.exp(logits - jnp.max(logits, axis=0, keepdims=True))
    scores = e * (1.0 / jnp.sum(e, axis=0, keepdims=True))
    sel = scores + br_ref[...]
    sel_rows = [sel[k:k + 1, :] for k in range(N_EXPERTS)]
    in_top2 = []
    group_sum = []
    for g in range(N_GROUPS):
        vals = sel_rows[g * EXPERTS_PER_GROUP:(g + 1) * EXPERTS_PER_GROUP]
        ranks = _first_wins_ranks(vals)
        top = [rk < 2.0 for rk in ranks]
        in_top2.extend(top)
        s = jnp.zeros_like(vals[0])
        for v, t in zip(vals, top):
            s = s + jnp.where(t, v, 0.0)
        group_sum.append(s)
    group_rank = _first_wins_ranks(group_sum)
    mask_rows = []
    for k in range(N_EXPERTS):
        chosen = jnp.where(in_top2[k], 1.0, 0.0) * jnp.where(group_rank[k // EXPERTS_PER_GROUP] < 1.0, 1.0, 0.0)
        mask_rows.append(chosen)
    mask = jnp.concatenate(mask_rows, axis=0)
    picked = scores * mask
    gates_ref[...] = picked * (1.0 / jnp.sum(picked, axis=0, keepdims=True))
    prefix = _dot(mask.astype(BF16), tri_ref[...])
    rank_ref[...] = jnp.where(mask > 0.0, prefix + carry_ref[...], -1.0)
    carry_ref[...] = carry_ref[...] + jnp.sum(mask, axis=1, keepdims=True)


def _outproj(o_prompt, o_sample, xa, xb, merged, w_bf16, mod_l, gain2, w_router, b_router):
    const = lambda i, *_: (0, 0)
    tok = lambda i, *_: (i, 0)
    tokT = lambda i, *_: (0, i)
    tm = TM_OUT
    tile_cond, _ = _tile_cond(tm)
    op_spec, os_spec = _split_specs(tm, False)
    xa_spec, xb_spec = _split_specs(tm, merged)
    tri = jnp.asarray(np.triu(np.ones((tm, tm)), k=1), dtype=BF16)
    wrt = w_router.T.astype(F32)
    wrt_hi = wrt.astype(BF16)
    wrt_split = jnp.concatenate([wrt_hi, (wrt - wrt_hi.astype(F32)).astype(BF16)], axis=0)
    return pl.pallas_call(
        _outproj_kernel,
        out_shape=[
            jax.ShapeDtypeStruct((N_TOK, D_MODEL), F32),
            jax.ShapeDtypeStruct((N_TOK * ROW_TILES, LANES), F32),
            jax.ShapeDtypeStruct((N_EXPERTS, N_TOK), F32),
            jax.ShapeDtypeStruct((N_EXPERTS, N_TOK), F32),
        ],
        grid_spec=pltpu.PrefetchScalarGridSpec(
            num_scalar_prefetch=1,
            grid=(N_TOK // tm,),
            in_specs=[
                op_spec, os_spec, xa_spec, xb_spec,
                pl.BlockSpec((D_MODEL, D_MODEL), const),
                pl.BlockSpec((COND_ROWS, 6 * D_MODEL), const),
                pl.BlockSpec((1, D_MODEL), const),
                pl.BlockSpec((2 * N_EXPERTS, D_MODEL), const),
                pl.BlockSpec((N_EXPERTS, 1), const),
                pl.BlockSpec((tm, tm), const),
            ],
            out_specs=[
                pl.BlockSpec((tm, D_MODEL), tok),
                pl.BlockSpec((tm * ROW_TILES, LANES), tok),
                pl.BlockSpec((N_EXPERTS, tm), tokT),
                pl.BlockSpec((N_EXPERTS, tm), tokT),
            ],
            scratch_shapes=[pltpu.VMEM((N_EXPERTS, 1), F32)],
        ),
        compiler_params=pltpu.CompilerParams(
            dimension_semantics=("arbitrary",), vmem_limit_bytes=VMEM_LIMIT),
        name="outproj_router",
    )(tile_cond, o_prompt, o_sample, xa, xb, w_bf16, mod_l, gain2.reshape(1, D_MODEL), wrt_split,
      b_router.reshape(N_EXPERTS, 1), tri)


def _plan_kernel(gates_ref, rank_ref, dests_ref, wts_ref, last_ref, misc_ref):
    rank = rank_ref[...]
    gates = gates_ref[...]
    sel = rank >= 0.0
    hit = jnp.where(sel, 1.0, 0.0)
    tile = float(MOE_TM)

    def pad_up(c):
        return jnp.floor((c + (tile - 1.0)) * (1.0 / tile)) * tile

    counts_col = jnp.sum(hit, axis=1, keepdims=True)
    ones = jnp.ones((SUBLANES, hit.shape[1]), BF16)
    counts_row = _dot_nt(ones, hit.astype(BF16))[0:1]
    padded_col = pad_up(counts_col)
    padded_row = pad_up(counts_row)
    e_sub = lax.broadcasted_iota(jnp.int32, (N_EXPERTS, N_EXPERTS), 0)
    e_lane = lax.broadcasted_iota(jnp.int32, (N_EXPERTS, N_EXPERTS), 1)
    ends_col = jnp.sum(jnp.where(e_lane <= e_sub, padded_row, 0.0), axis=1, keepdims=True)
    ends_row = jnp.sum(jnp.where(e_sub <= e_lane, padded_col, 0.0), axis=0, keepdims=True)
    dest = jnp.where(sel, ends_col - padded_col + rank, -1.0)
    d1 = jnp.max(dest, axis=0, keepdims=True)
    d0 = jnp.min(jnp.where(sel, dest, float(MOE_ROWS)), axis=0, keepdims=True)
    w0 = jnp.sum(jnp.where(dest == d0, gates, 0.0), axis=0, keepdims=True)
    w1 = jnp.sum(jnp.where(dest == d1, gates, 0.0), axis=0, keepdims=True)
    dests_ref[...] = (jnp.concatenate([d0, d1], axis=0) * float(ROW_TILES)).astype(jnp.int32)
    wts_ref[...] = jnp.concatenate([w0, w1], axis=0)

    last = jnp.where(padded_col > 0.0, (ends_col - tile) * float(ROW_TILES), -1.0)
    last_ref[...] = jnp.broadcast_to(last, last_ref.shape).astype(jnp.int32)

    lane = lax.broadcasted_iota(jnp.int32, (1, LANES), 1).astype(F32)
    n_tiles = ends_row[:, N_EXPERTS - 1:N_EXPERTS] * (1.0 / tile)
    tile_expert = jnp.sum(jnp.where(lane * tile >= ends_col, 1.0, 0.0), axis=0, keepdims=True)
    tile_expert = jnp.minimum(tile_expert, float(N_EXPERTS - 1))
    e_col = lax.broadcasted_iota(jnp.int32, (N_EXPERTS, 1), 0).astype(F32)
    later = jnp.where((e_col > tile_expert) & (padded_col > 0.0), e_col, float(N_EXPERTS))
    nxt = jnp.min(later, axis=0, keepdims=True)
    nxt = jnp.where(nxt > float(N_EXPERTS - 1), tile_expert, nxt)
    spare = n_tiles + lane
    spare = jnp.where(spare < float(MOE_TILES), spare * (tile * ROW_TILES), -1.0)
    zero = jnp.zeros((1, LANES), F32)
    rows = [tile_expert, nxt, spare, n_tiles + zero] + [zero] * (SUBLANES - 4)
    misc_ref[...] = jnp.concatenate(rows, axis=0).astype(jnp.int32)


def _route_plan(gates, rank):
    assert MOE_TILES <= LANES and N_EXPERTS <= LANES
    full = lambda shape: pl.BlockSpec(shape, lambda: (0,) * len(shape))
    dests, wts, last, misc = pl.pallas_call(
        _plan_kernel,
        out_shape=[
            jax.ShapeDtypeStruct((2, N_TOK), jnp.int32),
            jax.ShapeDtypeStruct((2, N_TOK), F32),
            jax.ShapeDtypeStruct((N_EXPERTS, LANES), jnp.int32),
            jax.ShapeDtypeStruct((SUBLANES, LANES), jnp.int32),
        ],
        in_specs=[full((N_EXPERTS, N_TOK)), full((N_EXPERTS, N_TOK))],
        out_specs=[full((2, N_TOK)), full((2, N_TOK)), full((N_EXPERTS, LANES)), full((SUBLANES, LANES))],
        compiler_params=pltpu.CompilerParams(vmem_limit_bytes=VMEM_LIMIT),
        name="route_plan",
    )(gates, rank)
    tile_expert = misc[0, :MOE_TILES]
    next_expert = misc[1, :MOE_TILES]
    n_tiles = misc[3, :1]
    clear = jnp.concatenate([last[:, 0], misc[2, :N_EXPERTS]])
    return dests.reshape(2 * N_TOK), wts.T, tile_expert, n_tiles, next_expert, clear


def _scatter_kernel(d_ref, last_ref, h_ref, xs_ref, zeros, sem, zsem):
    i = pl.program_id(0)
    base = i * TM
    rows = MOE_TM * ROW_TILES

    @pl.when(i == 0)
    def _():
        zeros[...] = jnp.zeros_like(zeros)
        for e in range(2 * N_EXPERTS):
            @pl.when(last_ref[e] >= 0)
            def _():
                at = pl.multiple_of(last_ref[e], SUBLANES)
                pltpu.make_async_copy(zeros, xs_ref.at[pl.ds(at, rows)], zsem).start()
        for e in range(2 * N_EXPERTS):
            @pl.when(last_ref[e] >= 0)
            def _():
                pltpu.make_async_copy(zeros, xs_ref.at[pl.ds(0, rows)], zsem).wait()

    def copy(r, t):
        src = h_ref.at[pl.ds(pl.multiple_of(r * ROW_TILES, SUBLANES), ROW_TILES)]
        dst = xs_ref.at[pl.ds(pl.multiple_of(t, SUBLANES), ROW_TILES)]
        return pltpu.make_async_copy(src, dst, sem)

    def issue(r, c):
        copy(r, d_ref[base + r]).start(priority=0)
        copy(r, d_ref[N_TOK + base + r]).start(priority=1)
        return c

    lax.fori_loop(0, TM, issue, 0, unroll=8)

    def drain(r, c):
        copy(r, 0).wait()
        copy(r, 0).wait()
        return c

    lax.fori_loop(0, TM, drain, 0, unroll=8)


def _moe_scatter(dests, last_tile, h2):
    return pl.pallas_call(
        _scatter_kernel,
        out_shape=jax.ShapeDtypeStruct((MOE_ROWS * ROW_TILES, LANES), F32),
        grid_spec=pltpu.PrefetchScalarGridSpec(
            num_scalar_prefetch=2,
            grid=(N_TILES,),
            in_specs=[pl.BlockSpec((TM * ROW_TILES, LANES), lambda i, *_: (i, 0))],
            out_specs=pl.BlockSpec(memory_space=pl.ANY),
            scratch_shapes=[
                pltpu.VMEM((MOE_TM * ROW_TILES, LANES), F32),
                pltpu.SemaphoreType.DMA,
                pltpu.SemaphoreType.DMA,
            ],
        ),
        compiler_params=pltpu.CompilerParams(
            dimension_semantics=("arbitrary",), vmem_limit_bytes=VMEM_LIMIT, has_side_effects=True),
        name="moe_scatter",
    )(dests, last_tile, h2)


def _mlp_kernel(te_ref, nv_ref, nxt_ref, xs_ref, wg_hbm, wu_hbm, wd_hbm, y_ref, stage, wgb, wub, wdb, hb, sem,
                *, layer):
    t = pl.program_id(0)

    def fetch(e):
        return [pltpu.make_async_copy(w.at[layer, e], stage.at[k], sem.at[k])
                for k, w in enumerate((wg_hbm, wu_hbm, wd_hbm))]

    @pl.when(t < nv_ref[0])
    def _():
        e = te_ref[t]
        prev = te_ref[jnp.maximum(t - 1, 0)]

        @pl.when(t == 0)
        def _():
            for cp in fetch(e):
                cp.start()

        @pl.when((t == 0) | (e != prev))
        def _():
            for cp in fetch(e):
                cp.wait()
            wgb[...] = stage[0].astype(BF16)
            wub[...] = stage[1].astype(BF16)
            wdb[...] = stage[2].astype(BF16)

            @pl.when(nxt_ref[t] != e)
            def _():
                for cp in fetch(nxt_ref[t]):
                    cp.start()

        x = jnp.concatenate(
            [xs_ref[pl.ds(s, MOE_TM, stride=ROW_TILES), :] for s in range(ROW_TILES)], axis=1).astype(BF16)
        step = 512
        for c in range(0, D_EXPERT, step):
            g = _dot(x, wgb[:, c:c + step])
            u = _dot(x, wub[:, c:c + step])
            hb[:, c:c + step] = (_silu(g) * u).astype(BF16)
        y = _dot(hb[...], wdb[...])
        for s in range(ROW_TILES):
            y_ref[pl.ds(s, MOE_TM, stride=ROW_TILES), :] = y[:, s * LANES:(s + 1) * LANES]

    @pl.when(t >= nv_ref[0])
    def _():
        y_ref[...] = jnp.zeros_like(y_ref)


def _moe_mlp(layer, tile_expert, n_tiles, next_expert, xs, w_gate, w_up, w_down):
    assert D_MODEL == D_EXPERT

    def tile_map(t, te, nv, nx):
        return (jnp.minimum(t, nv[0] - 1), 0)

    def out_map(t, te, nv, nx):
        return (t, 0)

    hbm = pl.BlockSpec(memory_space=pl.ANY)
    return pl.pallas_call(
        functools.partial(_mlp_kernel, layer=layer),
        out_shape=jax.ShapeDtypeStruct((MOE_ROWS * ROW_TILES, LANES), F32),
        grid_spec=pltpu.PrefetchScalarGridSpec(
            num_scalar_prefetch=3,
            grid=(MOE_TILES,),
            in_specs=[pl.BlockSpec((MOE_TM * ROW_TILES, LANES), tile_map), hbm, hbm, hbm],
            out_specs=pl.BlockSpec((MOE_TM * ROW_TILES, LANES), out_map),
            scratch_shapes=[
                pltpu.VMEM((3, D_MODEL, D_EXPERT), F32),
                pltpu.VMEM((D_MODEL, D_EXPERT), BF16),
                pltpu.VMEM((D_MODEL, D_EXPERT), BF16),
                pltpu.VMEM((D_EXPERT, D_MODEL), BF16),
                pltpu.VMEM((MOE_TM, D_EXPERT), BF16),
                pltpu.SemaphoreType.DMA((3,)),
            ],
        ),
        compiler_params=pltpu.CompilerParams(
            dimension_semantics=("arbitrary",), vmem_limit_bytes=VMEM_LIMIT),
        name="moe_mlp",
    )(tile_expert, n_tiles, next_expert, xs, w_gate, w_up, w_down)


def _combine_kernel(d_ref, tile_cond_ref, x1_ref, w_ref, mod_ref, y_ref, *rest, split_out):
    if split_out:
        outp_ref, outs_ref, buf, sem = rest
    else:
        out_ref, buf, sem = rest
    i = pl.program_id(0)
    row = tile_cond_ref[i]
    slot = i % 2

    def copy(sl, k, r, t):
        src = y_ref.at[pl.ds(pl.multiple_of(t, SUBLANES), ROW_TILES)]
        dst = buf.at[sl, k, pl.ds(pl.multiple_of(r * ROW_TILES, SUBLANES), ROW_TILES)]
        return pltpu.make_async_copy(src, dst, sem.at[sl])

    def gather_tile(tile, sl):
        base = tile * TM

        def issue(r, c):
            copy(sl, 0, r, d_ref[base + r]).start(priority=0)
            copy(sl, 1, r, d_ref[N_TOK + base + r]).start(priority=1)
            return c

        lax.fori_loop(0, TM, issue, 0, unroll=8)

    @pl.when(i == 0)
    def _():
        gather_tile(0, 0)

    @pl.when(i + 1 < N_TILES)
    def _():
        gather_tile(i + 1, 1 - slot)

    def drain(r, c):
        copy(slot, 0, r, 0).wait()
        copy(slot, 1, r, 0).wait()
        return c

    lax.fori_loop(0, TM, drain, 0, unroll=8)
    w = w_ref[...]
    w0 = w[:, 0:1]
    w1 = w[:, 1:2]
    parts = []
    for s in range(ROW_TILES):
        y0 = buf[slot, 0, pl.ds(s, TM, stride=ROW_TILES), :]
        y1 = buf[slot, 1, pl.ds(s, TM, stride=ROW_TILES), :]
        parts.append(w0 * y0 + w1 * y1)
    gate = mod_ref[pl.ds(row, 1), 5 * D_MODEL:6 * D_MODEL]
    out = x1_ref[...] + gate * jnp.concatenate(parts, axis=1)
    if split_out:
        @pl.when(i < N_PROMPT_TILES)
        def _():
            outp_ref[...] = out

        @pl.when(i >= N_PROMPT_TILES)
        def _():
            outs_ref[...] = out
    else:
        out_ref[...] = out


def _moe_combine(dests, x1, weights, mod_l, y, split_out):
    tile_cond, _ = _tile_cond(TM)
    if split_out:
        out_shape = [jax.ShapeDtypeStruct((N_PROMPT, D_MODEL), F32), jax.ShapeDtypeStruct((N_SAMPLE, D_MODEL), F32)]
        out_specs = list(_split_specs(TM, False))
    else:
        out_shape = jax.ShapeDtypeStruct((N_TOK, D_MODEL), F32)
        out_specs = pl.BlockSpec((TM, D_MODEL), lambda i, *_: (i, 0))
    return pl.pallas_call(
        functools.partial(_combine_kernel, split_out=split_out),
        out_shape=out_shape,
        grid_spec=pltpu.PrefetchScalarGridSpec(
            num_scalar_prefetch=2,
            grid=(N_TILES,),
            in_specs=[
                pl.BlockSpec((TM, D_MODEL), lambda i, *_: (i, 0)),
                pl.BlockSpec((TM, 2), lambda i, *_: (i, 0)),
                pl.BlockSpec((COND_ROWS, 6 * D_MODEL), lambda i, *_: (0, 0)),
                pl.BlockSpec(memory_space=pl.ANY),
            ],
            out_specs=out_specs,
            scratch_shapes=[pltpu.VMEM((2, 2, TM * ROW_TILES, LANES), F32), pltpu.SemaphoreType.DMA((2,))],
        ),
        compiler_params=pltpu.CompilerParams(
            dimension_semantics=("arbitrary",), vmem_limit_bytes=VMEM_LIMIT),
        name="moe_combine",
    )(dests, tile_cond, x1, weights, mod_l, y)


def _chunk_plan(segments):
    chunks = []
    for width, normed, dst, dst_col0, f32_dst in segments:
        for k in range(width // LANES):
            chunks.append((normed, dst, dst_col0 + k * LANES, f32_dst, k * LANES))
    return tuple(chunks)


def _head_gain(parts):
    cols = []
    for width, g, mult in parts:
        if g is None:
            cols.append(jnp.ones((width,), F32))
        else:
            cols.append(jnp.tile(g.astype(F32) * mult, width // HEAD_DIM))
    return jnp.concatenate(cols).reshape(1, -1)


def kernel(x_prompt, x_sample, cache_a_k, cache_a_v, cache_b_k, cache_b_v, cache_c_k, cache_c_v, c, c_ctx, w_mod, b_mod, norm_mix, norm_ffn, w_in_ab, w_out_ab, a_q_norm, a_k_norm, b_q_norm, b_k_norm, lam_q1, lam_k1, lam_q2, lam_k2, b_subln, w_in_c, w_out_c, c_q_norm, c_k_norm, c_rpb, w_router, b_router, w_gate, w_up, w_down):
    scale = LOG2E * HEAD_DIM ** -0.5
    cond = jnp.concatenate(
        [c, c_ctx[None, :], jnp.zeros((COND_ROWS - DEC_BATCH - 1, D_MODEL), F32)], axis=0)
    mod = _modulation(cond, w_mod, b_mod)

    xa = x_prompt.reshape(N_PROMPT, D_MODEL)
    xb = x_sample.reshape(N_SAMPLE, D_MODEL)
    merged = False
    new_caches = []
    for l in range(DEPTH):
        mod_l = mod[l]
        if l % 2 == 0:
            e = l // 2
            lam_init = 0.8 - 0.6 * math.exp(-0.3 * l)
            segments = (
                (A_Q_W, True, 0, 0, None),
                (A_KV_W, True, 1, 0, 2),
                (A_KV_W, False, 1, A_KV_W, 3),
                (B_QK_W, True, 0, A_Q_W, None),
                (B_QK_W, True, 1, 2 * A_KV_W, 4),
                (B_V_W, False, 1, 2 * A_KV_W + B_QK_W, 5),
            )
            hg = _head_gain((
                (A_Q_W, a_q_norm[e], scale), (A_KV_W, a_k_norm[e], 1.0), (A_KV_W, None, 1.0),
                (B_QK_W, b_q_norm[e], scale), (B_QK_W, b_k_norm[e], 1.0), (B_V_W, None, 1.0)))
            outs = _lnproj(xa, xb, merged, mod_l, norm_mix[l], w_in_ab[e].astype(BF16), hg, _chunk_plan(segments),
                           A_Q_W + B_QK_W, 2 * A_KV_W + B_QK_W + B_V_W,
                           (A_KV_W, A_KV_W, B_QK_W, B_V_W), True)
            q, kv, ak, av, bk, bv = outs
            new_caches.append((
                ak.reshape(BATCH, SEQ, A_KV_HEADS, HEAD_DIM), av.reshape(BATCH, SEQ, A_KV_HEADS, HEAD_DIM),
                bk.reshape(BATCH, SEQ, B_HEADS, 2, HEAD_DIM), bv.reshape(BATCH, SEQ, B_HEADS, B_V_DIM)))
            lamv = jnp.stack([lam_q1[e], lam_k1[e], lam_q2[e], lam_k2[e]]).astype(F32)
            subln = b_subln[e].reshape(1, B_V_DIM)
            o_p = _attn_ab(q, kv, lamv, subln, lam_init, None)
            caches = (cache_a_k[:, e].reshape(DEC_BATCH, PAST_LEN, A_KV_W),
                      cache_a_v[:, e].reshape(DEC_BATCH, PAST_LEN, A_KV_W),
                      cache_b_k[:, e].reshape(DEC_BATCH, PAST_LEN, B_QK_W),
                      cache_b_v[:, e].reshape(DEC_BATCH, PAST_LEN, B_V_W))
            o_s = _attn_ab(q, kv, lamv, subln, lam_init, caches)
            w_out = w_out_ab[e].astype(BF16)
        else:
            oi = l // 2
            segments = (
                (C_WIDTH, True, 0, 0, None),
                (C_WIDTH, True, 1, 0, 2),
                (C_WIDTH, False, 1, C_WIDTH, 3),
            )
            hg = _head_gain(((C_WIDTH, c_q_norm[oi], scale), (C_WIDTH, c_k_norm[oi], 1.0), (C_WIDTH, None, 1.0)))
            outs = _lnproj(xa, xb, merged, mod_l, norm_mix[l], w_in_c[oi].astype(BF16), hg, _chunk_plan(segments),
                           C_WIDTH, 2 * C_WIDTH, (C_WIDTH, C_WIDTH), False)
            q, kv, ck_new, cv_new = outs
            new_caches.append((ck_new.reshape(BATCH, SEQ, C_HEADS, HEAD_DIM),
                               cv_new.reshape(BATCH, SEQ, C_HEADS, HEAD_DIM)))
            o_p = _attn_c_prompt(q, kv)
            o_s = _attn_na(q, kv,
                           cache_c_k[:, oi].reshape(DEC_BATCH, PAST_LEN, C_WIDTH),
                           cache_c_v[:, oi].reshape(DEC_BATCH, PAST_LEN, C_WIDTH),
                           _na_bias_table(c_rpb[oi]))
            w_out = w_out_c[oi].astype(BF16)

        x1, h2, gates, rank = _outproj(o_p, o_s, xa, xb, merged, w_out, mod_l, norm_ffn[l], w_router, b_router)
        dests, weights, tile_expert, n_tiles, next_expert, clear = _route_plan(gates, rank)
        xs = _moe_scatter(dests, clear, h2)
        y = _moe_mlp(l, tile_expert, n_tiles, next_expert, xs, w_gate, w_up, w_down)
        last = l == DEPTH - 1
        out = _moe_combine(dests, x1, weights, mod_l, y, last)
        if last:
            y_prompt, y_sample = out
        else:
            xa = xb = out
            merged = True

    even = [nc for i, nc in enumerate(new_caches) if i % 2 == 0]
    odd = [nc for i, nc in enumerate(new_caches) if i % 2 == 1]
    stack = lambda items, k: jnp.stack([it[k] for it in items], axis=1)
    return (y_prompt.reshape(BATCH, SEQ, D_MODEL), y_sample.reshape(DEC_BATCH, DEC_SEQ, D_MODEL),
            stack(even, 0), stack(even, 1), stack(even, 2), stack(even, 3),
            stack(odd, 0), stack(odd, 1))
```

```python
import functools
import math

import numpy as np
import jax
import jax.numpy as jnp
from jax import lax
from jax.experimental import pallas as pl
from jax.experimental.pallas import tpu as pltpu

F32 = jnp.float32
BF16 = jnp.bfloat16

D_MODEL = 1024
BATCH = 16
SEQ = 256
DEPTH = 2
DEC_BATCH = 8
DEC_SEQ = 1024
PAST_LEN = 256
GRID_W = 64
HEAD_DIM = 64
ROPE_THETA = 10000.0
RMS_EPS = 1e-6
A_Q_HEADS = 8
A_KV_HEADS = 2
A_GROUP = A_Q_HEADS // A_KV_HEADS
B_HEADS = 4
B_V_DIM = 2 * HEAD_DIM
A_Q_W = A_Q_HEADS * HEAD_DIM
A_KV_W = A_KV_HEADS * HEAD_DIM
B_QK_W = B_HEADS * 2 * HEAD_DIM
B_V_W = B_HEADS * B_V_DIM
C_HEADS = 16
C_WIDTH = C_HEADS * HEAD_DIM
NA_ROWS = 8
NA_COLS = 16
N_EXPERTS = 16
N_GROUPS = 4
EXPERTS_PER_GROUP = N_EXPERTS // N_GROUPS
D_EXPERT = 1024

LANES = 128
SUBLANES = 8
ROW_TILES = D_MODEL // LANES
N_PROMPT = BATCH * SEQ
N_SAMPLE = DEC_BATCH * DEC_SEQ
N_TOK = N_PROMPT + N_SAMPLE
TM = 512
N_TILES = N_TOK // TM
N_PROMPT_TILES = N_PROMPT // TM
TM_LN = 512
LN_PIECE = 512
TM_OUT = 1024
COND_ROWS = 16
CTX_COND_ROW = DEC_BATCH
MOE_TM = 256
MOE_ROWS = 2 * N_TOK + N_EXPERTS * MOE_TM
MOE_TILES = MOE_ROWS // MOE_TM
GRID_ROWS = DEC_SEQ // GRID_W
PROMPTS_PER_STEP = 2
NA_ROWS_PER_STEP = 4
NEG_BIG = -1e30
LOG2E = math.log2(math.e)
VMEM_LIMIT = 56 * 1024 * 1024

def _silu(x):
    return x * (1.0 / (1.0 + jnp.exp(-x)))


def _dot(a, b):
    return jnp.dot(a, b, preferred_element_type=F32)


def _dot_nt(a, b):
    return lax.dot_general(a, b, (((1,), (1,)), ((), ())), preferred_element_type=F32)


def _low_half(shape):
    return lax.broadcasted_iota(jnp.int32, shape, len(shape) - 1) < HEAD_DIM


def _swap_halves(x):
    return jnp.concatenate([x[:, HEAD_DIM:], x[:, :HEAD_DIM]], axis=1)


def _tile_cond(tm):
    tiles = np.arange(N_TOK // tm)
    npt = N_PROMPT // tm
    samp = np.maximum(tiles - npt, 0)
    per_seq = DEC_SEQ // tm
    cond = np.where(tiles < npt, CTX_COND_ROW, samp // per_seq)
    rope = np.where(tiles < npt, per_seq, samp % per_seq)
    return jnp.asarray(cond, jnp.int32), jnp.asarray(rope, jnp.int32)


def _split_specs(tm, merged):
    npt = N_PROMPT // tm
    base = npt if merged else 0
    a = pl.BlockSpec((tm, D_MODEL), lambda i, *_: (jnp.minimum(i, npt - 1), 0))
    b = pl.BlockSpec((tm, D_MODEL), lambda i, *_: (jnp.maximum(i - npt, 0) + base, 0))
    return a, b


def _mod_kernel(c_ref, w_ref, b_ref, o_ref):
    s = _silu(c_ref[...])
    s_hi = s.astype(BF16)
    s_lo = (s - s_hi.astype(F32)).astype(BF16)
    w = w_ref[...].astype(BF16)
    o_ref[...] = _dot(s_hi, w) + _dot(s_lo, w) + b_ref[...]


def _modulation(cond, w_mod, b_mod):
    tn = 1536
    return pl.pallas_call(
        _mod_kernel,
        out_shape=jax.ShapeDtypeStruct((DEPTH, COND_ROWS, 6 * D_MODEL), F32),
        grid=(DEPTH, 6 * D_MODEL // tn),
        in_specs=[
            pl.BlockSpec((COND_ROWS, D_MODEL), lambda l, j: (0, 0)),
            pl.BlockSpec((None, D_MODEL, tn), lambda l, j: (l, 0, j)),
            pl.BlockSpec((None, 1, tn), lambda l, j: (l, 0, j)),
        ],
        out_specs=pl.BlockSpec((None, COND_ROWS, tn), lambda l, j: (l, 0, j)),
        compiler_params=pltpu.CompilerParams(
            dimension_semantics=("arbitrary", "arbitrary"), vmem_limit_bytes=VMEM_LIMIT),
        name="modulation",
    )(cond, w_mod, b_mod.reshape(DEPTH, 1, 6 * D_MODEL))


def _rope_tables(tm):
    pos = np.arange(DEC_SEQ)
    rows = (pos // GRID_W).astype(np.float64)
    cols = (pos % GRID_W).astype(np.float64)
    nfreq = HEAD_DIM // 4
    inv = ROPE_THETA ** (-np.arange(nfreq, dtype=np.float64) / nfreq)
    d = np.arange(HEAD_DIM)
    dd = d % (HEAD_DIM // 2)
    p = np.where((d >= HEAD_DIM // 2)[None, :], cols[:, None], rows[:, None])
    ang = p * inv[dd % nfreq][None, :]
    cos, sin = np.cos(ang), np.sin(ang)
    second = (dd >= nfreq)[None, :]
    sa = np.where(second, sin, 0.0)
    sb = np.where(second, 0.0, -sin)

    def full(t, ident):
        t = np.concatenate([t, np.full((tm, HEAD_DIM), ident)], axis=0)
        return jnp.asarray(np.tile(t, (1, LANES // HEAD_DIM)), dtype=F32)

    return full(cos, 1.0), full(sa, 0.0), full(sb, 0.0)


def _lnproj_kernel(tile_cond_ref, tile_rope_ref, xa_ref, xb_ref, mod_ref, g_ref, w_ref, hg_ref, gmat_ref,
                   cos_ref, sa_ref, sb_ref, *out_refs, chunks, use_rope):
    i = pl.program_id(0)
    row = tile_cond_ref[i]
    x = jnp.where(i >= N_SAMPLE // TM_LN, xa_ref[...], xb_ref[...])
    ms = jnp.mean(x * x, axis=-1, keepdims=True)
    xn = x * lax.rsqrt(ms + RMS_EPS) * g_ref[...]
    shift = mod_ref[pl.ds(row, 1), 0:D_MODEL]
    scale = mod_ref[pl.ds(row, 1), D_MODEL:2 * D_MODEL]
    h = (xn * (1.0 + scale) + shift).astype(BF16)
    piece_chunks = LN_PIECE // LANES
    for c0 in range(0, len(chunks), piece_chunks):
        n_sub = min(piece_chunks, len(chunks) - c0)
        yp = _dot(h, w_ref[:, c0 * LANES:(c0 + n_sub) * LANES])
        yn = [None] * n_sub
        for s0 in range(0, n_sub, 2):
            if chunks[c0 + s0][0] or chunks[c0 + s0 + 1][0]:
                y2 = yp[:, s0 * LANES:(s0 + 2) * LANES]
                gs = _dot((y2 * y2).astype(BF16), gmat_ref[...])
                yn2 = y2 * lax.rsqrt(gs * (1.0 / HEAD_DIM) + RMS_EPS)
                yn[s0], yn[s0 + 1] = yn2[:, :LANES], yn2[:, LANES:]
        for sub in range(n_sub):
            c = c0 + sub
            normed, dst, dst_col, f32_dst, f32_col = chunks[c]
            if normed:
                y = yn[sub] * hg_ref[:, c * LANES:(c + 1) * LANES]
            else:
                y = yp[:, sub * LANES:(sub + 1) * LANES]
            if f32_dst is not None:
                out_refs[f32_dst][:, f32_col:f32_col + LANES] = y
            if normed and use_rope:
                y = (y * cos_ref[...] + pltpu.roll(y, HEAD_DIM // 4, 1) * sa_ref[...]
                     + pltpu.roll(y, LANES - HEAD_DIM // 4, 1) * sb_ref[...])
            out_refs[dst][:, dst_col:dst_col + LANES] = y.astype(BF16)


def _lnproj(xa, xb, merged, mod_l, gain, w_bf16, head_gain, chunks, q_w, kv_w, f32_widths, use_rope):
    dout = w_bf16.shape[1]
    tm = TM_LN
    npt = N_PROMPT // tm
    nst = N_SAMPLE // tm
    cond_tok, rope_tok = _tile_cond(tm)
    order = np.concatenate([np.arange(npt, npt + nst), np.arange(npt)])
    tile_cond, tile_rope = cond_tok[order], rope_tok[order]
    cos, sa, sb = _rope_tables(tm)
    gmat = jnp.asarray(np.kron(np.eye(2 * LANES // HEAD_DIM), np.ones((HEAD_DIM, HEAD_DIM))), dtype=BF16)
    const = lambda i, *_: (0, 0)
    tok = lambda i, *_: (jnp.where(i < nst, i + npt, i - nst), 0)
    rope_map = lambda i, tc, tr: (tr[i], 0)
    prm = lambda i, *_: (jnp.maximum(i - nst, 0), 0)
    base = npt if merged else 0
    xa_spec = pl.BlockSpec((tm, D_MODEL), prm)
    xb_spec = pl.BlockSpec((tm, D_MODEL), lambda i, *_: (jnp.minimum(i, nst - 1) + base, 0))
    out_shape = [jax.ShapeDtypeStruct((N_TOK, q_w), BF16), jax.ShapeDtypeStruct((N_TOK, kv_w), BF16)]
    out_specs = [pl.BlockSpec((tm, q_w), tok), pl.BlockSpec((tm, kv_w), tok)]
    for wd in f32_widths:
        out_shape.append(jax.ShapeDtypeStruct((N_PROMPT, wd), F32))
        out_specs.append(pl.BlockSpec((tm, wd), prm))
    return pl.pallas_call(
        functools.partial(_lnproj_kernel, chunks=chunks, use_rope=use_rope),
        out_shape=out_shape,
        grid_spec=pltpu.PrefetchScalarGridSpec(
            num_scalar_prefetch=2,
            grid=(N_TOK // tm,),
            in_specs=[
                xa_spec, xb_spec,
                pl.BlockSpec((COND_ROWS, 6 * D_MODEL), const),
                pl.BlockSpec((1, D_MODEL), const),
                pl.BlockSpec((D_MODEL, dout), const),
                pl.BlockSpec((1, dout), const),
                pl.BlockSpec((2 * LANES, 2 * LANES), const),
                pl.BlockSpec((tm, LANES), rope_map),
                pl.BlockSpec((tm, LANES), rope_map),
                pl.BlockSpec((tm, LANES), rope_map),
            ],
            out_specs=out_specs,
        ),
        compiler_params=pltpu.CompilerParams(
            dimension_semantics=("arbitrary",), vmem_limit_bytes=VMEM_LIMIT),
        name="lnproj",
    )(tile_cond, tile_rope, xa, xb, mod_l, gain.reshape(1, D_MODEL), w_bf16, head_gain, gmat, cos, sa, sb)


def _softmax_parts(s, want_sum=True):
    p = jnp.exp2(s - jnp.max(s, axis=-1, keepdims=True))
    return p, (jnp.sum(p, axis=-1, keepdims=True) if want_sum else None)


def _gqa_pairs(q_ref, q_col0, kv_ref, k_col0, v_col0, n_kv, group, tq):
    low = _low_half((tq, LANES))
    mxu_sums = group > 1
    head_out = [None] * (n_kv * group)
    jobs = [(pair, half) for pair in range(n_kv // 2) for half in range(2)]
    scores = []
    for pair, half in jobs:
        k_pair = kv_ref[:, k_col0 + pair * LANES:k_col0 + (pair + 1) * LANES]
        kvh = 2 * pair + half
        keep = low if half == 0 else jnp.logical_not(low)
        qs = []
        for g in range(group):
            head = kvh * group + g
            blk = q_ref[:, q_col0 + (head // 2) * LANES:q_col0 + (head // 2 + 1) * LANES]
            if head % 2 != half:
                blk = _swap_halves(blk)
            qs.append(jnp.where(keep, blk, jnp.zeros_like(blk)))
        q = qs[0] if group == 1 else jnp.concatenate(qs, axis=0)
        scores.append(_dot_nt(q, k_pair))

    def finish():
        for (pair, half), s in zip(jobs, scores):
            v = kv_ref[:, v_col0 + pair * LANES:v_col0 + (pair + 1) * LANES]
            kvh = 2 * pair + half
            if mxu_sums:
                own = _low_half(v.shape) if half == 0 else jnp.logical_not(_low_half(v.shape))
                v = jnp.where(own, v, jnp.ones_like(v))
            p, l = _softmax_parts(s, want_sum=not mxu_sums)
            o = _dot(p.astype(BF16), v)
            o = o * (1.0 / (pltpu.roll(o, HEAD_DIM, 1) if mxu_sums else l))
            for g in range(group):
                head = kvh * group + g
                og = o[g * tq:(g + 1) * tq]
                if head % 2 != half:
                    og = pltpu.roll(og, HEAD_DIM, 1)
                head_out[head] = og
        return [jnp.where(low, head_out[2 * k], head_out[2 * k + 1]) for k in range(n_kv * group // 2)]

    return finish


def _attn_ab_kernel(*refs, has_cache, tq, n_seq, lam_init):
    if has_cache:
        q_ref, kvn_ref, cak_ref, cav_ref, cbk_ref, cbv_ref, lamv_ref, subln_ref, o_ref, kv_ref = refs

        @pl.when(pl.program_id(1) == 0)
        def _():
            col = 0
            for c_ref in (cak_ref, cav_ref, cbk_ref, cbv_ref):
                w = c_ref.shape[1]
                kv_ref[0:PAST_LEN, col:col + w] = c_ref[...].astype(BF16)
                col += w
            kv_ref[PAST_LEN:, :] = kvn_ref[...]
    else:
        q_ref, kv_ref, lamv_ref, subln_ref, o_ref = refs

    lv = lamv_ref[...]
    l1 = jnp.sum(lv[0:1] * lv[1:2], axis=-1, keepdims=True)
    l2 = jnp.sum(lv[2:3] * lv[3:4], axis=-1, keepdims=True)
    lam = jnp.exp(l1) - jnp.exp(l2) + lam_init
    bk0 = 2 * A_KV_W
    bv0 = bk0 + B_QK_W
    low = _low_half((tq, LANES))
    t_kv = kv_ref.shape[0] // n_seq
    for b in range(n_seq):
        q_b = q_ref.at[b * tq:(b + 1) * tq]
        kv_b = kv_ref.at[b * t_kv:(b + 1) * t_kv]
        finish_a = _gqa_pairs(q_b, 0, kv_b, 0, A_KV_W, A_KV_HEADS, A_GROUP, tq)
        b_scores = []
        for h in range(B_HEADS):
            lanes = slice(h * LANES, (h + 1) * LANES)
            qp = q_b[:, A_Q_W + lanes.start:A_Q_W + lanes.stop]
            zero = jnp.zeros_like(qp)
            q = jnp.concatenate([jnp.where(low, qp, zero), jnp.where(low, zero, qp)], axis=0)
            b_scores.append(_dot_nt(q, kv_b[:, bk0 + lanes.start:bk0 + lanes.stop]))
        outs = finish_a()
        for h in range(B_HEADS):
            lanes = slice(h * LANES, (h + 1) * LANES)
            p, l = _softmax_parts(b_scores[h])
            r = 1.0 / l
            a = p[:tq] * r[:tq] - p[tq:] * (lam * r[tq:])
            o = _dot(a.astype(BF16), kv_b[:, bv0 + lanes.start:bv0 + lanes.stop])
            ms = jnp.mean(o * o, axis=-1, keepdims=True)
            o = o * lax.rsqrt(ms + RMS_EPS) * subln_ref[...] * (1.0 - lam_init)
            outs.append(o)
        o_ref[b * tq:(b + 1) * tq, :] = jnp.concatenate(outs, axis=1).astype(BF16)


def _attn_ab(q, kv, lamv, subln, lam_init, caches):
    kv_w = kv.shape[1]
    kern = functools.partial(_attn_ab_kernel, lam_init=lam_init)
    const = lambda b, j: (0, 0)
    cp = pltpu.CompilerParams(dimension_semantics=("arbitrary", "arbitrary"), vmem_limit_bytes=VMEM_LIMIT)
    if caches is None:
        n_seq = 1
        rows = n_seq * SEQ
        return pl.pallas_call(
            functools.partial(kern, has_cache=False, tq=SEQ, n_seq=n_seq),
            out_shape=jax.ShapeDtypeStruct((N_PROMPT, D_MODEL), BF16),
            grid=(BATCH // n_seq, 1),
            in_specs=[
                pl.BlockSpec((rows, D_MODEL), lambda b, j: (b, 0)),
                pl.BlockSpec((rows, kv_w), lambda b, j: (b, 0)),
                pl.BlockSpec((4, HEAD_DIM), const),
                pl.BlockSpec((1, B_V_DIM), const),
            ],
            out_specs=pl.BlockSpec((rows, D_MODEL), lambda b, j: (b, 0)),
            compiler_params=cp,
            name="attn_ab_prompt",
        )(q, kv, lamv, subln)
    tq = 256
    nq = DEC_SEQ // tq
    q0 = N_PROMPT // tq
    kv0 = N_PROMPT // DEC_SEQ
    cak, cav, cbk, cbv = caches
    cspec = lambda w: pl.BlockSpec((None, PAST_LEN, w), lambda b, j: (b, 0, 0))
    return pl.pallas_call(
        functools.partial(kern, has_cache=True, tq=tq, n_seq=1),
        out_shape=jax.ShapeDtypeStruct((N_SAMPLE, D_MODEL), BF16),
        grid=(DEC_BATCH, nq),
        in_specs=[
            pl.BlockSpec((tq, D_MODEL), lambda b, j: (q0 + b * nq + j, 0)),
            pl.BlockSpec((DEC_SEQ, kv_w), lambda b, j: (kv0 + b, 0)),
            cspec(A_KV_W), cspec(A_KV_W), cspec(B_QK_W), cspec(B_V_W),
            pl.BlockSpec((4, HEAD_DIM), const),
            pl.BlockSpec((1, B_V_DIM), const),
        ],
        out_specs=pl.BlockSpec((tq, D_MODEL), lambda b, j: (b * nq + j, 0)),
        scratch_shapes=[pltpu.VMEM((PAST_LEN + DEC_SEQ, kv_w), BF16)],
        compiler_params=cp,
        name="attn_ab_sample",
    )(q, kv, cak, cav, cbk, cbv, lamv, subln)


def _attn_c_prompt_kernel(q_ref, kv_ref, o_ref):
    for b in range(PROMPTS_PER_STEP):
        rows = slice(b * SEQ, (b + 1) * SEQ)
        outs = _gqa_pairs(q_ref.at[rows], 0, kv_ref.at[rows], 0, C_WIDTH, C_HEADS, 1, SEQ)()
        o_ref[rows, :] = jnp.concatenate(outs, axis=1).astype(BF16)


def _attn_c_prompt(q, kv):
    rows = PROMPTS_PER_STEP * SEQ
    return pl.pallas_call(
        _attn_c_prompt_kernel,
        out_shape=jax.ShapeDtypeStruct((N_PROMPT, D_MODEL), BF16),
        grid=(BATCH // PROMPTS_PER_STEP,),
        in_specs=[
            pl.BlockSpec((rows, C_WIDTH), lambda b: (b, 0)),
            pl.BlockSpec((rows, 2 * C_WIDTH), lambda b: (b, 0)),
        ],
        out_specs=pl.BlockSpec((rows, C_WIDTH), lambda b: (b, 0)),
        compiler_params=pltpu.CompilerParams(dimension_semantics=("arbitrary",), vmem_limit_bytes=VMEM_LIMIT),
        name="attn_c_prompt",
    )(q, kv)


def _na_kernel(q_ref, kv_ref, ck_ref, cv_ref, tp_ref, o_ref, ckb, cvb):
    step = pl.program_id(1)

    @pl.when(step == 0)
    def _():
        ckb[...] = ck_ref[...].astype(BF16)
        cvb[...] = cv_ref[...].astype(BF16)

    kh = min(NA_ROWS, GRID_ROWS)
    win = kh * GRID_W
    low = _low_half((GRID_W, LANES))
    n_pair = C_HEADS // 2
    for rr in range(NA_ROWS_PER_STEP):
        r = step * NA_ROWS_PER_STEP + rr
        rows = slice(rr * GRID_W, (rr + 1) * GRID_W)
        rs = jnp.clip(r - kh // 2, 0, GRID_ROWS - kh)
        ro0 = rs - r + (NA_ROWS - 1)
        start = pl.multiple_of(rs * GRID_W, GRID_W)
        s_lat, s_ctx = [], []
        for j in range(n_pair):
            lanes = slice(j * LANES, (j + 1) * LANES)
            qp = q_ref[rows, lanes]
            zero = jnp.zeros_like(qp)
            q = jnp.concatenate([jnp.where(low, qp, zero), jnp.where(low, zero, qp)], axis=0)
            kw = kv_ref[pl.ds(start, win), lanes]
            bias = jnp.concatenate(
                [jnp.concatenate([tp_ref[2 * j + hh, ro0 + 2 * t] for t in range(kh // 2)], axis=1)
                 for hh in range(2)], axis=0)
            s_lat.append(_dot_nt(q, kw) + bias)
            s_ctx.append(_dot_nt(q, ckb[:, lanes]))
        outs = []
        for j in range(n_pair):
            lanes = slice(j * LANES, (j + 1) * LANES)
            s_l, s_c = s_lat[j], s_ctx[j]
            m = jnp.maximum(jnp.max(s_l, axis=-1, keepdims=True), jnp.max(s_c, axis=-1, keepdims=True))
            p_l = jnp.exp2(s_l - m)
            p_c = jnp.exp2(s_c - m)
            l = jnp.sum(p_l, axis=-1, keepdims=True) + jnp.sum(p_c, axis=-1, keepdims=True)
            vw = kv_ref[pl.ds(start, win), C_WIDTH + lanes.start:C_WIDTH + lanes.stop]
            o = _dot(p_c.astype(BF16), cvb[:, lanes]) + _dot(p_l.astype(BF16), vw)
            o = o * (1.0 / l)
            outs.append(jnp.where(low, o[:GRID_W], o[GRID_W:]))
        o_ref[rows, :] = jnp.concatenate(outs, axis=1).astype(BF16)


NA_BIAS_PAD = GRID_W - NA_COLS


def _na_bias_kernel(w_ref, mask_ref, tp_ref):
    low = _low_half((GRID_W, LANES))
    keep = mask_ref[...] > 0.0
    n_off = 2 * NA_ROWS - 1
    left, right = [], []
    for ro in range(n_off):
        row = jnp.broadcast_to(w_ref[ro:ro + 1, :] * LOG2E, (GRID_W, LANES))
        left.append(pltpu.roll(row, LANES - GRID_W + 1, 1, stride=1, stride_axis=0))
        right.append(pltpu.roll(row, 1, 1, stride=1, stride_axis=0))
    for t in range(n_off - 1):
        tp_ref[t] = jnp.where(keep, jnp.where(low, left[t], right[t + 1]), NEG_BIG)


def _na_bias_table(rpb):
    cols = np.arange(GRID_W)
    col_start = np.clip(cols - NA_COLS // 2, 0, GRID_W - NA_COLS)
    col_in = (cols[None, :] >= col_start[:, None]) & (cols[None, :] < col_start[:, None] + NA_COLS)
    assert np.abs((cols[None, :] - cols[:, None])[col_in]).max() <= NA_COLS - 1
    mask = jnp.asarray(np.tile(col_in, (1, LANES // GRID_W)), dtype=F32)
    n_off = 2 * NA_ROWS - 1
    n_rel = 2 * NA_COLS - 1
    w = jnp.pad(rpb.astype(F32), ((0, 0), (0, 0), (NA_BIAS_PAD, LANES - NA_BIAS_PAD - n_rel)))
    return pl.pallas_call(
        _na_bias_kernel,
        out_shape=jax.ShapeDtypeStruct((C_HEADS, n_off - 1, GRID_W, LANES), F32),
        grid=(C_HEADS,),
        in_specs=[
            pl.BlockSpec((None, n_off, LANES), lambda h: (h, 0, 0)),
            pl.BlockSpec((GRID_W, LANES), lambda h: (0, 0)),
        ],
        out_specs=pl.BlockSpec((None, n_off - 1, GRID_W, LANES), lambda h: (h, 0, 0, 0)),
        compiler_params=pltpu.CompilerParams(dimension_semantics=("arbitrary",)),
        name="na_bias",
    )(w, mask)


def _attn_na(q, kv, ck, cv, tp):
    tq = NA_ROWS_PER_STEP * GRID_W
    steps = GRID_ROWS // NA_ROWS_PER_STEP
    q0 = N_PROMPT // tq
    kv0 = N_PROMPT // DEC_SEQ
    return pl.pallas_call(
        _na_kernel,
        out_shape=jax.ShapeDtypeStruct((N_SAMPLE, D_MODEL), BF16),
        grid=(DEC_BATCH, steps),
        in_specs=[
            pl.BlockSpec((tq, C_WIDTH), lambda b, r: (q0 + b * steps + r, 0)),
            pl.BlockSpec((DEC_SEQ, 2 * C_WIDTH), lambda b, r: (kv0 + b, 0)),
            pl.BlockSpec((None, PAST_LEN, C_WIDTH), lambda b, r: (b, 0, 0)),
            pl.BlockSpec((None, PAST_LEN, C_WIDTH), lambda b, r: (b, 0, 0)),
            pl.BlockSpec(tp.shape, lambda b, r: (0, 0, 0, 0)),
        ],
        out_specs=pl.BlockSpec((tq, C_WIDTH), lambda b, r: (b * steps + r, 0)),
        scratch_shapes=[pltpu.VMEM((PAST_LEN, C_WIDTH), BF16), pltpu.VMEM((PAST_LEN, C_WIDTH), BF16)],
        compiler_params=pltpu.CompilerParams(
            dimension_semantics=("arbitrary", "arbitrary"), vmem_limit_bytes=VMEM_LIMIT),
        name="attn_na_sample",
    )(q, kv, ck, cv, tp)


def _first_wins_ranks(vals):
    ranks = []
    for i in range(len(vals)):
        r = jnp.zeros_like(vals[i])
        for j in range(len(vals)):
            if j == i:
                continue
            beats = (vals[j] >= vals[i]) if j < i else (vals[j] > vals[i])
            r = r + jnp.where(beats, 1.0, 0.0)
        ranks.append(r)
    return ranks


def _outproj_kernel(tile_cond_ref, op_ref, os_ref, xa_ref, xb_ref, w_ref, mod_ref, g2_ref, wrt_ref, br_ref,
                    tri_ref, x1_ref, h2_ref, gates_ref, rank_ref, carry_ref):
    i = pl.program_id(0)
    row = tile_cond_ref[i]
    is_prompt = i < N_PROMPT // TM_OUT

    @pl.when(i == 0)
    def _():
        carry_ref[...] = jnp.zeros_like(carry_ref)

    o = jnp.where(is_prompt, op_ref[...], os_ref[...])
    x = jnp.where(is_prompt, xa_ref[...], xb_ref[...])
    acc = _dot(o, w_ref[...])
    gate = mod_ref[pl.ds(row, 1), 2 * D_MODEL:3 * D_MODEL]
    x1 = x + gate * acc
    x1_ref[...] = x1
    ms = jnp.mean(x1 * x1, axis=-1, keepdims=True)
    xn = x1 * lax.rsqrt(ms + RMS_EPS) * g2_ref[...]
    shift = mod_ref[pl.ds(row, 1), 3 * D_MODEL:4 * D_MODEL]
    scale = mod_ref[pl.ds(row, 1), 4 * D_MODEL:5 * D_MODEL]
    h2 = xn * (1.0 + scale) + shift
    for s in range(ROW_TILES):
        h2_ref[pl.ds(s, TM_OUT, stride=ROW_TILES), :] = h2[:, s * LANES:(s + 1) * LANES]

    h_hi = h2.astype(BF16)
    h_lo = (h2 - h_hi.astype(F32)).astype(BF16)
    part = _dot_nt(wrt_ref[...], h_hi)
    logits = part[:N_EXPERTS] + part[N_EXPERTS:] + _dot_nt(wrt_ref[:N_EXPERTS, :], h_lo)
    e = jnp.exp(logits - jnp.max(logits, axis=0, keepdims=True))
    scores = e * (1.0 / jnp.sum(e, axis=0, keepdims=True))
    sel = scores + br_ref[...]
    sel_rows = [sel[k:k + 1, :] for k in range(N_EXPERTS)]
    in_top2 = []
    group_sum = []
    for g in range(N_GROUPS):
        vals = sel_rows[g * EXPERTS_PER_GROUP:(g + 1) * EXPERTS_PER_GROUP]
        ranks = _first_wins_ranks(vals)
        top = [rk < 2.0 for rk in ranks]
        in_top2.extend(top)
        s = jnp.zeros_like(vals[0])
        for v, t in zip(vals, top):
            s = s + jnp.where(t, v, 0.0)
        group_sum.append(s)
    group_rank = _first_wins_ranks(group_sum)
    mask_rows = []
    for k in range(N_EXPERTS):
        chosen = jnp.where(in_top2[k], 1.0, 0.0) * jnp.where(group_rank[k // EXPERTS_PER_GROUP] < 1.0, 1.0, 0.0)
        mask_rows.append(chosen)
    mask = jnp.concatenate(mask_rows, axis=0)
    picked = scores * mask
    gates_ref[...] = picked * (1.0 / jnp.sum(picked, axis=0, keepdims=True))
    prefix = _dot(mask.astype(BF16), tri_ref[...])
    rank_ref[...] = jnp.where(mask > 0.0, prefix + carry_ref[...], -1.0)
    carry_ref[...] = carry_ref[...] + jnp.sum(mask, axis=1, keepdims=True)


def _outproj(o_prompt, o_sample, xa, xb, merged, w_bf16, mod_l, gain2, w_router, b_router):
    const = lambda i, *_: (0, 0)
    tok = lambda i, *_: (i, 0)
    tokT = lambda i, *_: (0, i)
    tm = TM_OUT
    tile_cond, _ = _tile_cond(tm)
    op_spec, os_spec = _split_specs(tm, False)
    xa_spec, xb_spec = _split_specs(tm, merged)
    tri = jnp.asarray(np.triu(np.ones((tm, tm)), k=1), dtype=BF16)
    wrt = w_router.T.astype(F32)
    wrt_hi = wrt.astype(BF16)
    wrt_split = jnp.concatenate([wrt_hi, (wrt - wrt_hi.astype(F32)).astype(BF16)], axis=0)
    return pl.pallas_call(
        _outproj_kernel,
        out_shape=[
            jax.ShapeDtypeStruct((N_TOK, D_MODEL), F32),
            jax.ShapeDtypeStruct((N_TOK * ROW_TILES, LANES), F32),
            jax.ShapeDtypeStruct((N_EXPERTS, N_TOK), F32),
            jax.ShapeDtypeStruct((N_EXPERTS, N_TOK), F32),
        ],
        grid_spec=pltpu.PrefetchScalarGridSpec(
            num_scalar_prefetch=1,
            grid=(N_TOK // tm,),
            in_specs=[
                op_spec, os_spec, xa_spec, xb_spec,
                pl.BlockSpec((D_MODEL, D_MODEL), const),
                pl.BlockSpec((COND_ROWS, 6 * D_MODEL), const),
                pl.BlockSpec((1, D_MODEL), const),
                pl.BlockSpec((2 * N_EXPERTS, D_MODEL), const),
                pl.BlockSpec((N_EXPERTS, 1), const),
                pl.BlockSpec((tm, tm), const),
            ],
            out_specs=[
                pl.BlockSpec((tm, D_MODEL), tok),
                pl.BlockSpec((tm * ROW_TILES, LANES), tok),
                pl.BlockSpec((N_EXPERTS, tm), tokT),
                pl.BlockSpec((N_EXPERTS, tm), tokT),
            ],
            scratch_shapes=[pltpu.VMEM((N_EXPERTS, 1), F32)],
        ),
        compiler_params=pltpu.CompilerParams(
            dimension_semantics=("arbitrary",), vmem_limit_bytes=VMEM_LIMIT),
        name="outproj_router",
    )(tile_cond, o_prompt, o_sample, xa, xb, w_bf16, mod_l, gain2.reshape(1, D_MODEL), wrt_split,
      b_router.reshape(N_EXPERTS, 1), tri)


def _plan_kernel(gates_ref, rank_ref, dests_ref, wts_ref, last_ref, misc_ref):
    rank = rank_ref[...]
    gates = gates_ref[...]
    sel = rank >= 0.0
    hit = jnp.where(sel, 1.0, 0.0)
    tile = float(MOE_TM)

    def pad_up(c):
        return jnp.floor((c + (tile - 1.0)) * (1.0 / tile)) * tile

    counts_col = jnp.sum(hit, axis=1, keepdims=True)
    ones = jnp.ones((SUBLANES, hit.shape[1]), BF16)
    counts_row = _dot_nt(ones, hit.astype(BF16))[0:1]
    padded_col = pad_up(counts_col)
    padded_row = pad_up(counts_row)
    e_sub = lax.broadcasted_iota(jnp.int32, (N_EXPERTS, N_EXPERTS), 0)
    e_lane = lax.broadcasted_iota(jnp.int32, (N_EXPERTS, N_EXPERTS), 1)
    ends_col = jnp.sum(jnp.where(e_lane <= e_sub, padded_row, 0.0), axis=1, keepdims=True)
    ends_row = jnp.sum(jnp.where(e_sub <= e_lane, padded_col, 0.0), axis=0, keepdims=True)
    dest = jnp.where(sel, ends_col - padded_col + rank, -1.0)
    d1 = jnp.max(dest, axis=0, keepdims=True)
    d0 = jnp.min(jnp.where(sel, dest, float(MOE_ROWS)), axis=0, keepdims=True)
    w0 = jnp.sum(jnp.where(dest == d0, gates, 0.0), axis=0, keepdims=True)
    w1 = jnp.sum(jnp.where(dest == d1, gates, 0.0), axis=0, keepdims=True)
    dests_ref[...] = (jnp.concatenate([d0, d1], axis=0) * float(ROW_TILES)).astype(jnp.int32)
    wts_ref[...] = jnp.concatenate([w0, w1], axis=0)

    last = jnp.where(padded_col > 0.0, (ends_col - tile) * float(ROW_TILES), -1.0)
    last_ref[...] = jnp.broadcast_to(last, last_ref.shape).astype(jnp.int32)

    lane = lax.broadcasted_iota(jnp.int32, (1, LANES), 1).astype(F32)
    n_tiles = ends_row[:, N_EXPERTS - 1:N_EXPERTS] * (1.0 / tile)
    tile_expert = jnp.sum(jnp.where(lane * tile >= ends_col, 1.0, 0.0), axis=0, keepdims=True)
    tile_expert = jnp.minimum(tile_expert, float(N_EXPERTS - 1))
    e_col = lax.broadcasted_iota(jnp.int32, (N_EXPERTS, 1), 0).astype(F32)
    later = jnp.where((e_col > tile_expert) & (padded_col > 0.0), e_col, float(N_EXPERTS))
    nxt = jnp.min(later, axis=0, keepdims=True)
    nxt = jnp.where(nxt > float(N_EXPERTS - 1), tile_expert, nxt)
    spare = n_tiles + lane
    spare = jnp.where(spare < float(MOE_TILES), spare * (tile * ROW_TILES), -1.0)
    zero = jnp.zeros((1, LANES), F32)
    rows = [tile_expert, nxt, spare, n_tiles + zero] + [zero] * (SUBLANES - 4)
    misc_ref[...] = jnp.concatenate(rows, axis=0).astype(jnp.int32)


def _route_plan(gates, rank):
    assert MOE_TILES <= LANES and N_EXPERTS <= LANES
    full = lambda shape: pl.BlockSpec(shape, lambda: (0,) * len(shape))
    dests, wts, last, misc = pl.pallas_call(
        _plan_kernel,
        out_shape=[
            jax.ShapeDtypeStruct((2, N_TOK), jnp.int32),
            jax.ShapeDtypeStruct((2, N_TOK), F32),
            jax.ShapeDtypeStruct((N_EXPERTS, LANES), jnp.int32),
            jax.ShapeDtypeStruct((SUBLANES, LANES), jnp.int32),
        ],
        in_specs=[full((N_EXPERTS, N_TOK)), full((N_EXPERTS, N_TOK))],
        out_specs=[full((2, N_TOK)), full((2, N_TOK)), full((N_EXPERTS, LANES)), full((SUBLANES, LANES))],
        compiler_params=pltpu.CompilerParams(vmem_limit_bytes=VMEM_LIMIT),
        name="route_plan",
    )(gates, rank)
    tile_expert = misc[0, :MOE_TILES]
    next_expert = misc[1, :MOE_TILES]
    n_tiles = misc[3, :1]
    clear = jnp.concatenate([last[:, 0], misc[2, :N_EXPERTS]])
    return dests.reshape(2 * N_TOK), wts.T, tile_expert, n_tiles, next_expert, clear


def _scatter_kernel(d_ref, last_ref, h_ref, xs_ref, zeros, sem, zsem):
    i = pl.program_id(0)
    base = i * TM
    rows = MOE_TM * ROW_TILES

    @pl.when(i == 0)
    def _():
        zeros[...] = jnp.zeros_like(zeros)
        for e in range(2 * N_EXPERTS):
            @pl.when(last_ref[e] >= 0)
            def _():
                at = pl.multiple_of(last_ref[e], SUBLANES)
                pltpu.make_async_copy(zeros, xs_ref.at[pl.ds(at, rows)], zsem).start()
        for e in range(2 * N_EXPERTS):
            @pl.when(last_ref[e] >= 0)
            def _():
                pltpu.make_async_copy(zeros, xs_ref.at[pl.ds(0, rows)], zsem).wait()

    def copy(r, t):
        src = h_ref.at[pl.ds(pl.multiple_of(r * ROW_TILES, SUBLANES), ROW_TILES)]
        dst = xs_ref.at[pl.ds(pl.multiple_of(t, SUBLANES), ROW_TILES)]
        return pltpu.make_async_copy(src, dst, sem)

    def issue(r, c):
        copy(r, d_ref[base + r]).start(priority=0)
        copy(r, d_ref[N_TOK + base + r]).start(priority=1)
        return c

    lax.fori_loop(0, TM, issue, 0, unroll=8)

    def drain(r, c):
        copy(r, 0).wait()
        copy(r, 0).wait()
        return c

    lax.fori_loop(0, TM, drain, 0, unroll=8)


def _moe_scatter(dests, last_tile, h2):
    return pl.pallas_call(
        _scatter_kernel,
        out_shape=jax.ShapeDtypeStruct((MOE_ROWS * ROW_TILES, LANES), F32),
        grid_spec=pltpu.PrefetchScalarGridSpec(
            num_scalar_prefetch=2,
            grid=(N_TILES,),
            in_specs=[pl.BlockSpec((TM * ROW_TILES, LANES), lambda i, *_: (i, 0))],
            out_specs=pl.BlockSpec(memory_space=pl.ANY),
            scratch_shapes=[
                pltpu.VMEM((MOE_TM * ROW_TILES, LANES), F32),
                pltpu.SemaphoreType.DMA,
                pltpu.SemaphoreType.DMA,
            ],
        ),
        compiler_params=pltpu.CompilerParams(
            dimension_semantics=("arbitrary",), vmem_limit_bytes=VMEM_LIMIT, has_side_effects=True),
        name="moe_scatter",
    )(dests, last_tile, h2)


def _mlp_kernel(te_ref, nv_ref, nxt_ref, xs_ref, wg_hbm, wu_hbm, wd_hbm, y_ref, stage, wgb, wub, wdb, hb, sem,
                *, layer):
    t = pl.program_id(0)

    def fetch(e):
        return [pltpu.make_async_copy(w.at[layer, e], stage.at[k], sem.at[k])
                for k, w in enumerate((wg_hbm, wu_hbm, wd_hbm))]

    @pl.when(t < nv_ref[0])
    def _():
        e = te_ref[t]
        prev = te_ref[jnp.maximum(t - 1, 0)]

        @pl.when(t == 0)
        def _():
            for cp in fetch(e):
                cp.start()

        @pl.when((t == 0) | (e != prev))
        def _():
            for cp in fetch(e):
                cp.wait()
            wgb[...] = stage[0].astype(BF16)
            wub[...] = stage[1].astype(BF16)
            wdb[...] = stage[2].astype(BF16)

            @pl.when(nxt_ref[t] != e)
            def _():
                for cp in fetch(nxt_ref[t]):
                    cp.start()

        x = jnp.concatenate(
            [xs_ref[pl.ds(s, MOE_TM, stride=ROW_TILES), :] for s in range(ROW_TILES)], axis=1).astype(BF16)
        step = 512
        for c in range(0, D_EXPERT, step):
            g = _dot(x, wgb[:, c:c + step])
            u = _dot(x, wub[:, c:c + step])
            hb[:, c:c + step] = (_silu(g) * u).astype(BF16)
        y = _dot(hb[...], wdb[...])
        for s in range(ROW_TILES):
            y_ref[pl.ds(s, MOE_TM, stride=ROW_TILES), :] = y[:, s * LANES:(s + 1) * LANES]

    @pl.when(t >= nv_ref[0])
    def _():
        y_ref[...] = jnp.zeros_like(y_ref)


def _moe_mlp(layer, tile_expert, n_tiles, next_expert, xs, w_gate, w_up, w_down):
    assert D_MODEL == D_EXPERT

    def tile_map(t, te, nv, nx):
        return (jnp.minimum(t, nv[0] - 1), 0)

    def out_map(t, te, nv, nx):
        return (t, 0)

    hbm = pl.BlockSpec(memory_space=pl.ANY)
    return pl.pallas_call(
        functools.partial(_mlp_kernel, layer=layer),
        out_shape=jax.ShapeDtypeStruct((MOE_ROWS * ROW_TILES, LANES), F32),
        grid_spec=pltpu.PrefetchScalarGridSpec(
            num_scalar_prefetch=3,
            grid=(MOE_TILES,),
            in_specs=[pl.BlockSpec((MOE_TM * ROW_TILES, LANES), tile_map), hbm, hbm, hbm],
            out_specs=pl.BlockSpec((MOE_TM * ROW_TILES, LANES), out_map),
            scratch_shapes=[
                pltpu.VMEM((3, D_MODEL, D_EXPERT), F32),
                pltpu.VMEM((D_MODEL, D_EXPERT), BF16),
                pltpu.VMEM((D_MODEL, D_EXPERT), BF16),
                pltpu.VMEM((D_EXPERT, D_MODEL), BF16),
                pltpu.VMEM((MOE_TM, D_EXPERT), BF16),
                pltpu.SemaphoreType.DMA((3,)),
            ],
        ),
        compiler_params=pltpu.CompilerParams(
            dimension_semantics=("arbitrary",), vmem_limit_bytes=VMEM_LIMIT),
        name="moe_mlp",
    )(tile_expert, n_tiles, next_expert, xs, w_gate, w_up, w_down)


def _combine_kernel(d_ref, tile_cond_ref, x1_ref, w_ref, mod_ref, y_ref, *rest, split_out):
    if split_out:
        outp_ref, outs_ref, buf, sem = rest
    else:
        out_ref, buf, sem = rest
    i = pl.program_id(0)
    row = tile_cond_ref[i]
    slot = i % 2

    def copy(sl, k, r, t):
        src = y_ref.at[pl.ds(pl.multiple_of(t, SUBLANES), ROW_TILES)]
        dst = buf.at[sl, k, pl.ds(pl.multiple_of(r * ROW_TILES, SUBLANES), ROW_TILES)]
        return pltpu.make_async_copy(src, dst, sem.at[sl])

    def gather_tile(tile, sl):
        base = tile * TM

        def issue(r, c):
            copy(sl, 0, r, d_ref[base + r]).start(priority=0)
            copy(sl, 1, r, d_ref[N_TOK + base + r]).start(priority=1)
            return c

        lax.fori_loop(0, TM, issue, 0, unroll=8)

    @pl.when(i == 0)
    def _():
        gather_tile(0, 0)

    @pl.when(i + 1 < N_TILES)
    def _():
        gather_tile(i + 1, 1 - slot)

    def drain(r, c):
        copy(slot, 0, r, 0).wait()
        copy(slot, 1, r, 0).wait()
        return c

    lax.fori_loop(0, TM, drain, 0, unroll=8)
    w = w_ref[...]
    w0 = w[:, 0:1]
    w1 = w[:, 1:2]
    parts = []
    for s in range(ROW_TILES):
        y0 = buf[slot, 0, pl.ds(s, TM, stride=ROW_TILES), :]
        y1 = buf[slot, 1, pl.ds(s, TM, stride=ROW_TILES), :]
        parts.append(w0 * y0 + w1 * y1)
    gate = mod_ref[pl.ds(row, 1), 5 * D_MODEL:6 * D_MODEL]
    out = x1_ref[...] + gate * jnp.concatenate(parts, axis=1)
    if split_out:
        @pl.when(i < N_PROMPT_TILES)
        def _():
            outp_ref[...] = out

        @pl.when(i >= N_PROMPT_TILES)
        def _():
            outs_ref[...] = out
    else:
        out_ref[...] = out


def _moe_combine(dests, x1, weights, mod_l, y, split_out):
    tile_cond, _ = _tile_cond(TM)
    if split_out:
        out_shape = [jax.ShapeDtypeStruct((N_PROMPT, D_MODEL), F32), jax.ShapeDtypeStruct((N_SAMPLE, D_MODEL), F32)]
        out_specs = list(_split_specs(TM, False))
    else:
        out_shape = jax.ShapeDtypeStruct((N_TOK, D_MODEL), F32)
        out_specs = pl.BlockSpec((TM, D_MODEL), lambda i, *_: (i, 0))
    return pl.pallas_call(
        functools.partial(_combine_kernel, split_out=split_out),
        out_shape=out_shape,
        grid_spec=pltpu.PrefetchScalarGridSpec(
            num_scalar_prefetch=2,
            grid=(N_TILES,),
            in_specs=[
                pl.BlockSpec((TM, D_MODEL), lambda i, *_: (i, 0)),
                pl.BlockSpec((TM, 2), lambda i, *_: (i, 0)),
                pl.BlockSpec((COND_ROWS, 6 * D_MODEL), lambda i, *_: (0, 0)),
                pl.BlockSpec(memory_space=pl.ANY),
            ],
            out_specs=out_specs,
            scratch_shapes=[pltpu.VMEM((2, 2, TM * ROW_TILES, LANES), F32), pltpu.SemaphoreType.DMA((2,))],
        ),
        compiler_params=pltpu.CompilerParams(
            dimension_semantics=("arbitrary",), vmem_limit_bytes=VMEM_LIMIT),
        name="moe_combine",
    )(dests, tile_cond, x1, weights, mod_l, y)


def _chunk_plan(segments):
    chunks = []
    for width, normed, dst, dst_col0, f32_dst in segments:
        for k in range(width // LANES):
            chunks.append((normed, dst, dst_col0 + k * LANES, f32_dst, k * LANES))
    return tuple(chunks)


def _head_gain(parts):
    cols = []
    for width, g, mult in parts:
        if g is None:
            cols.append(jnp.ones((width,), F32))
        else:
            cols.append(jnp.tile(g.astype(F32) * mult, width // HEAD_DIM))
    return jnp.concatenate(cols).reshape(1, -1)


def kernel(x_prompt, x_sample, cache_a_k, cache_a_v, cache_b_k, cache_b_v, cache_c_k, cache_c_v, c, c_ctx, w_mod, b_mod, norm_mix, norm_ffn, w_in_ab, w_out_ab, a_q_norm, a_k_norm, b_q_norm, b_k_norm, lam_q1, lam_k1, lam_q2, lam_k2, b_subln, w_in_c, w_out_c, c_q_norm, c_k_norm, c_rpb, w_router, b_router, w_gate, w_up, w_down):
    scale = LOG2E * HEAD_DIM ** -0.5
    cond = jnp.concatenate(
        [c, c_ctx[None, :], jnp.zeros((COND_ROWS - DEC_BATCH - 1, D_MODEL), F32)], axis=0)
    mod = _modulation(cond, w_mod, b_mod)

    xa = x_prompt.reshape(N_PROMPT, D_MODEL)
    xb = x_sample.reshape(N_SAMPLE, D_MODEL)
    merged = False
    new_caches = []
    for l in range(DEPTH):
        mod_l = mod[l]
        if l % 2 == 0:
            e = l // 2
            lam_init = 0.8 - 0.6 * math.exp(-0.3 * l)
            segments = (
                (A_Q_W, True, 0, 0, None),
                (A_KV_W, True, 1, 0, 2),
                (A_KV_W, False, 1, A_KV_W, 3),
                (B_QK_W, True, 0, A_Q_W, None),
                (B_QK_W, True, 1, 2 * A_KV_W, 4),
                (B_V_W, False, 1, 2 * A_KV_W + B_QK_W, 5),
            )
            hg = _head_gain((
                (A_Q_W, a_q_norm[e], scale), (A_KV_W, a_k_norm[e], 1.0), (A_KV_W, None, 1.0),
                (B_QK_W, b_q_norm[e], scale), (B_QK_W, b_k_norm[e], 1.0), (B_V_W, None, 1.0)))
            outs = _lnproj(xa, xb, merged, mod_l, norm_mix[l], w_in_ab[e].astype(BF16), hg, _chunk_plan(segments),
                           A_Q_W + B_QK_W, 2 * A_KV_W + B_QK_W + B_V_W,
                           (A_KV_W, A_KV_W, B_QK_W, B_V_W), True)
            q, kv, ak, av, bk, bv = outs
            new_caches.append((
                ak.reshape(BATCH, SEQ, A_KV_HEADS, HEAD_DIM), av.reshape(BATCH, SEQ, A_KV_HEADS, HEAD_DIM),
                bk.reshape(BATCH, SEQ, B_HEADS, 2, HEAD_DIM), bv.reshape(BATCH, SEQ, B_HEADS, B_V_DIM)))
            lamv = jnp.stack([lam_q1[e], lam_k1[e], lam_q2[e], lam_k2[e]]).astype(F32)
            subln = b_subln[e].reshape(1, B_V_DIM)
            o_p = _attn_ab(q, kv, lamv, subln, lam_init, None)
            caches = (cache_a_k[:, e].reshape(DEC_BATCH, PAST_LEN, A_KV_W),
                      cache_a_v[:, e].reshape(DEC_BATCH, PAST_LEN, A_KV_W),
                      cache_b_k[:, e].reshape(DEC_BATCH, PAST_LEN, B_QK_W),
                      cache_b_v[:, e].reshape(DEC_BATCH, PAST_LEN, B_V_W))
            o_s = _attn_ab(q, kv, lamv, subln, lam_init, caches)
            w_out = w_out_ab[e].astype(BF16)
        else:
            oi = l // 2
            segments = (
                (C_WIDTH, True, 0, 0, None),
                (C_WIDTH, True, 1, 0, 2),
                (C_WIDTH, False, 1, C_WIDTH, 3),
            )
            hg = _head_gain(((C_WIDTH, c_q_norm[oi], scale), (C_WIDTH, c_k_norm[oi], 1.0), (C_WIDTH, None, 1.0)))
            outs = _lnproj(xa, xb, merged, mod_l, norm_mix[l], w_in_c[oi].astype(BF16), hg, _chunk_plan(segments),
                           C_WIDTH, 2 * C_WIDTH, (C_WIDTH, C_WIDTH), False)
            q, kv, ck_new, cv_new = outs
            new_caches.append((ck_new.reshape(BATCH, SEQ, C_HEADS, HEAD_DIM),
                               cv_new.reshape(BATCH, SEQ, C_HEADS, HEAD_DIM)))
            o_p = _attn_c_prompt(q, kv)
            o_s = _attn_na(q, kv,
                           cache_c_k[:, oi].reshape(DEC_BATCH, PAST_LEN, C_WIDTH),
                           cache_c_v[:, oi].reshape(DEC_BATCH, PAST_LEN, C_WIDTH),
                           _na_bias_table(c_rpb[oi]))
            w_out = w_out_c[oi].astype(BF16)

        x1, h2, gates, rank = _outproj(o_p, o_s, xa, xb, merged, w_out, mod_l, norm_ffn[l], w_router, b_router)
        dests, weights, tile_expert, n_tiles, next_expert, clear = _route_plan(gates, rank)
        xs = _moe_scatter(dests, clear, h2)
        y = _moe_mlp(l, tile_expert, n_tiles, next_expert, xs, w_gate, w_up, w_down)
        last = l == DEPTH - 1
        out = _moe_combine(dests, x1, weights, mod_l, y, last)
        if last:
            y_prompt, y_sample = out
        else:
            xa = xb = out
            merged = True

    even = [nc for i, nc in enumerate(new_caches) if i % 2 == 0]
    odd = [nc for i, nc in enumerate(new_caches) if i % 2 == 1]
    stack = lambda items, k: jnp.stack([it[k] for it in items], axis=1)
    return (y_prompt.reshape(BATCH, SEQ, D_MODEL), y_sample.reshape(DEC_BATCH, DEC_SEQ, D_MODEL),
            stack(even, 0), stack(even, 1), stack(even, 2), stack(even, 3),
            stack(odd, 0), stack(odd, 1))
```

```python
import functools
import math

import numpy as np
import jax
import jax.numpy as jnp
from jax import lax
from jax.experimental import pallas as pl
from jax.experimental.pallas import tpu as pltpu

F32 = jnp.float32
BF16 = jnp.bfloat16

D_MODEL = 1024
BATCH = 16
SEQ = 256
DEPTH = 2
DEC_BATCH = 8
DEC_SEQ = 1024
PAST_LEN = 256
GRID_W = 64
HEAD_DIM = 64
ROPE_THETA = 10000.0
RMS_EPS = 1e-6
A_Q_HEADS = 8
A_KV_HEADS = 2
A_GROUP = A_Q_HEADS // A_KV_HEADS
B_HEADS = 4
B_V_DIM = 2 * HEAD_DIM
A_Q_W = A_Q_HEADS * HEAD_DIM
A_KV_W = A_KV_HEADS * HEAD_DIM
B_QK_W = B_HEADS * 2 * HEAD_DIM
B_V_W = B_HEADS * B_V_DIM
C_HEADS = 16
C_WIDTH = C_HEADS * HEAD_DIM
NA_ROWS = 8
NA_COLS = 16
N_EXPERTS = 16
N_GROUPS = 4
EXPERTS_PER_GROUP = N_EXPERTS // N_GROUPS
D_EXPERT = 1024

LANES = 128
SUBLANES = 8
ROW_TILES = D_MODEL // LANES
N_PROMPT = BATCH * SEQ
N_SAMPLE = DEC_BATCH * DEC_SEQ
N_TOK = N_PROMPT + N_SAMPLE
TM = 256
TM_SCATTER = 1024
N_TILES = N_TOK // TM
N_PROMPT_TILES = N_PROMPT // TM
TM_LN = 512
LN_PIECE = 512
TM_OUT = 1024
COND_ROWS = 16
CTX_COND_ROW = DEC_BATCH
MOE_TM = 256
MOE_ROWS = 2 * N_TOK + N_EXPERTS * MOE_TM
MOE_TILES = MOE_ROWS // MOE_TM
GRID_ROWS = DEC_SEQ // GRID_W
PROMPTS_PER_STEP = 2
NA_ROWS_PER_STEP = 4
NEG_BIG = -1e30
LOG2E = math.log2(math.e)
VMEM_LIMIT = 56 * 1024 * 1024

def _silu(x):
    return x * (1.0 / (1.0 + jnp.exp(-x)))


def _dot(a, b):
    return jnp.dot(a, b, preferred_element_type=F32)


def _dot_nt(a, b):
    return lax.dot_general(a, b, (((1,), (1,)), ((), ())), preferred_element_type=F32)


def _low_half(shape):
    return lax.broadcasted_iota(jnp.int32, shape, len(shape) - 1) < HEAD_DIM


def _swap_halves(x):
    return jnp.concatenate([x[:, HEAD_DIM:], x[:, :HEAD_DIM]], axis=1)


def _tile_cond(tm):
    tiles = np.arange(N_TOK // tm)
    npt = N_PROMPT // tm
    samp = np.maximum(tiles - npt, 0)
    per_seq = DEC_SEQ // tm
    cond = np.where(tiles < npt, CTX_COND_ROW, samp // per_seq)
    rope = np.where(tiles < npt, per_seq, samp % per_seq)
    return jnp.asarray(cond, jnp.int32), jnp.asarray(rope, jnp.int32)


def _split_specs(tm, merged):
    npt = N_PROMPT // tm
    base = npt if merged else 0
    a = pl.BlockSpec((tm, D_MODEL), lambda i, *_: (jnp.minimum(i, npt - 1), 0))
    b = pl.BlockSpec((tm, D_MODEL), lambda i, *_: (jnp.maximum(i - npt, 0) + base, 0))
    return a, b


def _mod_kernel(c_ref, w_ref, b_ref, o_ref):
    s = _silu(c_ref[...])
    s_hi = s.astype(BF16)
    s_lo = (s - s_hi.astype(F32)).astype(BF16)
    w = w_ref[...].astype(BF16)
    o_ref[...] = _dot(s_hi, w) + _dot(s_lo, w) + b_ref[...]


def _modulation(cond, w_mod, b_mod):
    tn = 1536
    return pl.pallas_call(
        _mod_kernel,
        out_shape=jax.ShapeDtypeStruct((DEPTH, COND_ROWS, 6 * D_MODEL), F32),
        grid=(DEPTH, 6 * D_MODEL // tn),
        in_specs=[
            pl.BlockSpec((COND_ROWS, D_MODEL), lambda l, j: (0, 0)),
            pl.BlockSpec((None, D_MODEL, tn), lambda l, j: (l, 0, j)),
            pl.BlockSpec((None, 1, tn), lambda l, j: (l, 0, j)),
        ],
        out_specs=pl.BlockSpec((None, COND_ROWS, tn), lambda l, j: (l, 0, j)),
        compiler_params=pltpu.CompilerParams(
            dimension_semantics=("arbitrary", "arbitrary"), vmem_limit_bytes=VMEM_LIMIT),
        name="modulation",
    )(cond, w_mod, b_mod.reshape(DEPTH, 1, 6 * D_MODEL))


def _rope_tables(tm):
    pos = np.arange(DEC_SEQ)
    rows = (pos // GRID_W).astype(np.float64)
    cols = (pos % GRID_W).astype(np.float64)
    nfreq = HEAD_DIM // 4
    inv = ROPE_THETA ** (-np.arange(nfreq, dtype=np.float64) / nfreq)
    d = np.arange(HEAD_DIM)
    dd = d % (HEAD_DIM // 2)
    p = np.where((d >= HEAD_DIM // 2)[None, :], cols[:, None], rows[:, None])
    ang = p * inv[dd % nfreq][None, :]
    cos, sin = np.cos(ang), np.sin(ang)
    second = (dd >= nfreq)[None, :]
    sa = np.where(second, sin, 0.0)
    sb = np.where(second, 0.0, -sin)

    def full(t, ident):
        t = np.concatenate([t, np.full((tm, HEAD_DIM), ident)], axis=0)
        return jnp.asarray(np.tile(t, (1, LANES // HEAD_DIM)), dtype=F32)

    return full(cos, 1.0), full(sa, 0.0), full(sb, 0.0)


def _lnproj_kernel(tile_cond_ref, tile_rope_ref, xa_ref, xb_ref, mod_ref, g_ref, w_ref, hg_ref, gmat_ref,
                   cos_ref, sa_ref, sb_ref, *out_refs, chunks, use_rope):
    i = pl.program_id(0)
    row = tile_cond_ref[i]
    x = jnp.where(i >= N_SAMPLE // TM_LN, xa_ref[...], xb_ref[...])
    ms = jnp.mean(x * x, axis=-1, keepdims=True)
    xn = x * lax.rsqrt(ms + RMS_EPS) * g_ref[...]
    shift = mod_ref[pl.ds(row, 1), 0:D_MODEL]
    scale = mod_ref[pl.ds(row, 1), D_MODEL:2 * D_MODEL]
    h = (xn * (1.0 + scale) + shift).astype(BF16)
    piece_chunks = LN_PIECE // LANES
    for c0 in range(0, len(chunks), piece_chunks):
        n_sub = min(piece_chunks, len(chunks) - c0)
        yp = _dot(h, w_ref[:, c0 * LANES:(c0 + n_sub) * LANES])
        yn = [None] * n_sub
        for s0 in range(0, n_sub, 2):
            if chunks[c0 + s0][0] or chunks[c0 + s0 + 1][0]:
                y2 = yp[:, s0 * LANES:(s0 + 2) * LANES]
                gs = _dot((y2 * y2).astype(BF16), gmat_ref[...])
                yn2 = y2 * lax.rsqrt(gs * (1.0 / HEAD_DIM) + RMS_EPS)
                yn[s0], yn[s0 + 1] = yn2[:, :LANES], yn2[:, LANES:]
        for sub in range(n_sub):
            c = c0 + sub
            normed, dst, dst_col, f32_dst, f32_col = chunks[c]
            if normed:
                y = yn[sub] * hg_ref[:, c * LANES:(c + 1) * LANES]
            else:
                y = yp[:, sub * LANES:(sub + 1) * LANES]
            if f32_dst is not None:
                out_refs[f32_dst][:, f32_col:f32_col + LANES] = y
            if normed and use_rope:
                y = (y * cos_ref[...] + pltpu.roll(y, HEAD_DIM // 4, 1) * sa_ref[...]
                     + pltpu.roll(y, LANES - HEAD_DIM // 4, 1) * sb_ref[...])
            out_refs[dst][:, dst_col:dst_col + LANES] = y.astype(BF16)


def _lnproj(xa, xb, merged, mod_l, gain, w_bf16, head_gain, chunks, q_w, kv_w, f32_widths, use_rope):
    dout = w_bf16.shape[1]
    tm = TM_LN
    npt = N_PROMPT // tm
    nst = N_SAMPLE // tm
    cond_tok, rope_tok = _tile_cond(tm)
    order = np.concatenate([np.arange(npt, npt + nst), np.arange(npt)])
    tile_cond, tile_rope = cond_tok[order], rope_tok[order]
    cos, sa, sb = _rope_tables(tm)
    gmat = jnp.asarray(np.kron(np.eye(2 * LANES // HEAD_DIM), np.ones((HEAD_DIM, HEAD_DIM))), dtype=BF16)
    const = lambda i, *_: (0, 0)
    tok = lambda i, *_: (jnp.where(i < nst, i + npt, i - nst), 0)
    rope_map = lambda i, tc, tr: (tr[i], 0)
    prm = lambda i, *_: (jnp.maximum(i - nst, 0), 0)
    base = npt if merged else 0
    xa_spec = pl.BlockSpec((tm, D_MODEL), prm)
    xb_spec = pl.BlockSpec((tm, D_MODEL), lambda i, *_: (jnp.minimum(i, nst - 1) + base, 0))
    out_shape = [jax.ShapeDtypeStruct((N_TOK, q_w), BF16), jax.ShapeDtypeStruct((N_TOK, kv_w), BF16)]
    out_specs = [pl.BlockSpec((tm, q_w), tok), pl.BlockSpec((tm, kv_w), tok)]
    for wd in f32_widths:
        out_shape.append(jax.ShapeDtypeStruct((N_PROMPT, wd), F32))
        out_specs.append(pl.BlockSpec((tm, wd), prm))
    return pl.pallas_call(
        functools.partial(_lnproj_kernel, chunks=chunks, use_rope=use_rope),
        out_shape=out_shape,
        grid_spec=pltpu.PrefetchScalarGridSpec(
            num_scalar_prefetch=2,
            grid=(N_TOK // tm,),
            in_specs=[
                xa_spec, xb_spec,
                pl.BlockSpec((COND_ROWS, 6 * D_MODEL), const),
                pl.BlockSpec((1, D_MODEL), const),
                pl.BlockSpec((D_MODEL, dout), const),
                pl.BlockSpec((1, dout), const),
                pl.BlockSpec((2 * LANES, 2 * LANES), const),
                pl.BlockSpec((tm, LANES), rope_map),
                pl.BlockSpec((tm, LANES), rope_map),
                pl.BlockSpec((tm, LANES), rope_map),
            ],
            out_specs=out_specs,
        ),
        compiler_params=pltpu.CompilerParams(
            dimension_semantics=("arbitrary",), vmem_limit_bytes=VMEM_LIMIT),
        name="lnproj",
    )(tile_cond, tile_rope, xa, xb, mod_l, gain.reshape(1, D_MODEL), w_bf16, head_gain, gmat, cos, sa, sb)


def _softmax_parts(s, want_sum=True):
    p = jnp.exp2(s - jnp.max(s, axis=-1, keepdims=True))
    return p, (jnp.sum(p, axis=-1, keepdims=True) if want_sum else None)


def _gqa_pairs(q_ref, q_col0, kv_ref, k_col0, v_col0, n_kv, group, tq):
    low = _low_half((tq, LANES))
    mxu_sums = group > 1
    head_out = [None] * (n_kv * group)
    jobs = [(pair, half) for pair in range(n_kv // 2) for half in range(2)]
    scores = []
    for pair, half in jobs:
        k_pair = kv_ref[:, k_col0 + pair * LANES:k_col0 + (pair + 1) * LANES]
        kvh = 2 * pair + half
        keep = low if half == 0 else jnp.logical_not(low)
        qs = []
        for g in range(group):
            head = kvh * group + g
            blk = q_ref[:, q_col0 + (head // 2) * LANES:q_col0 + (head // 2 + 1) * LANES]
            if head % 2 != half:
                blk = _swap_halves(blk)
            qs.append(jnp.where(keep, blk, jnp.zeros_like(blk)))
        q = qs[0] if group == 1 else jnp.concatenate(qs, axis=0)
        scores.append(_dot_nt(q, k_pair))

    def finish():
        for (pair, half), s in zip(jobs, scores):
            v = kv_ref[:, v_col0 + pair * LANES:v_col0 + (pair + 1) * LANES]
            kvh = 2 * pair + half
            if mxu_sums:
                own = _low_half(v.shape) if half == 0 else jnp.logical_not(_low_half(v.shape))
                v = jnp.where(own, v, jnp.ones_like(v))
            p, l = _softmax_parts(s, want_sum=not mxu_sums)
            o = _dot(p.astype(BF16), v)
            o = o * (1.0 / (pltpu.roll(o, HEAD_DIM, 1) if mxu_sums else l))
            for g in range(group):
                head = kvh * group + g
                og = o[g * tq:(g + 1) * tq]
                if head % 2 != half:
                    og = pltpu.roll(og, HEAD_DIM, 1)
                head_out[head] = og
        return [jnp.where(low, head_out[2 * k], head_out[2 * k + 1]) for k in range(n_kv * group // 2)]

    return finish


def _attn_ab_kernel(*refs, has_cache, tq, n_seq, lam_init):
    if has_cache:
        q_ref, kvn_ref, cak_ref, cav_ref, cbk_ref, cbv_ref, lamv_ref, subln_ref, o_ref, kv_ref = refs

        @pl.when(pl.program_id(1) == 0)
        def _():
            col = 0
            for c_ref in (cak_ref, cav_ref, cbk_ref, cbv_ref):
                w = c_ref.shape[1]
                kv_ref[0:PAST_LEN, col:col + w] = c_ref[...].astype(BF16)
                col += w
            kv_ref[PAST_LEN:, :] = kvn_ref[...]
    else:
        q_ref, kv_ref, lamv_ref, subln_ref, o_ref = refs

    lv = lamv_ref[...]
    l1 = jnp.sum(lv[0:1] * lv[1:2], axis=-1, keepdims=True)
    l2 = jnp.sum(lv[2:3] * lv[3:4], axis=-1, keepdims=True)
    lam = jnp.exp(l1) - jnp.exp(l2) + lam_init
    bk0 = 2 * A_KV_W
    bv0 = bk0 + B_QK_W
    low = _low_half((tq, LANES))
    t_kv = kv_ref.shape[0] // n_seq
    for b in range(n_seq):
        q_b = q_ref.at[b * tq:(b + 1) * tq]
        kv_b = kv_ref.at[b * t_kv:(b + 1) * t_kv]
        finish_a = _gqa_pairs(q_b, 0, kv_b, 0, A_KV_W, A_KV_HEADS, A_GROUP, tq)
        b_scores = []
        for h in range(B_HEADS):
            lanes = slice(h * LANES, (h + 1) * LANES)
            qp = q_b[:, A_Q_W + lanes.start:A_Q_W + lanes.stop]
            zero = jnp.zeros_like(qp)
            q = jnp.concatenate([jnp.where(low, qp, zero), jnp.where(low, zero, qp)], axis=0)
            b_scores.append(_dot_nt(q, kv_b[:, bk0 + lanes.start:bk0 + lanes.stop]))
        outs = finish_a()
        for h in range(B_HEADS):
            lanes = slice(h * LANES, (h + 1) * LANES)
            p, l = _softmax_parts(b_scores[h])
            r = 1.0 / l
            a = p[:tq] * r[:tq] - p[tq:] * (lam * r[tq:])
            o = _dot(a.astype(BF16), kv_b[:, bv0 + lanes.start:bv0 + lanes.stop])
            ms = jnp.mean(o * o, axis=-1, keepdims=True)
            o = o * lax.rsqrt(ms + RMS_EPS) * subln_ref[...] * (1.0 - lam_init)
            outs.append(o)
        o_ref[b * tq:(b + 1) * tq, :] = jnp.concatenate(outs, axis=1).astype(BF16)


def _attn_ab(q, kv, lamv, subln, lam_init, caches):
    kv_w = kv.shape[1]
    kern = functools.partial(_attn_ab_kernel, lam_init=lam_init)
    const = lambda b, j: (0, 0)
    cp = pltpu.CompilerParams(dimension_semantics=("arbitrary", "arbitrary"), vmem_limit_bytes=VMEM_LIMIT)
    if caches is None:
        n_seq = 1
        rows = n_seq * SEQ
        return pl.pallas_call(
            functools.partial(kern, has_cache=False, tq=SEQ, n_seq=n_seq),
            out_shape=jax.ShapeDtypeStruct((N_PROMPT, D_MODEL), BF16),
            grid=(BATCH // n_seq, 1),
            in_specs=[
                pl.BlockSpec((rows, D_MODEL), lambda b, j: (b, 0)),
                pl.BlockSpec((rows, kv_w), lambda b, j: (b, 0)),
                pl.BlockSpec((4, HEAD_DIM), const),
                pl.BlockSpec((1, B_V_DIM), const),
            ],
            out_specs=pl.BlockSpec((rows, D_MODEL), lambda b, j: (b, 0)),
            compiler_params=cp,
            name="attn_ab_prompt",
        )(q, kv, lamv, subln)
    tq = 256
    nq = DEC_SEQ // tq
    q0 = N_PROMPT // tq
    kv0 = N_PROMPT // DEC_SEQ
    cak, cav, cbk, cbv = caches
    cspec = lambda w: pl.BlockSpec((None, PAST_LEN, w), lambda b, j: (b, 0, 0))
    return pl.pallas_call(
        functools.partial(kern, has_cache=True, tq=tq, n_seq=1),
        out_shape=jax.ShapeDtypeStruct((N_SAMPLE, D_MODEL), BF16),
        grid=(DEC_BATCH, nq),
        in_specs=[
            pl.BlockSpec((tq, D_MODEL), lambda b, j: (q0 + b * nq + j, 0)),
            pl.BlockSpec((DEC_SEQ, kv_w), lambda b, j: (kv0 + b, 0)),
            cspec(A_KV_W), cspec(A_KV_W), cspec(B_QK_W), cspec(B_V_W),
            pl.BlockSpec((4, HEAD_DIM), const),
            pl.BlockSpec((1, B_V_DIM), const),
        ],
        out_specs=pl.BlockSpec((tq, D_MODEL), lambda b, j: (b * nq + j, 0)),
        scratch_shapes=[pltpu.VMEM((PAST_LEN + DEC_SEQ, kv_w), BF16)],
        compiler_params=cp,
        name="attn_ab_sample",
    )(q, kv, cak, cav, cbk, cbv, lamv, subln)


def _attn_c_prompt_kernel(q_ref, kv_ref, o_ref):
    for b in range(PROMPTS_PER_STEP):
        rows = slice(b * SEQ, (b + 1) * SEQ)
        outs = _gqa_pairs(q_ref.at[rows], 0, kv_ref.at[rows], 0, C_WIDTH, C_HEADS, 1, SEQ)()
        o_ref[rows, :] = jnp.concatenate(outs, axis=1).astype(BF16)


def _attn_c_prompt(q, kv):
    rows = PROMPTS_PER_STEP * SEQ
    return pl.pallas_call(
        _attn_c_prompt_kernel,
        out_shape=jax.ShapeDtypeStruct((N_PROMPT, D_MODEL), BF16),
        grid=(BATCH // PROMPTS_PER_STEP,),
        in_specs=[
            pl.BlockSpec((rows, C_WIDTH), lambda b: (b, 0)),
            pl.BlockSpec((rows, 2 * C_WIDTH), lambda b: (b, 0)),
        ],
        out_specs=pl.BlockSpec((rows, C_WIDTH), lambda b: (b, 0)),
        compiler_params=pltpu.CompilerParams(dimension_semantics=("arbitrary",), vmem_limit_bytes=VMEM_LIMIT),
        name="attn_c_prompt",
    )(q, kv)


def _na_kernel(q_ref, kv_ref, ck_ref, cv_ref, tp_ref, o_ref, ckb, cvb):
    step = pl.program_id(1)

    @pl.when(step == 0)
    def _():
        ckb[...] = ck_ref[...].astype(BF16)
        cvb[...] = cv_ref[...].astype(BF16)

    kh = min(NA_ROWS, GRID_ROWS)
    win = kh * GRID_W
    low = _low_half((GRID_W, LANES))
    n_pair = C_HEADS // 2
    for rr in range(NA_ROWS_PER_STEP):
        r = step * NA_ROWS_PER_STEP + rr
        rows = slice(rr * GRID_W, (rr + 1) * GRID_W)
        rs = jnp.clip(r - kh // 2, 0, GRID_ROWS - kh)
        ro0 = rs - r + (NA_ROWS - 1)
        start = pl.multiple_of(rs * GRID_W, GRID_W)
        s_lat, s_ctx = [], []
        for j in range(n_pair):
            lanes = slice(j * LANES, (j + 1) * LANES)
            qp = q_ref[rows, lanes]
            zero = jnp.zeros_like(qp)
            q = jnp.concatenate([jnp.where(low, qp, zero), jnp.where(low, zero, qp)], axis=0)
            kw = kv_ref[pl.ds(start, win), lanes]
            bias = jnp.concatenate(
                [jnp.concatenate([tp_ref[2 * j + hh, ro0 + 2 * t] for t in range(kh // 2)], axis=1)
                 for hh in range(2)], axis=0)
            s_lat.append(_dot_nt(q, kw) + bias)
            s_ctx.append(_dot_nt(q, ckb[:, lanes]))
        outs = []
        for j in range(n_pair):
            lanes = slice(j * LANES, (j + 1) * LANES)
            s_l, s_c = s_lat[j], s_ctx[j]
            m = jnp.maximum(jnp.max(s_l, axis=-1, keepdims=True), jnp.max(s_c, axis=-1, keepdims=True))
            p_l = jnp.exp2(s_l - m)
            p_c = jnp.exp2(s_c - m)
            l = jnp.sum(p_l, axis=-1, keepdims=True) + jnp.sum(p_c, axis=-1, keepdims=True)
            vw = kv_ref[pl.ds(start, win), C_WIDTH + lanes.start:C_WIDTH + lanes.stop]
            o = _dot(p_c.astype(BF16), cvb[:, lanes]) + _dot(p_l.astype(BF16), vw)
            o = o * (1.0 / l)
            outs.append(jnp.where(low, o[:GRID_W], o[GRID_W:]))
        o_ref[rows, :] = jnp.concatenate(outs, axis=1).astype(BF16)


NA_BIAS_PAD = GRID_W - NA_COLS


def _na_bias_kernel(w_ref, mask_ref, tp_ref):
    low = _low_half((GRID_W, LANES))
    keep = mask_ref[...] > 0.0
    n_off = 2 * NA_ROWS - 1
    left, right = [], []
    for ro in range(n_off):
        row = jnp.broadcast_to(w_ref[ro:ro + 1, :] * LOG2E, (GRID_W, LANES))
        left.append(pltpu.roll(row, LANES - GRID_W + 1, 1, stride=1, stride_axis=0))
        right.append(pltpu.roll(row, 1, 1, stride=1, stride_axis=0))
    for t in range(n_off - 1):
        tp_ref[t] = jnp.where(keep, jnp.where(low, left[t], right[t + 1]), NEG_BIG)


def _na_bias_table(rpb):
    cols = np.arange(GRID_W)
    col_start = np.clip(cols - NA_COLS // 2, 0, GRID_W - NA_COLS)
    col_in = (cols[None, :] >= col_start[:, None]) & (cols[None, :] < col_start[:, None] + NA_COLS)
    assert np.abs((cols[None, :] - cols[:, None])[col_in]).max() <= NA_COLS - 1
    mask = jnp.asarray(np.tile(col_in, (1, LANES // GRID_W)), dtype=F32)
    n_off = 2 * NA_ROWS - 1
    n_rel = 2 * NA_COLS - 1
    w = jnp.pad(rpb.astype(F32), ((0, 0), (0, 0), (NA_BIAS_PAD, LANES - NA_BIAS_PAD - n_rel)))
    return pl.pallas_call(
        _na_bias_kernel,
        out_shape=jax.ShapeDtypeStruct((C_HEADS, n_off - 1, GRID_W, LANES), F32),
        grid=(C_HEADS,),
        in_specs=[
            pl.BlockSpec((None, n_off, LANES), lambda h: (h, 0, 0)),
            pl.BlockSpec((GRID_W, LANES), lambda h: (0, 0)),
        ],
        out_specs=pl.BlockSpec((None, n_off - 1, GRID_W, LANES), lambda h: (h, 0, 0, 0)),
        compiler_params=pltpu.CompilerParams(dimension_semantics=("arbitrary",)),
        name="na_bias",
    )(w, mask)


def _attn_na(q, kv, ck, cv, tp):
    tq = NA_ROWS_PER_STEP * GRID_W
    steps = GRID_ROWS // NA_ROWS_PER_STEP
    q0 = N_PROMPT // tq
    kv0 = N_PROMPT // DEC_SEQ
    return pl.pallas_call(
        _na_kernel,
        out_shape=jax.ShapeDtypeStruct((N_SAMPLE, D_MODEL), BF16),
        grid=(DEC_BATCH, steps),
        in_specs=[
            pl.BlockSpec((tq, C_WIDTH), lambda b, r: (q0 + b * steps + r, 0)),
            pl.BlockSpec((DEC_SEQ, 2 * C_WIDTH), lambda b, r: (kv0 + b, 0)),
            pl.BlockSpec((None, PAST_LEN, C_WIDTH), lambda b, r: (b, 0, 0)),
            pl.BlockSpec((None, PAST_LEN, C_WIDTH), lambda b, r: (b, 0, 0)),
            pl.BlockSpec(tp.shape, lambda b, r: (0, 0, 0, 0)),
        ],
        out_specs=pl.BlockSpec((tq, C_WIDTH), lambda b, r: (b * steps + r, 0)),
        scratch_shapes=[pltpu.VMEM((PAST_LEN, C_WIDTH), BF16), pltpu.VMEM((PAST_LEN, C_WIDTH), BF16)],
        compiler_params=pltpu.CompilerParams(
            dimension_semantics=("arbitrary", "arbitrary"), vmem_limit_bytes=VMEM_LIMIT),
        name="attn_na_sample",
    )(q, kv, ck, cv, tp)


def _first_wins_ranks(vals):
    ranks = []
    for i in range(len(vals)):
        r = jnp.zeros_like(vals[i])
        for j in range(len(vals)):
            if j == i:
                continue
            beats = (vals[j] >= vals[i]) if j < i else (vals[j] > vals[i])
            r = r + jnp.where(beats, 1.0, 0.0)
        ranks.append(r)
    return ranks


def _outproj_kernel(tile_cond_ref, op_ref, os_ref, xa_ref, xb_ref, w_ref, mod_ref, g2_ref, wrt_ref, br_ref,
                    tri_ref, x1_ref, h2_ref, gates_ref, rank_ref, carry_ref):
    i = pl.program_id(0)
    row = tile_cond_ref[i]
    is_prompt = i < N_PROMPT // TM_OUT

    @pl.when(i == 0)
    def _():
        carry_ref[...] = jnp.zeros_like(carry_ref)

    o = jnp.where(is_prompt, op_ref[...], os_ref[...])
    x = jnp.where(is_prompt, xa_ref[...], xb_ref[...])
    acc = _dot(o, w_ref[...])
    gate = mod_ref[pl.ds(row, 1), 2 * D_MODEL:3 * D_MODEL]
    x1 = x + gate * acc
    x1_ref[...] = x1
    ms = jnp.mean(x1 * x1, axis=-1, keepdims=True)
    xn = x1 * lax.rsqrt(ms + RMS_EPS) * g2_ref[...]
    shift = mod_ref[pl.ds(row, 1), 3 * D_MODEL:4 * D_MODEL]
    scale = mod_ref[pl.ds(row, 1), 4 * D_MODEL:5 * D_MODEL]
    h2 = xn * (1.0 + scale) + shift
    for s in range(ROW_TILES):
        h2_ref[pl.ds(s, TM_OUT, stride=ROW_TILES), :] = h2[:, s * LANES:(s + 1) * LANES]

    h_hi = h2.astype(BF16)
    h_lo = (h2 - h_hi.astype(F32)).astype(BF16)
    part = _dot_nt(wrt_ref[...], h_hi)
    logits = part[:N_EXPERTS] + part[N_EXPERTS:] + _dot_nt(wrt_ref[:N_EXPERTS, :], h_lo)
    e = jnp.exp(logits - jnp.max(logits, axis=0, keepdims=True))
    scores = e * (1.0 / jnp.sum(e, axis=0, keepdims=True))
    sel = scores + br_ref[...]
    sel_rows = [sel[k:k + 1, :] for k in range(N_EXPERTS)]
    in_top2 = []
    group_sum = []
    for g in range(N_GROUPS):
        vals = sel_rows[g * EXPERTS_PER_GROUP:(g + 1) * EXPERTS_PER_GROUP]
        ranks = _first_wins_ranks(vals)
        top = [rk < 2.0 for rk in ranks]
        in_top2.extend(top)
        s = jnp.zeros_like(vals[0])
        for v, t in zip(vals, top):
            s = s + jnp.where(t, v, 0.0)
        group_sum.append(s)
    group_rank = _first_wins_ranks(group_sum)
    mask_rows = []
    for k in range(N_EXPERTS):
        chosen = jnp.where(in_top2[k], 1.0, 0.0) * jnp.where(group_rank[k // EXPERTS_PER_GROUP] < 1.0, 1.0, 0.0)
        mask_rows.append(chosen)
    mask = jnp.concatenate(mask_rows, axis=0)
    picked = scores * mask
    gates_ref[...] = picked * (1.0 / jnp.sum(picked, axis=0, keepdims=True))
    prefix = _dot(mask.astype(BF16), tri_ref[...])
    rank_ref[...] = jnp.where(mask > 0.0, prefix + carry_ref[...], -1.0)
    carry_ref[...] = carry_ref[...] + jnp.sum(mask, axis=1, keepdims=True)


def _outproj(o_prompt, o_sample, xa, xb, merged, w_bf16, mod_l, gain2, w_router, b_router):
    const = lambda i, *_: (0, 0)
    tok = lambda i, *_: (i, 0)
    tokT = lambda i, *_: (0, i)
    tm = TM_OUT
    tile_cond, _ = _tile_cond(tm)
    op_spec, os_spec = _split_specs(tm, False)
    xa_spec, xb_spec = _split_specs(tm, merged)
    tri = jnp.asarray(np.triu(np.ones((tm, tm)), k=1), dtype=BF16)
    wrt = w_router.T.astype(F32)
    wrt_hi = wrt.astype(BF16)
    wrt_split = jnp.concatenate([wrt_hi, (wrt - wrt_hi.astype(F32)).astype(BF16)], axis=0)
    return pl.pallas_call(
        _outproj_kernel,
        out_shape=[
            jax.ShapeDtypeStruct((N_TOK, D_MODEL), F32),
            jax.ShapeDtypeStruct((N_TOK * ROW_TILES, LANES), F32),
            jax.ShapeDtypeStruct((N_EXPERTS, N_TOK), F32),
            jax.ShapeDtypeStruct((N_EXPERTS, N_TOK), F32),
        ],
        grid_spec=pltpu.PrefetchScalarGridSpec(
            num_scalar_prefetch=1,
            grid=(N_TOK // tm,),
            in_specs=[
                op_spec, os_spec, xa_spec, xb_spec,
                pl.BlockSpec((D_MODEL, D_MODEL), const),
                pl.BlockSpec((COND_ROWS, 6 * D_MODEL), const),
                pl.BlockSpec((1, D_MODEL), const),
                pl.BlockSpec((2 * N_EXPERTS, D_MODEL), const),
                pl.BlockSpec((N_EXPERTS, 1), const),
                pl.BlockSpec((tm, tm), const),
            ],
            out_specs=[
                pl.BlockSpec((tm, D_MODEL), tok),
                pl.BlockSpec((tm * ROW_TILES, LANES), tok),
                pl.BlockSpec((N_EXPERTS, tm), tokT),
                pl.BlockSpec((N_EXPERTS, tm), tokT),
            ],
            scratch_shapes=[pltpu.VMEM((N_EXPERTS, 1), F32)],
        ),
        compiler_params=pltpu.CompilerParams(
            dimension_semantics=("arbitrary",), vmem_limit_bytes=VMEM_LIMIT),
        name="outproj_router",
    )(tile_cond, o_prompt, o_sample, xa, xb, w_bf16, mod_l, gain2.reshape(1, D_MODEL), wrt_split,
      b_router.reshape(N_EXPERTS, 1), tri)


def _plan_kernel(gates_ref, rank_ref, dests_ref, wts_ref, last_ref, misc_ref):
    rank = rank_ref[...]
    gates = gates_ref[...]
    sel = rank >= 0.0
    hit = jnp.where(sel, 1.0, 0.0)
    tile = float(MOE_TM)

    def pad_up(c):
        return jnp.floor((c + (tile - 1.0)) * (1.0 / tile)) * tile

    counts_col = jnp.sum(hit, axis=1, keepdims=True)
    ones = jnp.ones((SUBLANES, hit.shape[1]), BF16)
    counts_row = _dot_nt(ones, hit.astype(BF16))[0:1]
    padded_col = pad_up(counts_col)
    padded_row = pad_up(counts_row)
    e_sub = lax.broadcasted_iota(jnp.int32, (N_EXPERTS, N_EXPERTS), 0)
    e_lane = lax.broadcasted_iota(jnp.int32, (N_EXPERTS, N_EXPERTS), 1)
    ends_col = jnp.sum(jnp.where(e_lane <= e_sub, padded_row, 0.0), axis=1, keepdims=True)
    ends_row = jnp.sum(jnp.where(e_sub <= e_lane, padded_col, 0.0), axis=0, keepdims=True)
    dest = jnp.where(sel, ends_col - padded_col + rank, -1.0)
    d1 = jnp.max(dest, axis=0, keepdims=True)
    d0 = jnp.min(jnp.where(sel, dest, float(MOE_ROWS)), axis=0, keepdims=True)
    w0 = jnp.sum(jnp.where(dest == d0, gates, 0.0), axis=0, keepdims=True)
    w1 = jnp.sum(jnp.where(dest == d1, gates, 0.0), axis=0, keepdims=True)
    dests_ref[...] = (jnp.concatenate([d0, d1], axis=0) * float(ROW_TILES)).astype(jnp.int32)
    wts_ref[...] = jnp.concatenate([w0, w1], axis=0)

    last = jnp.where(padded_col > 0.0, (ends_col - tile) * float(ROW_TILES), -1.0)
    last_ref[...] = jnp.broadcast_to(last, last_ref.shape).astype(jnp.int32)

    lane = lax.broadcasted_iota(jnp.int32, (1, LANES), 1).astype(F32)
    n_tiles = ends_row[:, N_EXPERTS - 1:N_EXPERTS] * (1.0 / tile)
    tile_expert = jnp.sum(jnp.where(lane * tile >= ends_col, 1.0, 0.0), axis=0, keepdims=True)
    tile_expert = jnp.minimum(tile_expert, float(N_EXPERTS - 1))
    e_col = lax.broadcasted_iota(jnp.int32, (N_EXPERTS, 1), 0).astype(F32)
    later = jnp.where((e_col > tile_expert) & (padded_col > 0.0), e_col, float(N_EXPERTS))
    nxt = jnp.min(later, axis=0, keepdims=True)
    nxt = jnp.where(nxt > float(N_EXPERTS - 1), tile_expert, nxt)
    spare = n_tiles + lane
    spare = jnp.where(spare < float(MOE_TILES), spare * (tile * ROW_TILES), -1.0)
    zero = jnp.zeros((1, LANES), F32)
    rows = [tile_expert, nxt, spare, n_tiles + zero] + [zero] * (SUBLANES - 4)
    misc_ref[...] = jnp.concatenate(rows, axis=0).astype(jnp.int32)


def _route_plan(gates, rank):
    assert MOE_TILES <= LANES and N_EXPERTS <= LANES
    full = lambda shape: pl.BlockSpec(shape, lambda: (0,) * len(shape))
    dests, wts, last, misc = pl.pallas_call(
        _plan_kernel,
        out_shape=[
            jax.ShapeDtypeStruct((2, N_TOK), jnp.int32),
            jax.ShapeDtypeStruct((2, N_TOK), F32),
            jax.ShapeDtypeStruct((N_EXPERTS, LANES), jnp.int32),
            jax.ShapeDtypeStruct((SUBLANES, LANES), jnp.int32),
        ],
        in_specs=[full((N_EXPERTS, N_TOK)), full((N_EXPERTS, N_TOK))],
        out_specs=[full((2, N_TOK)), full((2, N_TOK)), full((N_EXPERTS, LANES)), full((SUBLANES, LANES))],
        compiler_params=pltpu.CompilerParams(vmem_limit_bytes=VMEM_LIMIT),
        name="route_plan",
    )(gates, rank)
    tile_expert = misc[0, :MOE_TILES]
    next_expert = misc[1, :MOE_TILES]
    n_tiles = misc[3, :1]
    clear = jnp.concatenate([last[:, 0], misc[2, :N_EXPERTS]])
    return dests.reshape(2 * N_TOK), wts.T, tile_expert, n_tiles, next_expert, clear


def _scatter_kernel(d_ref, last_ref, h_ref, xs_ref, zeros, sem, zsem):
    i = pl.program_id(0)
    base = i * TM_SCATTER
    rows = MOE_TM * ROW_TILES

    @pl.when(i == 0)
    def _():
        zeros[...] = jnp.zeros_like(zeros)
        for e in range(2 * N_EXPERTS):
            @pl.when(last_ref[e] >= 0)
            def _():
                at = pl.multiple_of(last_ref[e], SUBLANES)
                pltpu.make_async_copy(zeros, xs_ref.at[pl.ds(at, rows)], zsem).start()
        for e in range(2 * N_EXPERTS):
            @pl.when(last_ref[e] >= 0)
            def _():
                pltpu.make_async_copy(zeros, xs_ref.at[pl.ds(0, rows)], zsem).wait()

    def copy(r, t):
        src = h_ref.at[pl.ds(pl.multiple_of(r * ROW_TILES, SUBLANES), ROW_TILES)]
        dst = xs_ref.at[pl.ds(pl.multiple_of(t, SUBLANES), ROW_TILES)]
        return pltpu.make_async_copy(src, dst, sem)

    def issue(r, c):
        copy(r, d_ref[base + r]).start(priority=0)
        copy(r, d_ref[N_TOK + base + r]).start(priority=1)
        return c

    lax.fori_loop(0, TM_SCATTER, issue, 0, unroll=8)

    def drain(r, c):
        copy(r, 0).wait()
        copy(r, 0).wait()
        return c

    lax.fori_loop(0, TM_SCATTER, drain, 0, unroll=8)


def _moe_scatter(dests, last_tile, h2):
    return pl.pallas_call(
        _scatter_kernel,
        out_shape=jax.ShapeDtypeStruct((MOE_ROWS * ROW_TILES, LANES), F32),
        grid_spec=pltpu.PrefetchScalarGridSpec(
            num_scalar_prefetch=2,
            grid=(N_TOK // TM_SCATTER,),
            in_specs=[pl.BlockSpec((TM_SCATTER * ROW_TILES, LANES), lambda i, *_: (i, 0))],
            out_specs=pl.BlockSpec(memory_space=pl.ANY),
            scratch_shapes=[
                pltpu.VMEM((MOE_TM * ROW_TILES, LANES), F32),
                pltpu.SemaphoreType.DMA,
                pltpu.SemaphoreType.DMA,
            ],
        ),
        compiler_params=pltpu.CompilerParams(
            dimension_semantics=("arbitrary",), vmem_limit_bytes=VMEM_LIMIT, has_side_effects=True),
        name="moe_scatter",
    )(dests, last_tile, h2)


def _mlp_kernel(te_ref, nv_ref, nxt_ref, xs_ref, wg_hbm, wu_hbm, wd_hbm, y_ref, stage, wgb, wub, wdb, hb, sem,
                *, layer):
    t = pl.program_id(0)

    def fetch(e):
        return [pltpu.make_async_copy(w.at[layer, e], stage.at[k], sem.at[k])
                for k, w in enumerate((wg_hbm, wu_hbm, wd_hbm))]

    @pl.when(t < nv_ref[0])
    def _():
        e = te_ref[t]
        prev = te_ref[jnp.maximum(t - 1, 0)]

        @pl.when(t == 0)
        def _():
            for cp in fetch(e):
                cp.start()

        @pl.when((t == 0) | (e != prev))
        def _():
            for cp in fetch(e):
                cp.wait()
            wgb[...] = stage[0].astype(BF16)
            wub[...] = stage[1].astype(BF16)
            wdb[...] = stage[2].astype(BF16)

            @pl.when(nxt_ref[t] != e)
            def _():
                for cp in fetch(nxt_ref[t]):
                    cp.start()

        x = jnp.concatenate(
            [xs_ref[pl.ds(s, MOE_TM, stride=ROW_TILES), :] for s in range(ROW_TILES)], axis=1).astype(BF16)
        step = 512
        for c in range(0, D_EXPERT, step):
            g = _dot(x, wgb[:, c:c + step])
            u = _dot(x, wub[:, c:c + step])
            hb[:, c:c + step] = (_silu(g) * u).astype(BF16)
        y = _dot(hb[...], wdb[...])
        for s in range(ROW_TILES):
            y_ref[pl.ds(s, MOE_TM, stride=ROW_TILES), :] = y[:, s * LANES:(s + 1) * LANES]

    @pl.when(t >= nv_ref[0])
    def _():
        y_ref[...] = jnp.zeros_like(y_ref)


def _moe_mlp(layer, tile_expert, n_tiles, next_expert, xs, w_gate, w_up, w_down):
    assert D_MODEL == D_EXPERT

    def tile_map(t, te, nv, nx):
        return (jnp.minimum(t, nv[0] - 1), 0)

    def out_map(t, te, nv, nx):
        return (t, 0)

    hbm = pl.BlockSpec(memory_space=pl.ANY)
    return pl.pallas_call(
        functools.partial(_mlp_kernel, layer=layer),
        out_shape=jax.ShapeDtypeStruct((MOE_ROWS * ROW_TILES, LANES), F32),
        grid_spec=pltpu.PrefetchScalarGridSpec(
            num_scalar_prefetch=3,
            grid=(MOE_TILES,),
            in_specs=[pl.BlockSpec((MOE_TM * ROW_TILES, LANES), tile_map), hbm, hbm, hbm],
            out_specs=pl.BlockSpec((MOE_TM * ROW_TILES, LANES), out_map),
            scratch_shapes=[
                pltpu.VMEM((3, D_MODEL, D_EXPERT), F32),
                pltpu.VMEM((D_MODEL, D_EXPERT), BF16),
                pltpu.VMEM((D_MODEL, D_EXPERT), BF16),
                pltpu.VMEM((D_EXPERT, D_MODEL), BF16),
                pltpu.VMEM((MOE_TM, D_EXPERT), BF16),
                pltpu.SemaphoreType.DMA((3,)),
            ],
        ),
        compiler_params=pltpu.CompilerParams(
            dimension_semantics=("arbitrary",), vmem_limit_bytes=VMEM_LIMIT),
        name="moe_mlp",
    )(tile_expert, n_tiles, next_expert, xs, w_gate, w_up, w_down)


def _combine_kernel(d_ref, tile_cond_ref, x1_ref, w_ref, mod_ref, y_ref, *rest, split_out):
    if split_out:
        outp_ref, outs_ref, buf, sem = rest
    else:
        out_ref, buf, sem = rest
    i = pl.program_id(0)
    row = tile_cond_ref[i]
    slot = i % 2

    def copy(sl, k, r, t):
        src = y_ref.at[pl.ds(pl.multiple_of(t, SUBLANES), ROW_TILES)]
        dst = buf.at[sl, k, pl.ds(pl.multiple_of(r * ROW_TILES, SUBLANES), ROW_TILES)]
        return pltpu.make_async_copy(src, dst, sem.at[sl])

    def gather_tile(tile, sl):
        base = tile * TM

        def issue(r, c):
            copy(sl, 0, r, d_ref[base + r]).start(priority=0)
            copy(sl, 1, r, d_ref[N_TOK + base + r]).start(priority=1)
            return c

        lax.fori_loop(0, TM, issue, 0, unroll=8)

    @pl.when(i == 0)
    def _():
        gather_tile(0, 0)

    @pl.when(i + 1 < N_TILES)
    def _():
        gather_tile(i + 1, 1 - slot)

    def drain(r, c):
        copy(slot, 0, r, 0).wait()
        copy(slot, 1, r, 0).wait()
        return c

    lax.fori_loop(0, TM, drain, 0, unroll=8)
    w = w_ref[...]
    w0 = w[:, 0:1]
    w1 = w[:, 1:2]
    parts = []
    for s in range(ROW_TILES):
        y0 = buf[slot, 0, pl.ds(s, TM, stride=ROW_TILES), :]
        y1 = buf[slot, 1, pl.ds(s, TM, stride=ROW_TILES), :]
        parts.append(w0 * y0 + w1 * y1)
    gate = mod_ref[pl.ds(row, 1), 5 * D_MODEL:6 * D_MODEL]
    out = x1_ref[...] + gate * jnp.concatenate(parts, axis=1)
    if split_out:
        @pl.when(i < N_PROMPT_TILES)
        def _():
            outp_ref[...] = out

        @pl.when(i >= N_PROMPT_TILES)
        def _():
            outs_ref[...] = out
    else:
        out_ref[...] = out


def _moe_combine(dests, x1, weights, mod_l, y, split_out):
    tile_cond, _ = _tile_cond(TM)
    if split_out:
        out_shape = [jax.ShapeDtypeStruct((N_PROMPT, D_MODEL), F32), jax.ShapeDtypeStruct((N_SAMPLE, D_MODEL), F32)]
        out_specs = list(_split_specs(TM, False))
    else:
        out_shape = jax.ShapeDtypeStruct((N_TOK, D_MODEL), F32)
        out_specs = pl.BlockSpec((TM, D_MODEL), lambda i, *_: (i, 0))
    return pl.pallas_call(
        functools.partial(_combine_kernel, split_out=split_out),
        out_shape=out_shape,
        grid_spec=pltpu.PrefetchScalarGridSpec(
            num_scalar_prefetch=2,
            grid=(N_TILES,),
            in_specs=[
                pl.BlockSpec((TM, D_MODEL), lambda i, *_: (i, 0)),
                pl.BlockSpec((TM, 2), lambda i, *_: (i, 0)),
                pl.BlockSpec((COND_ROWS, 6 * D_MODEL), lambda i, *_: (0, 0)),
                pl.BlockSpec(memory_space=pl.ANY),
            ],
            out_specs=out_specs,
            scratch_shapes=[pltpu.VMEM((2, 2, TM * ROW_TILES, LANES), F32), pltpu.SemaphoreType.DMA((2,))],
        ),
        compiler_params=pltpu.CompilerParams(
            dimension_semantics=("arbitrary",), vmem_limit_bytes=VMEM_LIMIT),
        name="moe_combine",
    )(dests, tile_cond, x1, weights, mod_l, y)


def _chunk_plan(segments):
    chunks = []
    for width, normed, dst, dst_col0, f32_dst in segments:
        for k in range(width // LANES):
            chunks.append((normed, dst, dst_col0 + k * LANES, f32_dst, k * LANES))
    return tuple(chunks)


def _head_gain(parts):
    cols = []
    for width, g, mult in parts:
        if g is None:
            cols.append(jnp.ones((width,), F32))
        else:
            cols.append(jnp.tile(g.astype(F32) * mult, width // HEAD_DIM))
    return jnp.concatenate(cols).reshape(1, -1)


def kernel(x_prompt, x_sample, cache_a_k, cache_a_v, cache_b_k, cache_b_v, cache_c_k, cache_c_v, c, c_ctx, w_mod, b_mod, norm_mix, norm_ffn, w_in_ab, w_out_ab, a_q_norm, a_k_norm, b_q_norm, b_k_norm, lam_q1, lam_k1, lam_q2, lam_k2, b_subln, w_in_c, w_out_c, c_q_norm, c_k_norm, c_rpb, w_router, b_router, w_gate, w_up, w_down):
    scale = LOG2E * HEAD_DIM ** -0.5
    cond = jnp.concatenate(
        [c, c_ctx[None, :], jnp.zeros((COND_ROWS - DEC_BATCH - 1, D_MODEL), F32)], axis=0)
    mod = _modulation(cond, w_mod, b_mod)

    xa = x_prompt.reshape(N_PROMPT, D_MODEL)
    xb = x_sample.reshape(N_SAMPLE, D_MODEL)
    merged = False
    new_caches = []
    for l in range(DEPTH):
        mod_l = mod[l]
        if l % 2 == 0:
            e = l // 2
            lam_init = 0.8 - 0.6 * math.exp(-0.3 * l)
            segments = (
                (A_Q_W, True, 0, 0, None),
                (A_KV_W, True, 1, 0, 2),
                (A_KV_W, False, 1, A_KV_W, 3),
                (B_QK_W, True, 0, A_Q_W, None),
                (B_QK_W, True, 1, 2 * A_KV_W, 4),
                (B_V_W, False, 1, 2 * A_KV_W + B_QK_W, 5),
            )
            hg = _head_gain((
                (A_Q_W, a_q_norm[e], scale), (A_KV_W, a_k_norm[e], 1.0), (A_KV_W, None, 1.0),
                (B_QK_W, b_q_norm[e], scale), (B_QK_W, b_k_norm[e], 1.0), (B_V_W, None, 1.0)))
            outs = _lnproj(xa, xb, merged, mod_l, norm_mix[l], w_in_ab[e].astype(BF16), hg, _chunk_plan(segments),
                           A_Q_W + B_QK_W, 2 * A_KV_W + B_QK_W + B_V_W,
                           (A_KV_W, A_KV_W, B_QK_W, B_V_W), True)
            q, kv, ak, av, bk, bv = outs
            new_caches.append((
                ak.reshape(BATCH, SEQ, A_KV_HEADS, HEAD_DIM), av.reshape(BATCH, SEQ, A_KV_HEADS, HEAD_DIM),
                bk.reshape(BATCH, SEQ, B_HEADS, 2, HEAD_DIM), bv.reshape(BATCH, SEQ, B_HEADS, B_V_DIM)))
            lamv = jnp.stack([lam_q1[e], lam_k1[e], lam_q2[e], lam_k2[e]]).astype(F32)
            subln = b_subln[e].reshape(1, B_V_DIM)
            o_p = _attn_ab(q, kv, lamv, subln, lam_init, None)
            caches = (cache_a_k[:, e].reshape(DEC_BATCH, PAST_LEN, A_KV_W),
                      cache_a_v[:, e].reshape(DEC_BATCH, PAST_LEN, A_KV_W),
                      cache_b_k[:, e].reshape(DEC_BATCH, PAST_LEN, B_QK_W),
                      cache_b_v[:, e].reshape(DEC_BATCH, PAST_LEN, B_V_W))
            o_s = _attn_ab(q, kv, lamv, subln, lam_init, caches)
            w_out = w_out_ab[e].astype(BF16)
        else:
            oi = l // 2
            segments = (
                (C_WIDTH, True, 0, 0, None),
                (C_WIDTH, True, 1, 0, 2),
                (C_WIDTH, False, 1, C_WIDTH, 3),
            )
            hg = _head_gain(((C_WIDTH, c_q_norm[oi], scale), (C_WIDTH, c_k_norm[oi], 1.0), (C_WIDTH, None, 1.0)))
            outs = _lnproj(xa, xb, merged, mod_l, norm_mix[l], w_in_c[oi].astype(BF16), hg, _chunk_plan(segments),
                           C_WIDTH, 2 * C_WIDTH, (C_WIDTH, C_WIDTH), False)
            q, kv, ck_new, cv_new = outs
            new_caches.append((ck_new.reshape(BATCH, SEQ, C_HEADS, HEAD_DIM),
                               cv_new.reshape(BATCH, SEQ, C_HEADS, HEAD_DIM)))
            o_p = _attn_c_prompt(q, kv)
            o_s = _attn_na(q, kv,
                           cache_c_k[:, oi].reshape(DEC_BATCH, PAST_LEN, C_WIDTH),
                           cache_c_v[:, oi].reshape(DEC_BATCH, PAST_LEN, C_WIDTH),
                           _na_bias_table(c_rpb[oi]))
            w_out = w_out_c[oi].astype(BF16)

        x1, h2, gates, rank = _outproj(o_p, o_s, xa, xb, merged, w_out, mod_l, norm_ffn[l], w_router, b_router)
        dests, weights, tile_expert, n_tiles, next_expert, clear = _route_plan(gates, rank)
        xs = _moe_scatter(dests, clear, h2)
        y = _moe_mlp(l, tile_expert, n_tiles, next_expert, xs, w_gate, w_up, w_down)
        last = l == DEPTH - 1
        out = _moe_combine(dests, x1, weights, mod_l, y, last)
        if last:
            y_prompt, y_sample = out
        else:
            xa = xb = out
            merged = True

    even = [nc for i, nc in enumerate(new_caches) if i % 2 == 0]
    odd = [nc for i, nc in enumerate(new_caches) if i % 2 == 1]
    stack = lambda items, k: jnp.stack([it[k] for it in items], axis=1)
    return (y_prompt.reshape(BATCH, SEQ, D_MODEL), y_sample.reshape(DEC_BATCH, DEC_SEQ, D_MODEL),
            stack(even, 0), stack(even, 1), stack(even, 2), stack(even, 3),
            stack(odd, 0), stack(odd, 1))
```
